```python
import jax, jax.numpy as jnp
from jax import lax
import numpy as np

D_MODEL = 2048
BATCH = 4
SEQ = 2048
DEPTH = 1

GMLP_GROUPS = 8
GMLP_GROUP_DIM = 128
GMLP_WIDTH = GMLP_GROUPS * GMLP_GROUP_DIM
GMLP_CHUNK = 128
HG_HEADS = 8
HG_DK = 128
HG_DV = 128
HG_WIDTH = HG_HEADS * HG_DK
HG_CHUNK = 64
FFN_HIDDEN = -(-8 * D_MODEL // (3 * 256)) * 256
IN_WIDTH = 2 * GMLP_WIDTH + 4 * HG_WIDTH + 2 * D_MODEL
EPS = 1e-6

kernel_name = "hybrid_gmlp_hgrn2_adaln_block"


def rmsnorm(x, g):
    xf = x.astype(jnp.float32)
    y = xf * lax.rsqrt(jnp.mean(xf * xf, axis=-1, keepdims=True) + EPS)
    return (y * g.astype(jnp.float32)).astype(x.dtype)


def layernorm(x, g, b):
    xf = x.astype(jnp.float32)
    mu = jnp.mean(xf, axis=-1, keepdims=True)
    var = jnp.mean(jnp.square(xf - mu), axis=-1, keepdims=True)
    y = (xf - mu) * lax.rsqrt(var + EPS)
    return (y * g.astype(jnp.float32) + b.astype(jnp.float32)).astype(x.dtype)


def modulate(h, shift, scale):
    return h * (1 + scale[:, None, :]) + shift[:, None, :]


def gmlp_branch(u, v, ln_g, ln_b, ws, bs):
    B, S, _ = v.shape
    nc = S // GMLP_CHUNK
    v = layernorm(v, ln_g, ln_b)
    vc = v.reshape(B, nc, GMLP_CHUNK, GMLP_GROUPS, GMLP_GROUP_DIM)
    mask = jnp.tril(jnp.ones((GMLP_CHUNK, GMLP_CHUNK), dtype=bool))
    w = jnp.where(mask[None], ws, 0).astype(v.dtype)
    s = jnp.einsum('gts,bcsgd->bctgd', w, vc)
    s = s + jnp.transpose(bs)[None, None, :, :, None].astype(v.dtype)
    return u * s.reshape(B, S, GMLP_WIDTH)


def hgrn2_branch(q, f_pre, i, g, lb, norm_g):
    B, S, _ = q.shape
    dt = q.dtype
    f = lb + (1.0 - lb) * jax.nn.sigmoid(f_pre.astype(jnp.float32))
    log_f = jnp.log(f)
    k = 1.0 - f
    qa = jax.nn.silu(q.astype(jnp.float32))
    nc = S // HG_CHUNK

    def heads(t, d):
        return t.reshape(B, nc, HG_CHUNK, HG_HEADS, d).transpose(1, 0, 3, 2, 4)

    qc, kc, lfc = heads(qa, HG_DK), heads(k, HG_DK), heads(log_f, HG_DK)
    ic = heads(i.astype(jnp.float32), HG_DV)
    mask = jnp.tril(jnp.ones((HG_CHUNK, HG_CHUNK), dtype=bool))

    def step(state, inp):
        qt, kt, it, lft = inp
        b = jnp.cumsum(lft, axis=2)
        o_inter = jnp.einsum('bhtk,bhkv->bhtv', qt * jnp.exp(b), state)
        rel = b[:, :, :, None, :] - b[:, :, None, :, :]
        decay = jnp.exp(jnp.where(mask[:, :, None], rel, -jnp.inf))
        attn = jnp.einsum('bhtk,bhsk,bhtsk->bhts', qt, kt, decay)
        o = o_inter + jnp.einsum('bhts,bhsv->bhtv', attn, it)
        b_last = b[:, :, -1:, :]
        new_state = jnp.exp(b_last[:, :, 0, :])[..., None] * state + jnp.einsum(
            'bhsk,bhsv->bhkv', kt * jnp.exp(b_last - b), it)
        return new_state, o

    s0 = jnp.zeros((B, HG_HEADS, HG_DK, HG_DV), jnp.float32)
    _, o = lax.scan(step, s0, (qc, kc, ic, lfc))
    o = o.transpose(1, 0, 3, 2, 4).reshape(B, S, HG_HEADS, HG_DV)
    o = o * lax.rsqrt(jnp.mean(o * o, axis=-1, keepdims=True) + EPS)
    o = o.reshape(B, S, HG_HEADS * HG_DV) * norm_g.astype(jnp.float32) * jax.nn.silu(g.astype(jnp.float32))
    return o.astype(dt)


def setup_inputs(seed: int = 0) -> dict:
    key = jax.random.key(seed)
    ks = jax.random.split(key, 24)
    f32 = jnp.float32
    nrm = lambda k, shape, s: jax.random.normal(k, shape, f32) * s
    L, D = DEPTH, D_MODEL
    return {
        "x": nrm(ks[0], (BATCH, SEQ, D), 1.0),
        "c": nrm(ks[1], (BATCH, D), 1.0),
        "w_ada": nrm(ks[2], (L, D, 6 * D), 0.5 * D ** -0.5),
        "b_ada": nrm(ks[3], (L, 6 * D), 0.02),
        "norm1_g": 1.0 + nrm(ks[4], (L, D), 0.02),
        "w_in": nrm(ks[5], (L, D, IN_WIDTH), D ** -0.5),
        "b_gate": nrm(ks[6], (L, 2 * D), 0.02),
        "gmlp_ln_g": 1.0 + nrm(ks[7], (L, GMLP_WIDTH), 0.02),
        "gmlp_ln_b": nrm(ks[8], (L, GMLP_WIDTH), 0.02),
        "gmlp_ws": nrm(ks[9], (L, GMLP_GROUPS, GMLP_CHUNK, GMLP_CHUNK), GMLP_CHUNK ** -0.5),
        "gmlp_bs": 1.0 + nrm(ks[10], (L, GMLP_GROUPS, GMLP_CHUNK), 0.1),
        "hg_lb": nrm(ks[11], (DEPTH + 1, HG_WIDTH), 1.0),
        "hg_norm_g": 1.0 + nrm(ks[12], (L, HG_WIDTH), 0.02),
        "w_branch_gmlp": nrm(ks[13], (L, GMLP_WIDTH, D), GMLP_WIDTH ** -0.5),
        "w_branch_hg": nrm(ks[14], (L, HG_WIDTH, D), HG_WIDTH ** -0.5),
        "w_out": nrm(ks[15], (L, D, D), D ** -0.5),
        "norm2_g": 1.0 + nrm(ks[16], (L, D), 0.02),
        "w_ffn_in": nrm(ks[17], (L, D, 2 * FFN_HIDDEN), D ** -0.5),
        "w_ffn_out": nrm(ks[18], (L, FFN_HIDDEN, D), FFN_HIDDEN ** -0.5),
        "final_norm_g": 1.0 + nrm(ks[19], (D,), 0.02),
    }


def reference(x, c, w_ada, b_ada, norm1_g, w_in, b_gate, gmlp_ln_g, gmlp_ln_b, gmlp_ws, gmlp_bs,
              hg_lb, hg_norm_g, w_branch_gmlp, w_branch_hg, w_out, norm2_g, w_ffn_in, w_ffn_out,
              final_norm_g):
    lb_all = jnp.cumsum(jax.nn.softmax(hg_lb.astype(jnp.float32), axis=0), axis=0)
    splits = np.cumsum([GMLP_WIDTH, GMLP_WIDTH, HG_WIDTH, HG_WIDTH, HG_WIDTH, HG_WIDTH, D_MODEL]).tolist()
    c_act = jax.nn.silu(c)
    for l in range(DEPTH):
        mod = c_act @ w_ada[l] + b_ada[l]
        sh1, sc1, gt1, sh2, sc2, gt2 = jnp.split(mod, 6, axis=-1)
        h = modulate(rmsnorm(x, norm1_g[l]), sh1, sc1)
        z = h @ w_in[l]
        u, v, q, f_pre, iv, og, ga, gb = jnp.split(z, splits, axis=-1)
        ya = gmlp_branch(jax.nn.gelu(u), jax.nn.gelu(v), gmlp_ln_g[l], gmlp_ln_b[l], gmlp_ws[l], gmlp_bs[l])
        yb = hgrn2_branch(q, f_pre, iv, og, lb_all[l], hg_norm_g[l])
        ga = jax.nn.sigmoid(ga + b_gate[l, :D_MODEL])
        gb = jax.nn.sigmoid(gb + b_gate[l, D_MODEL:])
        y = ga * (ya @ w_branch_gmlp[l]) + gb * (yb @ w_branch_hg[l])
        x = x + gt1[:, None, :] * (y @ w_out[l])
        h = modulate(rmsnorm(x, norm2_g[l]), sh2, sc2)
        a, up = jnp.split(h @ w_ffn_in[l], 2, axis=-1)
        x = x + gt2[:, None, :] * ((jax.nn.silu(a) * up) @ w_ffn_out[l])
    return rmsnorm(x, final_norm_g)
```

```python
import functools

import numpy as np
import jax
import jax.numpy as jnp
from jax import lax
from jax.experimental import pallas as pl
from jax.experimental.pallas import tpu as pltpu

F32 = jnp.float32
BF16 = jnp.bfloat16
EPS = 1e-6

GROUP = 128
CHUNK = 128
LEVELS = (64, 32, 16, 8, 4, 2, 1)
VMEM_LIMIT = 56 * 1024 * 1024


def _dot(a, b):
    return jnp.dot(a, b, preferred_element_type=F32)


def _dot_nt(a, b):
    return lax.dot_general(a, b, (((1,), (1,)), ((), ())), preferred_element_type=F32)


def _dot_tn(a, b):
    return lax.dot_general(a, b, (((0,), (0,)), ((), ())), preferred_element_type=F32)


def _rms(x):
    return x * lax.rsqrt(jnp.mean(x * x, axis=-1, keepdims=True) + EPS)


def _split_bf16(x, parts):
    out = []
    for _ in range(parts - 1):
        p = x.astype(BF16)
        out.append(p)
        x = x - p.astype(F32)
    out.append(x.astype(BF16))
    return out


def _ada_kernel(c_ref, w_ref, b_ref, o_ref):
    ca_hi, ca_lo = _split_bf16(jax.nn.silu(c_ref[...]), 2)
    w_hi, w_lo = _split_bf16(w_ref[...], 2)
    o_ref[...] = _dot(ca_hi, w_hi) + _dot(ca_lo, w_hi) + _dot(ca_hi, w_lo) + b_ref[...]


def _ada(c_pad, w_ada, b_ada, tn=1024):
    _, d, n = w_ada.shape
    return pl.pallas_call(
        _ada_kernel,
        grid=(n // tn,),
        in_specs=[pl.BlockSpec((8, d), lambda j: (0, 0)),
                  pl.BlockSpec((None, d, tn), lambda j: (0, 0, j)),
                  pl.BlockSpec((1, tn), lambda j: (0, j))],
        out_specs=pl.BlockSpec((8, tn), lambda j: (0, j)),
        out_shape=jax.ShapeDtypeStruct((8, n), F32),
        compiler_params=pltpu.CompilerParams(dimension_semantics=("arbitrary",), vmem_limit_bytes=VMEM_LIMIT),
        name="ada",
    )(c_pad, w_ada, b_ada)


def _prenorm_kernel(x_ref, mod_ref, g_ref, h_ref):
    hn = _rms(x_ref[0]) * g_ref[...]
    h_ref[0] = (hn * (1.0 + mod_ref[0, 1]) + mod_ref[0, 0]).astype(BF16)


def _prenorm(x, mod6, g, tm=512):
    b, s, d = x.shape
    return pl.pallas_call(
        _prenorm_kernel,
        grid=(b, s // tm),
        in_specs=[pl.BlockSpec((1, tm, d), lambda i, r: (i, r, 0)),
                  pl.BlockSpec((1, 6, 1, d), lambda i, r: (i, 0, 0, 0)),
                  pl.BlockSpec((1, d), lambda i, r: (0, 0))],
        out_specs=pl.BlockSpec((1, tm, d), lambda i, r: (i, r, 0)),
        out_shape=jax.ShapeDtypeStruct((b, s, d), BF16),
        compiler_params=pltpu.CompilerParams(dimension_semantics=("arbitrary", "arbitrary"),
                                             vmem_limit_bytes=VMEM_LIMIT),
        name="prenorm",
    )(x, mod6, g)


def _inproj_kernel(h_ref, w_ref, *rest, act):
    o_ref = rest[-1]
    acc = _dot(h_ref[...], w_ref[...])
    if act == "gelu":
        o_ref[...] = jax.nn.gelu(acc).astype(o_ref.dtype)
    elif act == "silu":
        o_ref[...] = jax.nn.silu(acc).astype(o_ref.dtype)
    elif act == "none":
        o_ref[...] = acc.astype(o_ref.dtype)
    elif act == "gate":
        o_ref[...] = jax.nn.sigmoid(acc + rest[0][...]).astype(o_ref.dtype)
    elif act == "logf":
        p = jax.nn.softmax(rest[0][...], axis=0)
        lb = p[0:1, :]
        o_ref[...] = jnp.log(lb + (1.0 - lb) * jax.nn.sigmoid(acc))
    else:
        raise ValueError(act)


def _inproj(h, w, cols, act, extra=None, out_dtype=BF16, tm=1024, tn=1024):
    m, d = h.shape
    nj = len(cols)
    c0, step = cols[0], (cols[1] - cols[0] if nj > 1 else 0)
    assert all(cols[j] == c0 + j * step for j in range(nj))
    in_specs = [pl.BlockSpec((tm, d), lambda j, i: (i, 0)),
                pl.BlockSpec((d, tn), lambda j, i: (0, c0 + j * step))]
    args = [h, w]
    if act == "gate":
        in_specs.append(pl.BlockSpec((1, tn), lambda j, i: (0, j)))
        args.append(extra)
    elif act == "logf":
        in_specs.append(pl.BlockSpec(extra.shape, lambda j, i: (0, 0)))
        args.append(extra)
    return pl.pallas_call(
        functools.partial(_inproj_kernel, act=act),
        grid=(nj, m // tm),
        in_specs=in_specs,
        out_specs=pl.BlockSpec((tm, tn), lambda j, i: (i, j)),
        out_shape=jax.ShapeDtypeStruct((m, nj * tn), out_dtype),
        compiler_params=pltpu.CompilerParams(dimension_semantics=("arbitrary", "arbitrary"),
                                             vmem_limit_bytes=VMEM_LIMIT),
        name="inproj_" + act,
    )(*args)


def _level_masks():
    t = np.arange(CHUNK)[:, None]
    s = np.arange(CHUNK)[None, :]
    out = []
    for m in LEVELS:
        out.append((t // (2 * m) == s // (2 * m)) & ((t // m) % 2 == 1) & ((s // m) % 2 == 0))
    out.append(t == s)
    return np.stack(out).astype(np.float32)


def _cumsum_matrix():
    t = np.arange(CHUNK)
    tri = (t[:, None] >= t[None, :]).astype(np.float32)
    blocks = [tri]
    for m in (2, 1):
        bnd = (t // (2 * m)) * (2 * m) + m - 1
        blocks.append(tri - tri[bnd])
    w = np.concatenate(blocks, axis=0)
    return np.concatenate([w, w, w], axis=1)


def _mixer_kernel(u_ref, v_ref, q_ref, og_ref, lf_ref, iv_ref, lng_ref, lnb_ref, ws_ref, bst_ref, ng_ref,
                  cw_ref, msk_ref, ya_ref, yb_ref, st_ref):
    rows = u_ref.shape[0]
    nchunk = rows // CHUNK
    ngroup = u_ref.shape[1] // GROUP

    @pl.when(pl.program_id(1) == 0)
    def _():
        st_ref[...] = jnp.zeros_like(st_ref)

    v = v_ref[...].astype(F32)
    mu = jnp.mean(v, axis=-1, keepdims=True)
    vc = v - mu
    var = jnp.mean(vc * vc, axis=-1, keepdims=True)
    vn = (vc * lax.rsqrt(var + EPS) * lng_ref[...] + lnb_ref[...]).astype(BF16)
    tri = lax.broadcasted_iota(jnp.int32, (CHUNK, CHUNK), 0) >= lax.broadcasted_iota(jnp.int32, (CHUNK, CHUNK), 1)
    for g in range(ngroup):
        cols = slice(g * GROUP, (g + 1) * GROUP)
        w = jnp.where(tri, ws_ref[g], 0.0).astype(BF16)
        rhs = jnp.concatenate([vn[c * CHUNK:(c + 1) * CHUNK, cols] for c in range(nchunk)], axis=1)
        sg = _dot(w, rhs) + bst_ref[:, g:g + 1]
        for c in range(nchunk):
            r = slice(c * CHUNK, (c + 1) * CHUNK)
            ya_ref[r, cols] = (u_ref[r, cols].astype(F32) * sg[:, c * CHUNK:(c + 1) * CHUNK]).astype(BF16)

    for c in range(nchunk):
        r = slice(c * CHUNK, (c + 1) * CHUNK)
        lf_all = lf_ref[r, :]
        bd = _dot(cw_ref[...], jnp.concatenate(_split_bf16(lf_all, 3), axis=0))
        for h in range(ngroup):
            cols = slice(h * GROUP, (h + 1) * GROUP)
            b = bd[0:CHUNK, cols]
            small = {2: bd[CHUNK:2 * CHUNK, cols], 1: bd[2 * CHUNK:3 * CHUNK, cols]}
            qb = q_ref[r, cols]
            qf = qb.astype(F32)
            k = 1.0 - jnp.exp(lf_all[:, cols])
            iv = iv_ref[r, cols]
            b_last = b[CHUNK - 1:CHUNK, :]
            st = st_ref[h]
            o = _dot_nt((qf * jnp.exp(b)).astype(BF16), st.astype(BF16))
            a = msk_ref[len(LEVELS)] * _dot_nt(qb, k.astype(BF16))
            for li, m in enumerate(LEVELS):
                if m in small:
                    dl = small[m]
                else:
                    b3 = b.reshape(CHUNK // (2 * m), 2 * m, GROUP)
                    dl = (b3 - b3[:, m - 1:m, :]).reshape(CHUNK, GROUP)
                e = jnp.exp(-jnp.abs(dl))
                a = a + msk_ref[li] * _dot_nt((qf * e).astype(BF16), (k * e).astype(BF16))
            o = o + _dot(a.astype(BF16), iv)
            st_ref[h] = st * jnp.exp(b_last) + _dot_tn(iv, (k * jnp.exp(b_last - b)).astype(BF16))
            y = _rms(o) * ng_ref[:, cols] * og_ref[r, cols].astype(F32)
            yb_ref[r, cols] = y.astype(BF16)


def _mixer(zg, zs, lf, iv, ln_g, ln_b, ws, bs_t, norm_g, batch, rows=256):
    m, width = lf.shape
    nr = m // batch // rows
    ngroup = width // GROUP
    cw = jnp.asarray(_cumsum_matrix(), BF16)
    msk = jnp.asarray(_level_masks(), F32)
    row_blk = lambda col: pl.BlockSpec((rows, width), lambda b, r: (b * nr + r, col))
    full = lambda a: pl.BlockSpec(a.shape, lambda b, r: (0,) * a.ndim)
    return pl.pallas_call(
        _mixer_kernel,
        grid=(batch, nr),
        in_specs=[row_blk(0), row_blk(1), row_blk(0), row_blk(1), row_blk(0), row_blk(0),
                  full(ln_g), full(ln_b), full(ws), full(bs_t), full(norm_g), full(cw), full(msk)],
        out_specs=[row_blk(0), row_blk(0)],
        out_shape=[jax.ShapeDtypeStruct((m, width), BF16), jax.ShapeDtypeStruct((m, width), BF16)],
        scratch_shapes=[pltpu.VMEM((ngroup, GROUP, GROUP), F32)],
        compiler_params=pltpu.CompilerParams(dimension_semantics=("arbitrary", "arbitrary"),
                                             vmem_limit_bytes=VMEM_LIMIT),
        name="mixer",
    )(zg, zg, zs, zs, lf, iv, ln_g, ln_b, ws, bs_t, norm_g, cw, msk)


def _merge_kernel(ya_ref, yb_ref, ga_ref, gb_ref, x_ref, mod_ref, wa_ref, wb_ref, wo_ref, g_ref, x1_ref, h2_ref):
    pa = _dot(ya_ref[...], wa_ref[...])
    pb = _dot(yb_ref[...], wb_ref[...])
    y = (ga_ref[...].astype(F32) * pa + gb_ref[...].astype(F32) * pb).astype(BF16)
    x1 = x_ref[0] + mod_ref[0, 2] * _dot(y, wo_ref[...])
    x1_ref[0] = x1
    hn = _rms(x1) * g_ref[...]
    h2_ref[0] = (hn * (1.0 + mod_ref[0, 4]) + mod_ref[0, 3]).astype(BF16)


def _merge(ya, yb, gates, x, mod6, wa, wb, wo, g, tm=256):
    b, s, d = x.shape
    nr = s // tm
    width = ya.shape[1]
    resident = lambda a: pl.BlockSpec(a.shape, lambda i, r: (0,) * a.ndim, pipeline_mode=pl.Buffered(1))
    return pl.pallas_call(
        _merge_kernel,
        grid=(b, nr),
        in_specs=[pl.BlockSpec((tm, width), lambda i, r: (i * nr + r, 0)),
                  pl.BlockSpec((tm, width), lambda i, r: (i * nr + r, 0)),
                  pl.BlockSpec((tm, d), lambda i, r: (i * nr + r, 0)),
                  pl.BlockSpec((tm, d), lambda i, r: (i * nr + r, 1)),
                  pl.BlockSpec((1, tm, d), lambda i, r: (i, r, 0)),
                  pl.BlockSpec((1, 6, 1, d), lambda i, r: (i, 0, 0, 0)),
                  resident(wa), resident(wb), resident(wo),
                  pl.BlockSpec((1, d), lambda i, r: (0, 0))],
        out_specs=[pl.BlockSpec((1, tm, d), lambda i, r: (i, r, 0)),
                   pl.BlockSpec((1, tm, d), lambda i, r: (i, r, 0))],
        out_shape=[jax.ShapeDtypeStruct((b, s, d), F32), jax.ShapeDtypeStruct((b, s, d), BF16)],
        compiler_params=pltpu.CompilerParams(dimension_semantics=("arbitrary", "arbitrary"),
                                             vmem_limit_bytes=VMEM_LIMIT),
        name="merge",
    )(ya, yb, gates, gates, x, mod6, wa, wb, wo, g)


def _ffn_kernel(h_ref, wa_ref, wu_ref, wo_ref, x1_ref, mod_ref, g_ref, o_ref, acc_ref):
    t = pl.program_id(2)

    @pl.when(t == 0)
    def _():
        acc_ref[...] = jnp.zeros_like(acc_ref)

    h = h_ref[0]
    a = _dot(h, wa_ref[...])
    up = _dot(h, wu_ref[...])
    acc_ref[...] += _dot((jax.nn.silu(a) * up).astype(BF16), wo_ref[...])

    @pl.when(t == pl.num_programs(2) - 1)
    def _():
        x2 = x1_ref[0] + mod_ref[0, 5] * acc_ref[...]
        o_ref[0] = _rms(x2) * g_ref[...]


def _ffn(h2, w_in, w_out, x1, mod6, g, tm=512, th=512):
    b, s, d = x1.shape
    hidden = w_out.shape[0]
    nt = hidden // th
    return pl.pallas_call(
        _ffn_kernel,
        grid=(b, s // tm, nt),
        in_specs=[pl.BlockSpec((1, tm, d), lambda i, r, t: (i, r, 0)),
                  pl.BlockSpec((d, th), lambda i, r, t: (0, t)),
                  pl.BlockSpec((d, th), lambda i, r, t: (0, nt + t)),
                  pl.BlockSpec((th, d), lambda i, r, t: (t, 0)),
                  pl.BlockSpec((1, tm, d), lambda i, r, t: (i, r, 0)),
                  pl.BlockSpec((1, 6, 1, d), lambda i, r, t: (i, 0, 0, 0)),
                  pl.BlockSpec((1, d), lambda i, r, t: (0, 0))],
        out_specs=pl.BlockSpec((1, tm, d), lambda i, r, t: (i, r, 0)),
        out_shape=jax.ShapeDtypeStruct((b, s, d), F32),
        scratch_shapes=[pltpu.VMEM((tm, d), F32)],
        compiler_params=pltpu.CompilerParams(dimension_semantics=("arbitrary", "arbitrary", "arbitrary"),
                                             vmem_limit_bytes=VMEM_LIMIT),
        name="ffn",
    )(h2, w_in, w_in, w_out, x1, mod6, g)


def kernel(x, c, w_ada, b_ada, norm1_g, w_in, b_gate, gmlp_ln_g, gmlp_ln_b, gmlp_ws, gmlp_bs, hg_lb, hg_norm_g,
           w_branch_gmlp, w_branch_hg, w_out, norm2_g, w_ffn_in, w_ffn_out, final_norm_g):
    batch, seq, d = x.shape
    depth = w_ada.shape[0]
    width = w_branch_gmlp.shape[1]
    assert depth == 1 and width == 8 * GROUP and gmlp_ws.shape[2] == CHUNK and seq % 256 == 0
    assert w_in.shape[2] == 6 * width + 2 * d and hg_lb.shape[0] == depth + 1
    m = batch * seq
    tn = 1024
    assert width == tn and d == 2 * tn

    c_pad = jnp.zeros((8, d), F32).at[:batch].set(c)
    mod = _ada(c_pad, w_ada, b_ada)
    mod6 = mod[:batch].reshape(batch, 6, 1, d)

    h1 = _prenorm(x, mod6, norm1_g).reshape(m, d)
    w_in_b = w_in[0].astype(BF16)
    zg = _inproj(h1, w_in_b, (0, 1), "gelu")
    zs = _inproj(h1, w_in_b, (2, 5), "silu")
    lf = _inproj(h1, w_in_b, (3,), "logf", extra=hg_lb, out_dtype=F32)
    iv = _inproj(h1, w_in_b, (4,), "none")
    gates = _inproj(h1, w_in_b, (6, 7, 8, 9), "gate", extra=b_gate)

    ya, yb = _mixer(zg, zs, lf, iv, gmlp_ln_g, gmlp_ln_b, gmlp_ws[0], gmlp_bs[0].T, hg_norm_g, batch)

    x1, h2 = _merge(ya, yb, gates, x, mod6, w_branch_gmlp[0].astype(BF16), w_branch_hg[0].astype(BF16),
                    w_out[0].astype(BF16), norm2_g)
    return _ffn(h2, w_ffn_in[0].astype(BF16), w_ffn_out[0].astype(BF16), x1, mod6, final_norm_g.reshape(1, d))
```

```python
import functools

import numpy as np
import jax
import jax.numpy as jnp
from jax import lax
from jax.experimental import pallas as pl
from jax.experimental.pallas import tpu as pltpu

F32 = jnp.float32
BF16 = jnp.bfloat16
EPS = 1e-6

GROUP = 128
CHUNK = 128
LEVELS = (64, 32, 16, 8, 4, 2, 1)
VMEM_LIMIT = 56 * 1024 * 1024


def _dot(a, b):
    return jnp.dot(a, b, preferred_element_type=F32)


def _dot_nt(a, b):
    return lax.dot_general(a, b, (((1,), (1,)), ((), ())), preferred_element_type=F32)


def _dot_tn(a, b):
    return lax.dot_general(a, b, (((0,), (0,)), ((), ())), preferred_element_type=F32)


def _rms(x):
    return x * lax.rsqrt(jnp.mean(x * x, axis=-1, keepdims=True) + EPS)


def _split_bf16(x, parts):
    out = []
    for _ in range(parts - 1):
        p = x.astype(BF16)
        out.append(p)
        x = x - p.astype(F32)
    out.append(x.astype(BF16))
    return out


def _ada_kernel(c_ref, w_ref, b_ref, o_ref):
    ca_hi, ca_lo = _split_bf16(jax.nn.silu(c_ref[...]), 2)
    w_hi, w_lo = _split_bf16(w_ref[...], 2)
    o_ref[...] = _dot(ca_hi, w_hi) + _dot(ca_lo, w_hi) + _dot(ca_hi, w_lo) + b_ref[...]


def _ada(c_pad, w_ada, b_ada, tn=1024):
    _, d, n = w_ada.shape
    return pl.pallas_call(
        _ada_kernel,
        grid=(n // tn,),
        in_specs=[pl.BlockSpec((8, d), lambda j: (0, 0)),
                  pl.BlockSpec((None, d, tn), lambda j: (0, 0, j)),
                  pl.BlockSpec((1, tn), lambda j: (0, j))],
        out_specs=pl.BlockSpec((8, tn), lambda j: (0, j)),
        out_shape=jax.ShapeDtypeStruct((8, n), F32),
        compiler_params=pltpu.CompilerParams(dimension_semantics=("arbitrary",), vmem_limit_bytes=VMEM_LIMIT),
        name="ada",
    )(c_pad, w_ada, b_ada)


def _prenorm_kernel(x_ref, mod_ref, g_ref, h_ref):
    hn = _rms(x_ref[0]) * g_ref[...]
    h_ref[0] = (hn * (1.0 + mod_ref[0, 1]) + mod_ref[0, 0]).astype(BF16)


def _prenorm(x, mod6, g, tm=512):
    b, s, d = x.shape
    return pl.pallas_call(
        _prenorm_kernel,
        grid=(b, s // tm),
        in_specs=[pl.BlockSpec((1, tm, d), lambda i, r: (i, r, 0)),
                  pl.BlockSpec((1, 6, 1, d), lambda i, r: (i, 0, 0, 0)),
                  pl.BlockSpec((1, d), lambda i, r: (0, 0))],
        out_specs=pl.BlockSpec((1, tm, d), lambda i, r: (i, r, 0)),
        out_shape=jax.ShapeDtypeStruct((b, s, d), BF16),
        compiler_params=pltpu.CompilerParams(dimension_semantics=("arbitrary", "arbitrary"),
                                             vmem_limit_bytes=VMEM_LIMIT),
        name="prenorm",
    )(x, mod6, g)


def _inproj_kernel(h_ref, w_ref, *rest, act):
    o_ref, wb_ref = rest[-2], rest[-1]

    @pl.when(pl.program_id(1) == 0)
    def _():
        wb_ref[...] = w_ref[...].astype(BF16)

    acc = _dot(h_ref[...], wb_ref[...])
    if act == "gelu":
        o_ref[...] = jax.nn.gelu(acc).astype(o_ref.dtype)
    elif act == "silu":
        o_ref[...] = jax.nn.silu(acc).astype(o_ref.dtype)
    elif act == "none":
        o_ref[...] = acc.astype(o_ref.dtype)
    elif act == "gate":
        o_ref[...] = jax.nn.sigmoid(acc + rest[0][...]).astype(o_ref.dtype)
    elif act == "logf":
        p = jax.nn.softmax(rest[0][...], axis=0)
        lb = p[0:1, :]
        o_ref[...] = jnp.log(lb + (1.0 - lb) * jax.nn.sigmoid(acc))
    else:
        raise ValueError(act)


def _inproj(h, w, cols, act, extra=None, out_dtype=BF16, tm=1024, tn=1024):
    m, d = h.shape
    nj = len(cols)
    c0, step = cols[0], (cols[1] - cols[0] if nj > 1 else 0)
    assert all(cols[j] == c0 + j * step for j in range(nj))
    in_specs = [pl.BlockSpec((tm, d), lambda j, i: (i, 0)),
                pl.BlockSpec((None, d, tn), lambda j, i: (0, 0, c0 + j * step))]
    args = [h, w]
    if act == "gate":
        in_specs.append(pl.BlockSpec((1, tn), lambda j, i: (0, j)))
        args.append(extra)
    elif act == "logf":
        in_specs.append(pl.BlockSpec(extra.shape, lambda j, i: (0, 0)))
        args.append(extra)
    return pl.pallas_call(
        functools.partial(_inproj_kernel, act=act),
        grid=(nj, m // tm),
        in_specs=in_specs,
        out_specs=pl.BlockSpec((tm, tn), lambda j, i: (i, j)),
        out_shape=jax.ShapeDtypeStruct((m, nj * tn), out_dtype),
        scratch_shapes=[pltpu.VMEM((d, tn), BF16)],
        compiler_params=pltpu.CompilerParams(dimension_semantics=("arbitrary", "arbitrary"),
                                             vmem_limit_bytes=VMEM_LIMIT),
        name="inproj_" + act,
    )(*args)


def _level_masks():
    t = np.arange(CHUNK)[:, None]
    s = np.arange(CHUNK)[None, :]
    out = []
    for m in LEVELS:
        out.append((t // (2 * m) == s // (2 * m)) & ((t // m) % 2 == 1) & ((s // m) % 2 == 0))
    out.append(t == s)
    return np.stack(out).astype(np.float32)


def _cumsum_matrix():
    t = np.arange(CHUNK)
    tri = (t[:, None] >= t[None, :]).astype(np.float32)
    blocks = [tri]
    for m in (2, 1):
        bnd = (t // (2 * m)) * (2 * m) + m - 1
        blocks.append(tri - tri[bnd])
    w = np.concatenate(blocks, axis=0)
    return np.concatenate([w, w, w], axis=1)


def _mixer_kernel(u_ref, v_ref, q_ref, og_ref, lf_ref, iv_ref, lng_ref, lnb_ref, ws_ref, bst_ref, ng_ref,
                  cw_ref, msk_ref, ya_ref, yb_ref, st_ref):
    rows = u_ref.shape[0]
    nchunk = rows // CHUNK
    ngroup = u_ref.shape[1] // GROUP

    @pl.when(pl.program_id(1) == 0)
    def _():
        st_ref[...] = jnp.zeros_like(st_ref)

    v = v_ref[...].astype(F32)
    mu = jnp.mean(v, axis=-1, keepdims=True)
    vc = v - mu
    var = jnp.mean(vc * vc, axis=-1, keepdims=True)
    vn = (vc * lax.rsqrt(var + EPS) * lng_ref[...] + lnb_ref[...]).astype(BF16)
    tri = lax.broadcasted_iota(jnp.int32, (CHUNK, CHUNK), 0) >= lax.broadcasted_iota(jnp.int32, (CHUNK, CHUNK), 1)
    for g in range(ngroup):
        cols = slice(g * GROUP, (g + 1) * GROUP)
        w = jnp.where(tri, ws_ref[g], 0.0).astype(BF16)
        rhs = jnp.concatenate([vn[c * CHUNK:(c + 1) * CHUNK, cols] for c in range(nchunk)], axis=1)
        sg = _dot(w, rhs) + bst_ref[:, g:g + 1]
        for c in range(nchunk):
            r = slice(c * CHUNK, (c + 1) * CHUNK)
            ya_ref[r, cols] = (u_ref[r, cols].astype(F32) * sg[:, c * CHUNK:(c + 1) * CHUNK]).astype(BF16)

    for c in range(nchunk):
        r = slice(c * CHUNK, (c + 1) * CHUNK)
        lf_all = lf_ref[r, :]
        bd = _dot(cw_ref[...], jnp.concatenate(_split_bf16(lf_all, 3), axis=0))
        for h in range(ngroup):
            cols = slice(h * GROUP, (h + 1) * GROUP)
            b = bd[0:CHUNK, cols]
            small = {2: bd[CHUNK:2 * CHUNK, cols], 1: bd[2 * CHUNK:3 * CHUNK, cols]}
            qb = q_ref[r, cols]
            qf = qb.astype(F32)
            k = 1.0 - jnp.exp(lf_all[:, cols])
            iv = iv_ref[r, cols]
            b_last = b[CHUNK - 1:CHUNK, :]
            st = st_ref[h]
            o = _dot_nt((qf * jnp.exp(b)).astype(BF16), st.astype(BF16))
            a = msk_ref[len(LEVELS)] * _dot_nt(qb, k.astype(BF16))
            for li, m in enumerate(LEVELS):
                if m in small:
                    dl = small[m]
                else:
                    b3 = b.reshape(CHUNK // (2 * m), 2 * m, GROUP)
                    dl = (b3 - b3[:, m - 1:m, :]).reshape(CHUNK, GROUP)
                e = jnp.exp(-jnp.abs(dl))
                a = a + msk_ref[li] * _dot_nt((qf * e).astype(BF16), (k * e).astype(BF16))
            o = o + _dot(a.astype(BF16), iv)
            st_ref[h] = st * jnp.exp(b_last) + _dot_tn(iv, (k * jnp.exp(b_last - b)).astype(BF16))
            y = _rms(o) * ng_ref[:, cols] * og_ref[r, cols].astype(F32)
            yb_ref[r, cols] = y.astype(BF16)


def _mixer(zg, zs, lf, iv, ln_g, ln_b, ws, bs_t, norm_g, batch, rows=256):
    m, width = lf.shape
    nr = m // batch // rows
    ngroup = width // GROUP
    cw = jnp.asarray(_cumsum_matrix(), BF16)
    msk = jnp.asarray(_level_masks(), F32)
    row_blk = lambda col: pl.BlockSpec((rows, width), lambda b, r: (b * nr + r, col))
    full = lambda a: pl.BlockSpec(a.shape, lambda b, r: (0,) * a.ndim)
    return pl.pallas_call(
        _mixer_kernel,
        grid=(batch, nr),
        in_specs=[row_blk(0), row_blk(1), row_blk(0), row_blk(1), row_blk(0), row_blk(0),
                  full(ln_g), full(ln_b), full(ws), full(bs_t), full(norm_g), full(cw), full(msk)],
        out_specs=[row_blk(0), row_blk(0)],
        out_shape=[jax.ShapeDtypeStruct((m, width), BF16), jax.ShapeDtypeStruct((m, width), BF16)],
        scratch_shapes=[pltpu.VMEM((ngroup, GROUP, GROUP), F32)],
        compiler_params=pltpu.CompilerParams(dimension_semantics=("arbitrary", "arbitrary"),
                                             vmem_limit_bytes=VMEM_LIMIT),
        name="mixer",
    )(zg, zg, zs, zs, lf, iv, ln_g, ln_b, ws, bs_t, norm_g, cw, msk)


def _merge_kernel(ya_ref, yb_ref, ga_ref, gb_ref, x_ref, mod_ref, wa_ref, wb_ref, wo_ref, g_ref, x1_ref, h2_ref):
    pa = _dot(ya_ref[...], wa_ref[...])
    pb = _dot(yb_ref[...], wb_ref[...])
    y = (ga_ref[...].astype(F32) * pa + gb_ref[...].astype(F32) * pb).astype(BF16)
    x1 = x_ref[0] + mod_ref[0, 2] * _dot(y, wo_ref[...])
    x1_ref[0] = x1
    hn = _rms(x1) * g_ref[...]
    h2_ref[0] = (hn * (1.0 + mod_ref[0, 4]) + mod_ref[0, 3]).astype(BF16)


def _merge(ya, yb, gates, x, mod6, wa, wb, wo, g, tm=256):
    b, s, d = x.shape
    nr = s // tm
    width = ya.shape[1]
    resident = lambda a: pl.BlockSpec(a.shape, lambda i, r: (0,) * a.ndim, pipeline_mode=pl.Buffered(1))
    return pl.pallas_call(
        _merge_kernel,
        grid=(b, nr),
        in_specs=[pl.BlockSpec((tm, width), lambda i, r: (i * nr + r, 0)),
                  pl.BlockSpec((tm, width), lambda i, r: (i * nr + r, 0)),
                  pl.BlockSpec((tm, d), lambda i, r: (i * nr + r, 0)),
                  pl.BlockSpec((tm, d), lambda i, r: (i * nr + r, 1)),
                  pl.BlockSpec((1, tm, d), lambda i, r: (i, r, 0)),
                  pl.BlockSpec((1, 6, 1, d), lambda i, r: (i, 0, 0, 0)),
                  resident(wa), resident(wb), resident(wo),
                  pl.BlockSpec((1, d), lambda i, r: (0, 0))],
        out_specs=[pl.BlockSpec((1, tm, d), lambda i, r: (i, r, 0)),
                   pl.BlockSpec((1, tm, d), lambda i, r: (i, r, 0))],
        out_shape=[jax.ShapeDtypeStruct((b, s, d), F32), jax.ShapeDtypeStruct((b, s, d), BF16)],
        compiler_params=pltpu.CompilerParams(dimension_semantics=("arbitrary", "arbitrary"),
                                             vmem_limit_bytes=VMEM_LIMIT),
        name="merge",
    )(ya, yb, gates, gates, x, mod6, wa, wb, wo, g)


def _ffn_kernel(h_ref, wa_ref, wu_ref, wo_ref, x1_ref, mod_ref, g_ref, o_ref, acc_ref, *, nt):
    t = pl.program_id(2)

    @pl.when(t == 0)
    def _():
        acc_ref[...] = jnp.zeros_like(acc_ref)

    @pl.when(t < nt)
    def _():
        h = h_ref[0]
        a = _dot(h, wa_ref[...].astype(BF16))
        up = _dot(h, wu_ref[...].astype(BF16))
        acc_ref[...] += _dot((jax.nn.silu(a) * up).astype(BF16), wo_ref[...].astype(BF16))

    @pl.when(t >= nt)
    def _():
        half = o_ref.shape[1]
        rows = pl.ds(pl.multiple_of((t - nt) * half, half), half)
        x2 = x1_ref[0] + mod_ref[0, 5] * acc_ref[rows, :]
        o_ref[0] = _rms(x2) * g_ref[...]


def _ffn(h2, w_in, w_out, x1, mod6, g, tm=1024, th=256):
    b, s, d = x1.shape
    hidden = w_out.shape[1]
    nt = hidden // th
    half = tm // 2
    wt = lambda t: jnp.minimum(t, nt - 1)
    fin = lambda r, t: 2 * r + jnp.clip(t - nt, 0, 1)
    return pl.pallas_call(
        functools.partial(_ffn_kernel, nt=nt),
        grid=(b, s // tm, nt + 2),
        in_specs=[pl.BlockSpec((1, tm, d), lambda i, r, t: (i, r, 0)),
                  pl.BlockSpec((None, d, th), lambda i, r, t: (0, 0, wt(t))),
                  pl.BlockSpec((None, d, th), lambda i, r, t: (0, 0, nt + wt(t))),
                  pl.BlockSpec((None, th, d), lambda i, r, t: (0, wt(t), 0)),
                  pl.BlockSpec((1, half, d), lambda i, r, t: (i, fin(r, t), 0)),
                  pl.BlockSpec((1, 6, 1, d), lambda i, r, t: (i, 0, 0, 0)),
                  pl.BlockSpec((1, d), lambda i, r, t: (0, 0))],
        out_specs=pl.BlockSpec((1, half, d), lambda i, r, t: (i, fin(r, t), 0)),
        out_shape=jax.ShapeDtypeStruct((b, s, d), F32),
        scratch_shapes=[pltpu.VMEM((tm, d), F32)],
        compiler_params=pltpu.CompilerParams(dimension_semantics=("arbitrary", "arbitrary", "arbitrary"),
                                             vmem_limit_bytes=VMEM_LIMIT),
        name="ffn",
    )(h2, w_in, w_in, w_out, x1, mod6, g)


def kernel(x, c, w_ada, b_ada, norm1_g, w_in, b_gate, gmlp_ln_g, gmlp_ln_b, gmlp_ws, gmlp_bs, hg_lb, hg_norm_g,
           w_branch_gmlp, w_branch_hg, w_out, norm2_g, w_ffn_in, w_ffn_out, final_norm_g):
    batch, seq, d = x.shape
    depth = w_ada.shape[0]
    width = w_branch_gmlp.shape[1]
    assert depth == 1 and width == 8 * GROUP and gmlp_ws.shape[2] == CHUNK and seq % 256 == 0
    assert w_in.shape[2] == 6 * width + 2 * d and hg_lb.shape[0] == depth + 1
    m = batch * seq
    tn = 1024
    assert width == tn and d == 2 * tn

    c_pad = jnp.zeros((8, d), F32).at[:batch].set(c)
    mod = _ada(c_pad, w_ada, b_ada)
    mod6 = mod[:batch].reshape(batch, 6, 1, d)

    h1 = _prenorm(x, mod6, norm1_g).reshape(m, d)
    zg = _inproj(h1, w_in, (0, 1), "gelu")
    zs = _inproj(h1, w_in, (2, 5), "silu")
    lf = _inproj(h1, w_in, (3,), "logf", extra=hg_lb, out_dtype=F32)
    iv = _inproj(h1, w_in, (4,), "none")
    gates = _inproj(h1, w_in, (6, 7, 8, 9), "gate", extra=b_gate)

    ya, yb = _mixer(zg, zs, lf, iv, gmlp_ln_g, gmlp_ln_b, gmlp_ws[0], gmlp_bs[0].T, hg_norm_g, batch)

    x1, h2 = _merge(ya, yb, gates, x, mod6, w_branch_gmlp[0].astype(BF16), w_branch_hg[0].astype(BF16),
                    w_out[0].astype(BF16), norm2_g)
    return _ffn(h2, w_ffn_in, w_ffn_out, x1, mod6, final_norm_g.reshape(1, d))
```

```python
import functools

import numpy as np
import jax
import jax.numpy as jnp
from jax import lax
from jax.experimental import pallas as pl
from jax.experimental.pallas import tpu as pltpu

F32 = jnp.float32
BF16 = jnp.bfloat16
EPS = 1e-6

GROUP = 128
CHUNK = 128
LEVELS = (64, 32, 16, 8, 4, 2, 1)
VMEM_LIMIT = 56 * 1024 * 1024


def _dot(a, b):
    return jnp.dot(a, b, preferred_element_type=F32)


def _dot_nt(a, b):
    return lax.dot_general(a, b, (((1,), (1,)), ((), ())), preferred_element_type=F32)


def _dot_tn(a, b):
    return lax.dot_general(a, b, (((0,), (0,)), ((), ())), preferred_element_type=F32)


def _rms(x):
    return x * lax.rsqrt(jnp.mean(x * x, axis=-1, keepdims=True) + EPS)


def _split_bf16(x, parts):
    out = []
    for _ in range(parts - 1):
        p = x.astype(BF16)
        out.append(p)
        x = x - p.astype(F32)
    out.append(x.astype(BF16))
    return out


def _ada_kernel(c_ref, w_ref, b_ref, o_ref):
    ca_hi, ca_lo = _split_bf16(jax.nn.silu(c_ref[...]), 2)
    w_hi, w_lo = _split_bf16(w_ref[...], 2)
    o_ref[...] = _dot(ca_hi, w_hi) + _dot(ca_lo, w_hi) + _dot(ca_hi, w_lo) + b_ref[...]


def _ada(c_pad, w_ada, b_ada, tn=1024):
    _, d, n = w_ada.shape
    return pl.pallas_call(
        _ada_kernel,
        grid=(n // tn,),
        in_specs=[pl.BlockSpec((8, d), lambda j: (0, 0)),
                  pl.BlockSpec((None, d, tn), lambda j: (0, 0, j)),
                  pl.BlockSpec((1, tn), lambda j: (0, j))],
        out_specs=pl.BlockSpec((8, tn), lambda j: (0, j)),
        out_shape=jax.ShapeDtypeStruct((8, n), F32),
        compiler_params=pltpu.CompilerParams(dimension_semantics=("arbitrary",), vmem_limit_bytes=VMEM_LIMIT),
        name="ada",
    )(c_pad, w_ada, b_ada)


def _prenorm_kernel(x_ref, mod_ref, g_ref, h_ref):
    hn = _rms(x_ref[0]) * g_ref[...]
    h_ref[0] = (hn * (1.0 + mod_ref[0, 1]) + mod_ref[0, 0]).astype(BF16)


def _prenorm(x, mod6, g, tm=512):
    b, s, d = x.shape
    return pl.pallas_call(
        _prenorm_kernel,
        grid=(b, s // tm),
        in_specs=[pl.BlockSpec((1, tm, d), lambda i, r: (i, r, 0)),
                  pl.BlockSpec((1, 6, 1, d), lambda i, r: (i, 0, 0, 0)),
                  pl.BlockSpec((1, d), lambda i, r: (0, 0))],
        out_specs=pl.BlockSpec((1, tm, d), lambda i, r: (i, r, 0)),
        out_shape=jax.ShapeDtypeStruct((b, s, d), BF16),
        compiler_params=pltpu.CompilerParams(dimension_semantics=("arbitrary", "arbitrary"),
                                             vmem_limit_bytes=VMEM_LIMIT),
        name="prenorm",
    )(x, mod6, g)


def _inproj_kernel(*refs, act, n_extra, n_cast):
    h_ref, w_ref = refs[0], refs[1]
    extra = refs[2:2 + n_extra]
    src = refs[2 + n_extra:2 + n_extra + n_cast]
    o_ref = refs[2 + n_extra + n_cast]
    dst = refs[3 + n_extra + n_cast:3 + n_extra + 2 * n_cast]
    wb_ref = refs[-1]

    @pl.when(pl.program_id(1) == 0)
    def _():
        wb_ref[...] = w_ref[...].astype(BF16)

    for s_ref, d_ref in zip(src, dst):
        d_ref[...] = s_ref[...].astype(BF16)

    acc = _dot(h_ref[...], wb_ref[...])
    if act == "gelu":
        o_ref[...] = jax.nn.gelu(acc).astype(o_ref.dtype)
    elif act == "silu":
        o_ref[...] = jax.nn.silu(acc).astype(o_ref.dtype)
    elif act == "none":
        o_ref[...] = acc.astype(o_ref.dtype)
    elif act == "gate":
        o_ref[...] = jax.nn.sigmoid(acc + extra[0][...]).astype(o_ref.dtype)
    elif act == "logf":
        p = jax.nn.softmax(extra[0][...], axis=0)
        lb = p[0:1, :]
        o_ref[...] = jnp.log(lb + (1.0 - lb) * jax.nn.sigmoid(acc))
    else:
        raise ValueError(act)


def _inproj(h, w, cols, act, extra=None, out_dtype=BF16, casts=(), tm=1024, tn=1024):
    m, d = h.shape
    nj, ni = len(cols), m // tm
    c0, step = cols[0], (cols[1] - cols[0] if nj > 1 else 0)
    assert all(cols[j] == c0 + j * step for j in range(nj))
    in_specs = [pl.BlockSpec((tm, d), lambda j, i: (i, 0)),
                pl.BlockSpec((None, d, tn), lambda j, i: (0, 0, c0 + j * step))]
    args = [h, w]
    if act == "gate":
        in_specs.append(pl.BlockSpec((1, tn), lambda j, i: (0, j)))
        args.append(extra)
    elif act == "logf":
        in_specs.append(pl.BlockSpec(extra.shape, lambda j, i: (0, 0)))
        args.append(extra)
    n_extra = len(args) - 2
    out_specs = [pl.BlockSpec((tm, tn), lambda j, i: (i, j))]
    out_shape = [jax.ShapeDtypeStruct((m, nj * tn), out_dtype)]
    for a in casts:
        _, rows, width = a.shape
        slab = rows // (nj * ni)
        assert slab * nj * ni == rows and slab % 16 == 0
        in_specs.append(pl.BlockSpec((None, slab, width), lambda j, i: (0, j * ni + i, 0)))
        out_specs.append(pl.BlockSpec((slab, width), lambda j, i: (j * ni + i, 0)))
        out_shape.append(jax.ShapeDtypeStruct((rows, width), BF16))
        args.append(a)
    return pl.pallas_call(
        functools.partial(_inproj_kernel, act=act, n_extra=n_extra, n_cast=len(casts)),
        grid=(nj, ni),
        in_specs=in_specs,
        out_specs=out_specs,
        out_shape=out_shape,
        scratch_shapes=[pltpu.VMEM((d, tn), BF16)],
        compiler_params=pltpu.CompilerParams(dimension_semantics=("arbitrary", "arbitrary"),
                                             vmem_limit_bytes=VMEM_LIMIT),
        name="inproj_" + act,
    )(*args)


def _level_masks():
    t = np.arange(CHUNK)[:, None]
    s = np.arange(CHUNK)[None, :]
    out = []
    for m in LEVELS:
        out.append((t // (2 * m) == s // (2 * m)) & ((t // m) % 2 == 1) & ((s // m) % 2 == 0))
    out.append(t == s)
    return np.stack(out).astype(np.float32)


def _cumsum_matrix():
    t = np.arange(CHUNK)
    tri = (t[:, None] >= t[None, :]).astype(np.float32)
    blocks = [tri]
    for m in (2, 1):
        bnd = (t // (2 * m)) * (2 * m) + m - 1
        blocks.append(tri - tri[bnd])
    w = np.concatenate(blocks, axis=0)
    return np.concatenate([w, w, w], axis=1)


def _mixer_kernel(u_ref, v_ref, q_ref, og_ref, lf_ref, iv_ref, lng_ref, lnb_ref, ws_ref, bst_ref, ng_ref,
                  cw_ref, msk_ref, ya_ref, yb_ref, st_ref):
    rows = u_ref.shape[0]
    nchunk = rows // CHUNK
    ngroup = u_ref.shape[1] // GROUP

    @pl.when(pl.program_id(1) == 0)
    def _():
        st_ref[...] = jnp.zeros_like(st_ref)

    v = v_ref[...].astype(F32)
    mu = jnp.mean(v, axis=-1, keepdims=True)
    vc = v - mu
    var = jnp.mean(vc * vc, axis=-1, keepdims=True)
    vn = (vc * lax.rsqrt(var + EPS) * lng_ref[...] + lnb_ref[...]).astype(BF16)
    tri = lax.broadcasted_iota(jnp.int32, (CHUNK, CHUNK), 0) >= lax.broadcasted_iota(jnp.int32, (CHUNK, CHUNK), 1)
    for g in range(ngroup):
        cols = slice(g * GROUP, (g + 1) * GROUP)
        w = jnp.where(tri, ws_ref[g], 0.0).astype(BF16)
        rhs = jnp.concatenate([vn[c * CHUNK:(c + 1) * CHUNK, cols] for c in range(nchunk)], axis=1)
        sg = _dot(w, rhs) + bst_ref[:, g:g + 1]
        for c in range(nchunk):
            r = slice(c * CHUNK, (c + 1) * CHUNK)
            ya_ref[r, cols] = (u_ref[r, cols].astype(F32) * sg[:, c * CHUNK:(c + 1) * CHUNK]).astype(BF16)

    for c in range(nchunk):
        r = slice(c * CHUNK, (c + 1) * CHUNK)
        lf_all = lf_ref[r, :]
        bd = _dot(cw_ref[...], jnp.concatenate(_split_bf16(lf_all, 3), axis=0))
        for h in range(ngroup):
            cols = slice(h * GROUP, (h + 1) * GROUP)
            b = bd[0:CHUNK, cols]
            small = {2: bd[CHUNK:2 * CHUNK, cols], 1: bd[2 * CHUNK:3 * CHUNK, cols]}
            qb = q_ref[r, cols]
            qf = qb.astype(F32)
            k = 1.0 - jnp.exp(lf_all[:, cols])
            iv = iv_ref[r, cols]
            b_last = b[CHUNK - 1:CHUNK, :]
            st = st_ref[h]
            o = _dot_nt((qf * jnp.exp(b)).astype(BF16), st.astype(BF16))
            a = msk_ref[len(LEVELS)] * _dot_nt(qb, k.astype(BF16))
            for li, m in enumerate(LEVELS):
                if m in small:
                    dl = small[m]
                else:
                    b3 = b.reshape(CHUNK // (2 * m), 2 * m, GROUP)
                    dl = (b3 - b3[:, m - 1:m, :]).reshape(CHUNK, GROUP)
                e = jnp.exp(-jnp.abs(dl))
                a = a + msk_ref[li] * _dot_nt((qf * e).astype(BF16), (k * e).astype(BF16))
            o = o + _dot(a.astype(BF16), iv)
            st_ref[h] = st * jnp.exp(b_last) + _dot_tn(iv, (k * jnp.exp(b_last - b)).astype(BF16))
            y = _rms(o) * ng_ref[:, cols] * og_ref[r, cols].astype(F32)
            yb_ref[r, cols] = y.astype(BF16)


def _mixer(zg, zs, lf, iv, ln_g, ln_b, ws, bs_t, norm_g, batch, rows=256):
    m, width = lf.shape
    nr = m // batch // rows
    ngroup = width // GROUP
    cw = jnp.asarray(_cumsum_matrix(), BF16)
    msk = jnp.asarray(_level_masks(), F32)
    row_blk = lambda col: pl.BlockSpec((rows, width), lambda b, r: (b * nr + r, col))
    full = lambda a: pl.BlockSpec(a.shape, lambda b, r: (0,) * a.ndim)
    return pl.pallas_call(
        _mixer_kernel,
        grid=(batch, nr),
        in_specs=[row_blk(0), row_blk(1), row_blk(0), row_blk(1), row_blk(0), row_blk(0),
                  full(ln_g), full(ln_b), full(ws), full(bs_t), full(norm_g), full(cw), full(msk)],
        out_specs=[row_blk(0), row_blk(0)],
        out_shape=[jax.ShapeDtypeStruct((m, width), BF16), jax.ShapeDtypeStruct((m, width), BF16)],
        scratch_shapes=[pltpu.VMEM((ngroup, GROUP, GROUP), F32)],
        compiler_params=pltpu.CompilerParams(dimension_semantics=("arbitrary", "arbitrary"),
                                             vmem_limit_bytes=VMEM_LIMIT),
        name="mixer",
    )(zg, zg, zs, zs, lf, iv, ln_g, ln_b, ws, bs_t, norm_g, cw, msk)


def _merge_kernel(ya_ref, yb_ref, ga_ref, gb_ref, x_ref, mod_ref, wa_ref, wb_ref, wo_ref, g_ref, x1_ref, h2_ref):
    pa = _dot(ya_ref[...], wa_ref[...])
    pb = _dot(yb_ref[...], wb_ref[...])
    y = (ga_ref[...].astype(F32) * pa + gb_ref[...].astype(F32) * pb).astype(BF16)
    x1 = x_ref[0] + mod_ref[0, 2] * _dot(y, wo_ref[...])
    x1_ref[0] = x1
    hn = _rms(x1) * g_ref[...]
    h2_ref[0] = (hn * (1.0 + mod_ref[0, 4]) + mod_ref[0, 3]).astype(BF16)


def _merge(ya, yb, gates, x, mod6, wa, wb, wo, g, tm=256):
    b, s, d = x.shape
    nr = s // tm
    width = ya.shape[1]
    resident = lambda a: pl.BlockSpec(a.shape, lambda i, r: (0,) * a.ndim, pipeline_mode=pl.Buffered(1))
    return pl.pallas_call(
        _merge_kernel,
        grid=(b, nr),
        in_specs=[pl.BlockSpec((tm, width), lambda i, r: (i * nr + r, 0)),
                  pl.BlockSpec((tm, width), lambda i, r: (i * nr + r, 0)),
                  pl.BlockSpec((tm, d), lambda i, r: (i * nr + r, 0)),
                  pl.BlockSpec((tm, d), lambda i, r: (i * nr + r, 1)),
                  pl.BlockSpec((1, tm, d), lambda i, r: (i, r, 0)),
                  pl.BlockSpec((1, 6, 1, d), lambda i, r: (i, 0, 0, 0)),
                  resident(wa), resident(wb), resident(wo),
                  pl.BlockSpec((1, d), lambda i, r: (0, 0))],
        out_specs=[pl.BlockSpec((1, tm, d), lambda i, r: (i, r, 0)),
                   pl.BlockSpec((1, tm, d), lambda i, r: (i, r, 0))],
        out_shape=[jax.ShapeDtypeStruct((b, s, d), F32), jax.ShapeDtypeStruct((b, s, d), BF16)],
        compiler_params=pltpu.CompilerParams(dimension_semantics=("arbitrary", "arbitrary"),
                                             vmem_limit_bytes=VMEM_LIMIT),
        name="merge",
    )(ya, yb, gates, gates, x, mod6, wa, wb, wo, g)


def _ffn_kernel(h_ref, wa_ref, wu_ref, wo_ref, x1_ref, mod_ref, g_ref, o_ref, acc_ref, *, nt):
    t = pl.program_id(2)

    @pl.when(t == 0)
    def _():
        acc_ref[...] = jnp.zeros_like(acc_ref)

    @pl.when(t < nt)
    def _():
        h = h_ref[0]
        a = _dot(h, wa_ref[...])
        up = _dot(h, wu_ref[...])
        acc_ref[...] += _dot((jax.nn.silu(a) * up).astype(BF16), wo_ref[...])

    @pl.when(t >= nt)
    def _():
        half = o_ref.shape[1]
        rows = pl.ds(pl.multiple_of((t - nt) * half, half), half)
        x2 = x1_ref[0] + mod_ref[0, 5] * acc_ref[rows, :]
        o_ref[0] = _rms(x2) * g_ref[...]


def _ffn(h2, w_in, w_out, x1, mod6, g, tm=1024, th=512):
    b, s, d = x1.shape
    hidden = w_out.shape[0]
    nt = hidden // th
    half = tm // 2
    wt = lambda t: jnp.minimum(t, nt - 1)
    fin = lambda r, t: 2 * r + jnp.clip(t - nt, 0, 1)
    return pl.pallas_call(
        functools.partial(_ffn_kernel, nt=nt),
        grid=(b, s // tm, nt + 2),
        in_specs=[pl.BlockSpec((1, tm, d), lambda i, r, t: (i, r, 0)),
                  pl.BlockSpec((d, th), lambda i, r, t: (0, wt(t))),
                  pl.BlockSpec((d, th), lambda i, r, t: (0, nt + wt(t))),
                  pl.BlockSpec((th, d), lambda i, r, t: (wt(t), 0)),
                  pl.BlockSpec((1, half, d), lambda i, r, t: (i, fin(r, t), 0)),
                  pl.BlockSpec((1, 6, 1, d), lambda i, r, t: (i, 0, 0, 0)),
                  pl.BlockSpec((1, d), lambda i, r, t: (0, 0))],
        out_specs=pl.BlockSpec((1, half, d), lambda i, r, t: (i, fin(r, t), 0)),
        out_shape=jax.ShapeDtypeStruct((b, s, d), F32),
        scratch_shapes=[pltpu.VMEM((tm, d), F32)],
        compiler_params=pltpu.CompilerParams(dimension_semantics=("arbitrary", "arbitrary", "arbitrary"),
                                             vmem_limit_bytes=VMEM_LIMIT),
        name="ffn",
    )(h2, w_in, w_in, w_out, x1, mod6, g)


def kernel(x, c, w_ada, b_ada, norm1_g, w_in, b_gate, gmlp_ln_g, gmlp_ln_b, gmlp_ws, gmlp_bs, hg_lb, hg_norm_g,
           w_branch_gmlp, w_branch_hg, w_out, norm2_g, w_ffn_in, w_ffn_out, final_norm_g):
    batch, seq, d = x.shape
    depth = w_ada.shape[0]
    width = w_branch_gmlp.shape[1]
    assert depth == 1 and width == 8 * GROUP and gmlp_ws.shape[2] == CHUNK and seq % 256 == 0
    assert w_in.shape[2] == 6 * width + 2 * d and hg_lb.shape[0] == depth + 1
    m = batch * seq
    tn = 1024
    assert width == tn and d == 2 * tn

    c_pad = jnp.zeros((8, d), F32).at[:batch].set(c)
    mod = _ada(c_pad, w_ada, b_ada)
    mod6 = mod[:batch].reshape(batch, 6, 1, d)

    h1 = _prenorm(x, mod6, norm1_g).reshape(m, d)
    zg, w_fo = _inproj(h1, w_in, (0, 1), "gelu", casts=(w_ffn_out,))
    zs, w_a, w_b, w_o = _inproj(h1, w_in, (2, 5), "silu",
                                casts=(w_branch_gmlp, w_branch_hg, w_out))
    (lf,) = _inproj(h1, w_in, (3,), "logf", extra=hg_lb, out_dtype=F32)
    (iv,) = _inproj(h1, w_in, (4,), "none")
    gates, w_fi = _inproj(h1, w_in, (6, 7, 8, 9), "gate", extra=b_gate, casts=(w_ffn_in,))

    ya, yb = _mixer(zg, zs, lf, iv, gmlp_ln_g, gmlp_ln_b, gmlp_ws[0], gmlp_bs[0].T, hg_norm_g, batch)

    x1, h2 = _merge(ya, yb, gates, x, mod6, w_a, w_b, w_o, norm2_g)
    return _ffn(h2, w_fi, w_fo, x1, mod6, final_norm_g.reshape(1, d))
```

```python
import functools

import numpy as np
import jax
import jax.numpy as jnp
from jax import lax
from jax.experimental import pallas as pl
from jax.experimental.pallas import tpu as pltpu

F32 = jnp.float32
BF16 = jnp.bfloat16
EPS = 1e-6

GROUP = 128
CHUNK = 128
LEVELS = (64, 32, 16, 8, 4, 2, 1)
VMEM_LIMIT = 56 * 1024 * 1024


def _dot(a, b):
    return jnp.dot(a, b, preferred_element_type=F32)


def _dot_nt(a, b):
    return lax.dot_general(a, b, (((1,), (1,)), ((), ())), preferred_element_type=F32)


def _dot_tn(a, b):
    return lax.dot_general(a, b, (((0,), (0,)), ((), ())), preferred_element_type=F32)


def _rms(x):
    return x * lax.rsqrt(jnp.mean(x * x, axis=-1, keepdims=True) + EPS)


def _split_bf16(x, parts):
    out = []
    for _ in range(parts - 1):
        p = x.astype(BF16)
        out.append(p)
        x = x - p.astype(F32)
    out.append(x.astype(BF16))
    return out


def _ada_kernel(c_ref, w_ref, b_ref, o_ref):
    ca_hi, ca_lo = _split_bf16(jax.nn.silu(c_ref[...]), 2)
    w_hi, w_lo = _split_bf16(w_ref[...], 2)
    o_ref[...] = _dot(ca_hi, w_hi) + _dot(ca_lo, w_hi) + _dot(ca_hi, w_lo) + b_ref[...]


def _ada(c_pad, w_ada, b_ada, tn=1024):
    _, d, n = w_ada.shape
    return pl.pallas_call(
        _ada_kernel,
        grid=(n // tn,),
        in_specs=[pl.BlockSpec((8, d), lambda j: (0, 0)),
                  pl.BlockSpec((None, d, tn), lambda j: (0, 0, j)),
                  pl.BlockSpec((1, tn), lambda j: (0, j))],
        out_specs=pl.BlockSpec((8, tn), lambda j: (0, j)),
        out_shape=jax.ShapeDtypeStruct((8, n), F32),
        compiler_params=pltpu.CompilerParams(dimension_semantics=("arbitrary",), vmem_limit_bytes=VMEM_LIMIT),
        name="ada",
    )(c_pad, w_ada, b_ada)


def _prenorm_kernel(x_ref, mod_ref, g_ref, h_ref):
    hn = _rms(x_ref[0]) * g_ref[...]
    h_ref[0] = (hn * (1.0 + mod_ref[0, 1]) + mod_ref[0, 0]).astype(BF16)


def _prenorm(x, mod6, g, tm=512):
    b, s, d = x.shape
    return pl.pallas_call(
        _prenorm_kernel,
        grid=(b, s // tm),
        in_specs=[pl.BlockSpec((1, tm, d), lambda i, r: (i, r, 0)),
                  pl.BlockSpec((1, 6, 1, d), lambda i, r: (i, 0, 0, 0)),
                  pl.BlockSpec((1, d), lambda i, r: (0, 0))],
        out_specs=pl.BlockSpec((1, tm, d), lambda i, r: (i, r, 0)),
        out_shape=jax.ShapeDtypeStruct((b, s, d), BF16),
        compiler_params=pltpu.CompilerParams(dimension_semantics=("arbitrary", "arbitrary"),
                                             vmem_limit_bytes=VMEM_LIMIT),
        name="prenorm",
    )(x, mod6, g)


def _inproj_kernel(h_ref, w_ref, *rest, act):
    extra, o_ref, wb_ref = rest[:-2], rest[-2], rest[-1]

    @pl.when(pl.program_id(1) == 0)
    def _():
        wb_ref[...] = w_ref[...].astype(BF16)

    acc = _dot(h_ref[...], wb_ref[...])
    if act == "gelu":
        o_ref[...] = jax.nn.gelu(acc).astype(o_ref.dtype)
    elif act == "silu":
        o_ref[...] = jax.nn.silu(acc).astype(o_ref.dtype)
    elif act == "none":
        o_ref[...] = acc.astype(o_ref.dtype)
    elif act == "gate":
        o_ref[...] = jax.nn.sigmoid(acc + extra[0][...]).astype(o_ref.dtype)
    elif act == "logf":
        p = jax.nn.softmax(extra[0][...], axis=0)
        lb = p[0:1, :]
        o_ref[...] = jnp.log(lb + (1.0 - lb) * jax.nn.sigmoid(acc))
    else:
        raise ValueError(act)


def _inproj(h, w, cols, act, extra=None, out_dtype=BF16, tm=2048, tn=1024):
    m, d = h.shape
    nj = len(cols)
    c0, step = cols[0], (cols[1] - cols[0] if nj > 1 else 0)
    assert all(cols[j] == c0 + j * step for j in range(nj))
    in_specs = [pl.BlockSpec((tm, d), lambda j, i: (i, 0)),
                pl.BlockSpec((None, d, tn), lambda j, i: (0, 0, c0 + j * step))]
    args = [h, w]
    if act == "gate":
        in_specs.append(pl.BlockSpec((1, tn), lambda j, i: (0, j)))
        args.append(extra)
    elif act == "logf":
        in_specs.append(pl.BlockSpec(extra.shape, lambda j, i: (0, 0)))
        args.append(extra)
    return pl.pallas_call(
        functools.partial(_inproj_kernel, act=act),
        grid=(nj, m // tm),
        in_specs=in_specs,
        out_specs=pl.BlockSpec((tm, tn), lambda j, i: (i, j)),
        out_shape=jax.ShapeDtypeStruct((m, nj * tn), out_dtype),
        scratch_shapes=[pltpu.VMEM((d, tn), BF16)],
        compiler_params=pltpu.CompilerParams(dimension_semantics=("arbitrary", "arbitrary"),
                                             vmem_limit_bytes=VMEM_LIMIT),
        name="inproj_" + act,
    )(*args)


def _level_masks():
    t = np.arange(CHUNK)[:, None]
    s = np.arange(CHUNK)[None, :]
    out = []
    for m in LEVELS:
        out.append((t // (2 * m) == s // (2 * m)) & ((t // m) % 2 == 1) & ((s // m) % 2 == 0))
    out.append(t == s)
    return np.stack(out).astype(np.float32)


def _cumsum_matrix():
    t = np.arange(CHUNK)
    tri = (t[:, None] >= t[None, :]).astype(np.float32)
    blocks = [tri]
    for m in (2, 1):
        bnd = (t // (2 * m)) * (2 * m) + m - 1
        blocks.append(tri - tri[bnd])
    w = np.concatenate(blocks, axis=0)
    return np.concatenate([w, w, w], axis=1)


def _mixer_kernel(u_ref, v_ref, q_ref, og_ref, lf_ref, iv_ref, lng_ref, lnb_ref, ws_ref, bst_ref, ng_ref,
                  cw_ref, msk_ref, *rest, n_cast):
    src, (ya_ref, yb_ref), dst, st_ref = rest[:n_cast], rest[n_cast:n_cast + 2], rest[n_cast + 2:-1], rest[-1]
    rows = u_ref.shape[0]
    nchunk = rows // CHUNK
    ngroup = u_ref.shape[1] // GROUP

    @pl.when(pl.program_id(1) == 0)
    def _():
        st_ref[...] = jnp.zeros_like(st_ref)

    for s_ref, d_ref in zip(src, dst):
        d_ref[...] = s_ref[...].astype(BF16)

    v = v_ref[...].astype(F32)
    mu = jnp.mean(v, axis=-1, keepdims=True)
    vc = v - mu
    var = jnp.mean(vc * vc, axis=-1, keepdims=True)
    vn = (vc * lax.rsqrt(var + EPS) * lng_ref[...] + lnb_ref[...]).astype(BF16)
    tri = lax.broadcasted_iota(jnp.int32, (CHUNK, CHUNK), 0) >= lax.broadcasted_iota(jnp.int32, (CHUNK, CHUNK), 1)
    for g in range(ngroup):
        cols = slice(g * GROUP, (g + 1) * GROUP)
        w = jnp.where(tri, ws_ref[g], 0.0).astype(BF16)
        rhs = jnp.concatenate([vn[c * CHUNK:(c + 1) * CHUNK, cols] for c in range(nchunk)], axis=1)
        sg = _dot(w, rhs) + bst_ref[:, g:g + 1]
        for c in range(nchunk):
            r = slice(c * CHUNK, (c + 1) * CHUNK)
            ya_ref[r, cols] = (u_ref[r, cols].astype(F32) * sg[:, c * CHUNK:(c + 1) * CHUNK]).astype(BF16)

    for c in range(nchunk):
        r = slice(c * CHUNK, (c + 1) * CHUNK)
        lf_all = lf_ref[r, :]
        bd = _dot(cw_ref[...], jnp.concatenate(_split_bf16(lf_all, 3), axis=0))
        for h in range(ngroup):
            cols = slice(h * GROUP, (h + 1) * GROUP)
            b = bd[0:CHUNK, cols]
            small = {2: bd[CHUNK:2 * CHUNK, cols], 1: bd[2 * CHUNK:3 * CHUNK, cols]}
            qb = q_ref[r, cols]
            qf = qb.astype(F32)
            k = 1.0 - jnp.exp(lf_all[:, cols])
            iv = iv_ref[r, cols]
            b_last = b[CHUNK - 1:CHUNK, :]
            st = st_ref[h]
            o = _dot_nt((qf * jnp.exp(b)).astype(BF16), st.astype(BF16))
            a = msk_ref[len(LEVELS)] * _dot_nt(qb, k.astype(BF16))
            for li, m in enumerate(LEVELS):
                if m in small:
                    dl = small[m]
                else:
                    b3 = b.reshape(CHUNK // (2 * m), 2 * m, GROUP)
                    dl = (b3 - b3[:, m - 1:m, :]).reshape(CHUNK, GROUP)
                e = jnp.exp(-jnp.abs(dl))
                a = a + msk_ref[li] * _dot_nt((qf * e).astype(BF16), (k * e).astype(BF16))
            o = o + _dot(a.astype(BF16), iv)
            st_ref[h] = st * jnp.exp(b_last) + _dot_tn(iv, (k * jnp.exp(b_last - b)).astype(BF16))
            y = _rms(o) * ng_ref[:, cols] * og_ref[r, cols].astype(F32)
            yb_ref[r, cols] = y.astype(BF16)


def _mixer(zg, zs, lf, iv, ln_g, ln_b, ws, bs_t, norm_g, batch, casts, rows=256):
    m, width = lf.shape
    nr = m // batch // rows
    nsteps = batch * nr
    ngroup = width // GROUP
    cw = jnp.asarray(_cumsum_matrix(), BF16)
    msk = jnp.asarray(_level_masks(), F32)
    row_blk = lambda col: pl.BlockSpec((rows, width), lambda b, r: (b * nr + r, col))
    full = lambda a: pl.BlockSpec(a.shape, lambda b, r: (0,) * a.ndim)
    in_specs = [row_blk(0), row_blk(1), row_blk(0), row_blk(1), row_blk(0), row_blk(0),
                full(ln_g), full(ln_b), full(ws), full(bs_t), full(norm_g), full(cw), full(msk)]
    out_specs = [row_blk(0), row_blk(0)]
    out_shape = [jax.ShapeDtypeStruct((m, width), BF16), jax.ShapeDtypeStruct((m, width), BF16)]
    for a in casts:
        _, n, w = a.shape
        slab = n // nsteps
        assert slab * nsteps == n and slab % 16 == 0
        in_specs.append(pl.BlockSpec((None, slab, w), lambda b, r: (0, b * nr + r, 0)))
        out_specs.append(pl.BlockSpec((slab, w), lambda b, r: (b * nr + r, 0)))
        out_shape.append(jax.ShapeDtypeStruct((n, w), BF16))
    return pl.pallas_call(
        functools.partial(_mixer_kernel, n_cast=len(casts)),
        grid=(batch, nr),
        in_specs=in_specs,
        out_specs=out_specs,
        out_shape=out_shape,
        scratch_shapes=[pltpu.VMEM((ngroup, GROUP, GROUP), F32)],
        compiler_params=pltpu.CompilerParams(dimension_semantics=("arbitrary", "arbitrary"),
                                             vmem_limit_bytes=VMEM_LIMIT),
        name="mixer",
    )(zg, zg, zs, zs, lf, iv, ln_g, ln_b, ws, bs_t, norm_g, cw, msk, *casts)


def _merge_kernel(ya_ref, yb_ref, ga_ref, gb_ref, x_ref, mod_ref, wa_ref, wb_ref, wo_ref, g_ref, x1_ref, h2_ref):
    pa = _dot(ya_ref[...], wa_ref[...])
    pb = _dot(yb_ref[...], wb_ref[...])
    y = (ga_ref[...].astype(F32) * pa + gb_ref[...].astype(F32) * pb).astype(BF16)
    x1 = x_ref[0] + mod_ref[0, 2] * _dot(y, wo_ref[...])
    x1_ref[0] = x1
    hn = _rms(x1) * g_ref[...]
    h2_ref[0] = (hn * (1.0 + mod_ref[0, 4]) + mod_ref[0, 3]).astype(BF16)


def _merge(ya, yb, gates, x, mod6, wa, wb, wo, g, tm=256):
    b, s, d = x.shape
    nr = s // tm
    width = ya.shape[1]
    resident = lambda a: pl.BlockSpec(a.shape, lambda i, r: (0,) * a.ndim, pipeline_mode=pl.Buffered(1))
    return pl.pallas_call(
        _merge_kernel,
        grid=(b, nr),
        in_specs=[pl.BlockSpec((tm, width), lambda i, r: (i * nr + r, 0)),
                  pl.BlockSpec((tm, width), lambda i, r: (i * nr + r, 0)),
                  pl.BlockSpec((tm, d), lambda i, r: (i * nr + r, 0)),
                  pl.BlockSpec((tm, d), lambda i, r: (i * nr + r, 1)),
                  pl.BlockSpec((1, tm, d), lambda i, r: (i, r, 0)),
                  pl.BlockSpec((1, 6, 1, d), lambda i, r: (i, 0, 0, 0)),
                  resident(wa), resident(wb), resident(wo),
                  pl.BlockSpec((1, d), lambda i, r: (0, 0))],
        out_specs=[pl.BlockSpec((1, tm, d), lambda i, r: (i, r, 0)),
                   pl.BlockSpec((1, tm, d), lambda i, r: (i, r, 0))],
        out_shape=[jax.ShapeDtypeStruct((b, s, d), F32), jax.ShapeDtypeStruct((b, s, d), BF16)],
        compiler_params=pltpu.CompilerParams(dimension_semantics=("arbitrary", "arbitrary"),
                                             vmem_limit_bytes=VMEM_LIMIT),
        name="merge",
    )(ya, yb, gates, gates, x, mod6, wa, wb, wo, g)


def _ffn_kernel(h_ref, wa_ref, wu_ref, wo_ref, x1_ref, mod_ref, g_ref, o_ref, acc_ref, *, nt):
    t = pl.program_id(2)

    @pl.when(t == 0)
    def _():
        acc_ref[...] = jnp.zeros_like(acc_ref)

    @pl.when(t < nt)
    def _():
        h = h_ref[0]
        a = _dot(h, wa_ref[...])
        up = _dot(h, wu_ref[...])
        acc_ref[...] += _dot((jax.nn.silu(a) * up).astype(BF16), wo_ref[...])

    @pl.when(t >= nt)
    def _():
        half = o_ref.shape[1]
        rows = pl.ds(pl.multiple_of((t - nt) * half, half), half)
        x2 = x1_ref[0] + mod_ref[0, 5] * acc_ref[rows, :]
        o_ref[0] = _rms(x2) * g_ref[...]


def _ffn(h2, w_in, w_out, x1, mod6, g, tm=1024, th=512):
    b, s, d = x1.shape
    hidden = w_out.shape[0]
    nt = hidden // th
    half = tm // 2
    wt = lambda t: jnp.minimum(t, nt - 1)
    fin = lambda r, t: 2 * r + jnp.clip(t - nt, 0, 1)
    return pl.pallas_call(
        functools.partial(_ffn_kernel, nt=nt),
        grid=(b, s // tm, nt + 2),
        in_specs=[pl.BlockSpec((1, tm, d), lambda i, r, t: (i, r, 0)),
                  pl.BlockSpec((d, th), lambda i, r, t: (0, wt(t))),
                  pl.BlockSpec((d, th), lambda i, r, t: (0, nt + wt(t))),
                  pl.BlockSpec((th, d), lambda i, r, t: (wt(t), 0)),
                  pl.BlockSpec((1, half, d), lambda i, r, t: (i, fin(r, t), 0)),
                  pl.BlockSpec((1, 6, 1, d), lambda i, r, t: (i, 0, 0, 0)),
                  pl.BlockSpec((1, d), lambda i, r, t: (0, 0))],
        out_specs=pl.BlockSpec((1, half, d), lambda i, r, t: (i, fin(r, t), 0)),
        out_shape=jax.ShapeDtypeStruct((b, s, d), F32),
        scratch_shapes=[pltpu.VMEM((tm, d), F32)],
        compiler_params=pltpu.CompilerParams(dimension_semantics=("arbitrary", "arbitrary", "arbitrary"),
                                             vmem_limit_bytes=VMEM_LIMIT),
        name="ffn",
    )(h2, w_in, w_in, w_out, x1, mod6, g)


def kernel(x, c, w_ada, b_ada, norm1_g, w_in, b_gate, gmlp_ln_g, gmlp_ln_b, gmlp_ws, gmlp_bs, hg_lb, hg_norm_g,
           w_branch_gmlp, w_branch_hg, w_out, norm2_g, w_ffn_in, w_ffn_out, final_norm_g):
    batch, seq, d = x.shape
    depth = w_ada.shape[0]
    width = w_branch_gmlp.shape[1]
    assert depth == 1 and width == 8 * GROUP and gmlp_ws.shape[2] == CHUNK and seq % 256 == 0
    assert w_in.shape[2] == 6 * width + 2 * d and hg_lb.shape[0] == depth + 1
    m = batch * seq
    tn = 1024
    assert width == tn and d == 2 * tn

    c_pad = jnp.zeros((8, d), F32).at[:batch].set(c)
    mod = _ada(c_pad, w_ada, b_ada)
    mod6 = mod[:batch].reshape(batch, 6, 1, d)

    h1 = _prenorm(x, mod6, norm1_g).reshape(m, d)
    zg = _inproj(h1, w_in, (0, 1), "gelu")
    zs = _inproj(h1, w_in, (2, 5), "silu")
    lf = _inproj(h1, w_in, (3,), "logf", extra=hg_lb, out_dtype=F32)
    iv = _inproj(h1, w_in, (4,), "none")
    gates = _inproj(h1, w_in, (6, 7, 8, 9), "gate", extra=b_gate)

    ya, yb, w_a, w_b, w_o, w_fi, w_fo = _mixer(
        zg, zs, lf, iv, gmlp_ln_g, gmlp_ln_b, gmlp_ws[0], gmlp_bs[0].T, hg_norm_g, batch,
        casts=(w_branch_gmlp, w_branch_hg, w_out, w_ffn_in, w_ffn_out))

    x1, h2 = _merge(ya, yb, gates, x, mod6, w_a, w_b, w_o, norm2_g)
    return _ffn(h2, w_fi, w_fo, x1, mod6, final_norm_g.reshape(1, d))
```

```python
import functools

import numpy as np
import jax
import jax.numpy as jnp
from jax import lax
from jax.experimental import pallas as pl
from jax.experimental.pallas import tpu as pltpu

F32 = jnp.float32
BF16 = jnp.bfloat16
EPS = 1e-6
LOG2E = 1.4426950408889634

GROUP = 128
CHUNK = 128
HEADS_PER_PASS = 4
LEVELS = (64, 32, 16, 8, 4, 2, 1)
VMEM_LIMIT = 56 * 1024 * 1024


def _dot(a, b):
    return jnp.dot(a, b, preferred_element_type=F32)


def _dot_nt(a, b):
    return lax.dot_general(a, b, (((1,), (1,)), ((), ())), preferred_element_type=F32)


def _dot_tn(a, b):
    return lax.dot_general(a, b, (((0,), (0,)), ((), ())), preferred_element_type=F32)


def _rms(x):
    return x * lax.rsqrt(jnp.mean(x * x, axis=-1, keepdims=True) + EPS)


def _split_bf16(x, parts):
    out = []
    for _ in range(parts - 1):
        p = x.astype(BF16)
        out.append(p)
        x = x - p.astype(F32)
    out.append(x.astype(BF16))
    return out


def _ada_kernel(c_ref, w_ref, b_ref, o_ref):
    rows = c_ref.shape[0]
    ca = jnp.concatenate(_split_bf16(jax.nn.silu(c_ref[...]), 2), axis=0)
    acc = _dot(ca, w_ref[...].astype(BF16))
    o_ref[...] = acc[:rows] + acc[rows:] + b_ref[...]


def _ada(c_pad, w_ada, b_ada, tn=1024):
    _, d, n = w_ada.shape
    return pl.pallas_call(
        _ada_kernel,
        grid=(n // tn,),
        in_specs=[pl.BlockSpec((8, d), lambda j: (0, 0)),
                  pl.BlockSpec((None, d, tn), lambda j: (0, 0, j)),
                  pl.BlockSpec((1, tn), lambda j: (0, j))],
        out_specs=pl.BlockSpec((8, tn), lambda j: (0, j)),
        out_shape=jax.ShapeDtypeStruct((8, n), F32),
        compiler_params=pltpu.CompilerParams(dimension_semantics=("arbitrary",), vmem_limit_bytes=VMEM_LIMIT),
        name="ada",
    )(c_pad, w_ada, b_ada)


def _prenorm_kernel(x_ref, mod_ref, g_ref, h_ref):
    hn = _rms(x_ref[0]) * g_ref[...]
    h_ref[0] = (hn * (1.0 + mod_ref[0, 1]) + mod_ref[0, 0]).astype(BF16)


def _prenorm(x, mod6, g, tm=512):
    b, s, d = x.shape
    return pl.pallas_call(
        _prenorm_kernel,
        grid=(b, s // tm),
        in_specs=[pl.BlockSpec((1, tm, d), lambda i, r: (i, r, 0)),
                  pl.BlockSpec((1, 6, 1, d), lambda i, r: (i, 0, 0, 0)),
                  pl.BlockSpec((1, d), lambda i, r: (0, 0))],
        out_specs=pl.BlockSpec((1, tm, d), lambda i, r: (i, r, 0)),
        out_shape=jax.ShapeDtypeStruct((b, s, d), BF16),
        compiler_params=pltpu.CompilerParams(dimension_semantics=("arbitrary", "arbitrary"),
                                             vmem_limit_bytes=VMEM_LIMIT),
        name="prenorm",
    )(x, mod6, g)


def _inproj_kernel(h_ref, w_ref, *rest, act):
    extra, o_ref, wb_ref = rest[:-2], rest[-2], rest[-1]

    @pl.when(pl.program_id(1) == 0)
    def _():
        wb_ref[...] = w_ref[...].astype(BF16)

    acc = _dot(h_ref[...], wb_ref[...])
    if act == "gelu":
        o_ref[...] = jax.nn.gelu(acc).astype(o_ref.dtype)
    elif act == "silu":
        o_ref[...] = jax.nn.silu(acc).astype(o_ref.dtype)
    elif act == "none":
        o_ref[...] = acc.astype(o_ref.dtype)
    elif act == "gate":
        o_ref[...] = jax.nn.sigmoid(acc + extra[0][...]).astype(o_ref.dtype)
    elif act == "logf":
        p = jax.nn.softmax(extra[0][...], axis=0)
        lb = p[0:1, :]
        o_ref[...] = jnp.log(lb + (1.0 - lb) * jax.nn.sigmoid(acc))
    else:
        raise ValueError(act)


def _inproj(h, w, cols, act, extra=None, out_dtype=BF16, tm=1024, tn=1024):
    m, d = h.shape
    nj = len(cols)
    c0, step = cols[0], (cols[1] - cols[0] if nj > 1 else 0)
    assert all(cols[j] == c0 + j * step for j in range(nj))
    in_specs = [pl.BlockSpec((tm, d), lambda j, i: (i, 0)),
                pl.BlockSpec((None, d, tn), lambda j, i: (0, 0, c0 + j * step))]
    args = [h, w]
    if act == "gate":
        in_specs.append(pl.BlockSpec((1, tn), lambda j, i: (0, j)))
        args.append(extra)
    elif act == "logf":
        in_specs.append(pl.BlockSpec(extra.shape, lambda j, i: (0, 0)))
        args.append(extra)
    return pl.pallas_call(
        functools.partial(_inproj_kernel, act=act),
        grid=(nj, m // tm),
        in_specs=in_specs,
        out_specs=pl.BlockSpec((tm, tn), lambda j, i: (i, j)),
        out_shape=jax.ShapeDtypeStruct((m, nj * tn), out_dtype),
        scratch_shapes=[pltpu.VMEM((d, tn), BF16)],
        compiler_params=pltpu.CompilerParams(dimension_semantics=("arbitrary", "arbitrary"),
                                             vmem_limit_bytes=VMEM_LIMIT),
        name="inproj_" + act,
    )(*args)


def _level_masks():
    t = np.arange(CHUNK)[:, None]
    s = np.arange(CHUNK)[None, :]
    out = []
    for m in LEVELS:
        out.append((t // (2 * m) == s // (2 * m)) & ((t // m) % 2 == 1) & ((s // m) % 2 == 0))
    out.append(t == s)
    return np.stack(out).astype(np.float32)


def _decay_matrix():
    t = np.arange(CHUNK)[:, None]
    j = np.arange(CHUNK)[None, :]
    blocks = [j <= t]
    for m in LEVELS:
        r = (t // (2 * m)) * (2 * m) + m - 1
        blocks.append(np.where((t // m) % 2 == 1, (j > r) & (j <= t), (j > t) & (j <= r)))
    blocks.append(j > t)
    w = np.concatenate(blocks, axis=0).astype(np.float32)
    return np.concatenate([w, w], axis=1)


def _mixer_kernel(u_ref, v_ref, q_ref, og_ref, lf_ref, iv_ref, lng_ref, lnb_ref, ws_ref, bst_ref, ng_ref,
                  cw_ref, msk_ref, *rest, n_cast):
    src, (ya_ref, yb_ref), dst, st_ref = rest[:n_cast], rest[n_cast:n_cast + 2], rest[n_cast + 2:-1], rest[-1]
    rows = u_ref.shape[0]
    nchunk = rows // CHUNK
    ngroup = u_ref.shape[1] // GROUP

    @pl.when(pl.program_id(1) == 0)
    def _():
        st_ref[...] = jnp.zeros_like(st_ref)

    for s_ref, d_ref in zip(src, dst):
        d_ref[...] = s_ref[...].astype(BF16)

    v = v_ref[...].astype(F32)
    mu = jnp.mean(v, axis=-1, keepdims=True)
    vc = v - mu
    var = jnp.mean(vc * vc, axis=-1, keepdims=True)
    vn = (vc * lax.rsqrt(var + EPS) * lng_ref[...] + lnb_ref[...]).astype(BF16)
    tri = lax.broadcasted_iota(jnp.int32, (CHUNK, CHUNK), 0) >= lax.broadcasted_iota(jnp.int32, (CHUNK, CHUNK), 1)
    for g in range(ngroup):
        cols = slice(g * GROUP, (g + 1) * GROUP)
        w = jnp.where(tri, ws_ref[g], 0.0).astype(BF16)
        rhs = jnp.concatenate([vn[c * CHUNK:(c + 1) * CHUNK, cols] for c in range(nchunk)], axis=1)
        sg = _dot(w, rhs) + bst_ref[:, g:g + 1]
        for c in range(nchunk):
            r = slice(c * CHUNK, (c + 1) * CHUNK)
            ya_ref[r, cols] = (u_ref[r, cols].astype(F32) * sg[:, c * CHUNK:(c + 1) * CHUNK]).astype(BF16)

    nlev = len(LEVELS)
    for c in range(nchunk):
        r = slice(c * CHUNK, (c + 1) * CHUNK)
        lf2 = lf_ref[r, :] * LOG2E
        ex = _dot(cw_ref[...], jnp.concatenate(_split_bf16(lf2, 2), axis=0))
        for h0 in range(0, ngroup, HEADS_PER_PASS):
            hs = range(h0, h0 + HEADS_PER_PASS)
            cs = {h: slice(h * GROUP, (h + 1) * GROUP) for h in hs}
            e2 = lambda h, i: jnp.exp2(ex[i * CHUNK:(i + 1) * CHUNK, cs[h]]).astype(BF16)
            qb = {h: q_ref[r, cs[h]] for h in hs}
            kb = {h: (1.0 - jnp.exp2(lf2[:, cs[h]])).astype(BF16) for h in hs}
            st = {h: st_ref[h] for h in hs}
            o = {h: _dot_nt(qb[h] * e2(h, 0), st[h].astype(BF16)) for h in hs}
            a = {h: msk_ref[nlev] * _dot_nt(qb[h], kb[h]) for h in hs}
            for li in range(nlev):
                for h in hs:
                    e = e2(h, 1 + li)
                    a[h] = a[h] + msk_ref[li] * _dot_nt(qb[h] * e, kb[h] * e)
            for h in hs:
                iv = iv_ref[r, cs[h]]
                oh = o[h] + _dot(a[h].astype(BF16), iv)
                b_last = ex[CHUNK - 1:CHUNK, cs[h]]
                st_ref[h] = st[h] * jnp.exp2(b_last) + _dot_tn(iv, kb[h] * e2(h, nlev + 1))
                y = _rms(oh) * ng_ref[:, cs[h]] * og_ref[r, cs[h]].astype(F32)
                yb_ref[r, cs[h]] = y.astype(BF16)


def _mixer(zg, zs, lf, iv, ln_g, ln_b, ws, bs_t, norm_g, batch, casts, rows=256):
    m, width = lf.shape
    nr = m // batch // rows
    nsteps = batch * nr
    ngroup = width // GROUP
    cw = jnp.asarray(_decay_matrix(), BF16)
    msk = jnp.asarray(_level_masks(), F32)
    row_blk = lambda col: pl.BlockSpec((rows, width), lambda b, r: (b * nr + r, col))
    full = lambda a: pl.BlockSpec(a.shape, lambda b, r: (0,) * a.ndim)
    in_specs = [row_blk(0), row_blk(1), row_blk(0), row_blk(1), row_blk(0), row_blk(0),
                full(ln_g), full(ln_b), full(ws), full(bs_t), full(norm_g), full(cw), full(msk)]
    out_specs = [row_blk(0), row_blk(0)]
    out_shape = [jax.ShapeDtypeStruct((m, width), BF16), jax.ShapeDtypeStruct((m, width), BF16)]
    for a in casts:
        _, n, w = a.shape
        slab = n // nsteps
        assert slab * nsteps == n and slab % 16 == 0
        in_specs.append(pl.BlockSpec((None, slab, w), lambda b, r: (0, b * nr + r, 0)))
        out_specs.append(pl.BlockSpec((slab, w), lambda b, r: (b * nr + r, 0)))
        out_shape.append(jax.ShapeDtypeStruct((n, w), BF16))
    return pl.pallas_call(
        functools.partial(_mixer_kernel, n_cast=len(casts)),
        grid=(batch, nr),
        in_specs=in_specs,
        out_specs=out_specs,
        out_shape=out_shape,
        scratch_shapes=[pltpu.VMEM((ngroup, GROUP, GROUP), F32)],
        compiler_params=pltpu.CompilerParams(dimension_semantics=("arbitrary", "arbitrary"),
                                             vmem_limit_bytes=VMEM_LIMIT),
        name="mixer",
    )(zg, zg, zs, zs, lf, iv, ln_g, ln_b, ws, bs_t, norm_g, cw, msk, *casts)


def _merge_kernel(ya_ref, yb_ref, ga_ref, gb_ref, x_ref, mod_ref, wa_ref, wb_ref, wo_ref, g_ref, x1_ref, h2_ref):
    pa = _dot(ya_ref[...], wa_ref[...])
    pb = _dot(yb_ref[...], wb_ref[...])
    y = (ga_ref[...].astype(F32) * pa + gb_ref[...].astype(F32) * pb).astype(BF16)
    x1 = x_ref[0] + mod_ref[0, 2] * _dot(y, wo_ref[...])
    x1_ref[0] = x1
    hn = _rms(x1) * g_ref[...]
    h2_ref[0] = (hn * (1.0 + mod_ref[0, 4]) + mod_ref[0, 3]).astype(BF16)


def _merge(ya, yb, gates, x, mod6, wa, wb, wo, g, tm=256):
    b, s, d = x.shape
    nr = s // tm
    width = ya.shape[1]
    resident = lambda a: pl.BlockSpec(a.shape, lambda i, r: (0,) * a.ndim, pipeline_mode=pl.Buffered(1))
    return pl.pallas_call(
        _merge_kernel,
        grid=(b, nr),
        in_specs=[pl.BlockSpec((tm, width), lambda i, r: (i * nr + r, 0)),
                  pl.BlockSpec((tm, width), lambda i, r: (i * nr + r, 0)),
                  pl.BlockSpec((tm, d), lambda i, r: (i * nr + r, 0)),
                  pl.BlockSpec((tm, d), lambda i, r: (i * nr + r, 1)),
                  pl.BlockSpec((1, tm, d), lambda i, r: (i, r, 0)),
                  pl.BlockSpec((1, 6, 1, d), lambda i, r: (i, 0, 0, 0)),
                  resident(wa), resident(wb), resident(wo),
                  pl.BlockSpec((1, d), lambda i, r: (0, 0))],
        out_specs=[pl.BlockSpec((1, tm, d), lambda i, r: (i, r, 0)),
                   pl.BlockSpec((1, tm, d), lambda i, r: (i, r, 0))],
        out_shape=[jax.ShapeDtypeStruct((b, s, d), F32), jax.ShapeDtypeStruct((b, s, d), BF16)],
        compiler_params=pltpu.CompilerParams(dimension_semantics=("arbitrary", "arbitrary"),
                                             vmem_limit_bytes=VMEM_LIMIT),
        name="merge",
    )(ya, yb, gates, gates, x, mod6, wa, wb, wo, g)


def _ffn_kernel(h_ref, wa_ref, wu_ref, wo_ref, x1_ref, mod_ref, g_ref, o_ref, acc_ref, *, nt):
    t = pl.program_id(2)

    @pl.when(t == 0)
    def _():
        acc_ref[...] = jnp.zeros_like(acc_ref)

    @pl.when(t < nt)
    def _():
        h = h_ref[0]
        a = _dot(h, wa_ref[...])
        up = _dot(h, wu_ref[...])
        acc_ref[...] += _dot((jax.nn.silu(a) * up).astype(BF16), wo_ref[...])

    @pl.when(t >= nt)
    def _():
        half = o_ref.shape[1]
        rows = pl.ds(pl.multiple_of((t - nt) * half, half), half)
        x2 = x1_ref[0] + mod_ref[0, 5] * acc_ref[rows, :]
        o_ref[0] = _rms(x2) * g_ref[...]


def _ffn(h2, w_in, w_out, x1, mod6, g, tm=1024, th=512):
    b, s, d = x1.shape
    hidden = w_out.shape[0]
    nt = hidden // th
    half = tm // 2
    wt = lambda t: jnp.minimum(t, nt - 1)
    fin = lambda r, t: 2 * r + jnp.clip(t - nt, 0, 1)
    return pl.pallas_call(
        functools.partial(_ffn_kernel, nt=nt),
        grid=(b, s // tm, nt + 2),
        in_specs=[pl.BlockSpec((1, tm, d), lambda i, r, t: (i, r, 0)),
                  pl.BlockSpec((d, th), lambda i, r, t: (0, wt(t))),
                  pl.BlockSpec((d, th), lambda i, r, t: (0, nt + wt(t))),
                  pl.BlockSpec((th, d), lambda i, r, t: (wt(t), 0)),
                  pl.BlockSpec((1, half, d), lambda i, r, t: (i, fin(r, t), 0)),
                  pl.BlockSpec((1, 6, 1, d), lambda i, r, t: (i, 0, 0, 0)),
                  pl.BlockSpec((1, d), lambda i, r, t: (0, 0))],
        out_specs=pl.BlockSpec((1, half, d), lambda i, r, t: (i, fin(r, t), 0)),
        out_shape=jax.ShapeDtypeStruct((b, s, d), F32),
        scratch_shapes=[pltpu.VMEM((tm, d), F32)],
        compiler_params=pltpu.CompilerParams(dimension_semantics=("arbitrary", "arbitrary", "arbitrary"),
                                             vmem_limit_bytes=VMEM_LIMIT),
        name="ffn",
    )(h2, w_in, w_in, w_out, x1, mod6, g)


def kernel(x, c, w_ada, b_ada, norm1_g, w_in, b_gate, gmlp_ln_g, gmlp_ln_b, gmlp_ws, gmlp_bs, hg_lb, hg_norm_g,
           w_branch_gmlp, w_branch_hg, w_out, norm2_g, w_ffn_in, w_ffn_out, final_norm_g):
    batch, seq, d = x.shape
    depth = w_ada.shape[0]
    width = w_branch_gmlp.shape[1]
    assert depth == 1 and width == 8 * GROUP and gmlp_ws.shape[2] == CHUNK and seq % 256 == 0
    assert w_in.shape[2] == 6 * width + 2 * d and hg_lb.shape[0] == depth + 1
    m = batch * seq
    tn = 1024
    assert width == tn and d == 2 * tn

    c_pad = jnp.zeros((8, d), F32).at[:batch].set(c)
    mod = _ada(c_pad, w_ada, b_ada)
    mod6 = mod[:batch].reshape(batch, 6, 1, d)

    h1 = _prenorm(x, mod6, norm1_g).reshape(m, d)
    zg = _inproj(h1, w_in, (0, 1), "gelu")
    zs = _inproj(h1, w_in, (2, 5), "silu")
    lf = _inproj(h1, w_in, (3,), "logf", extra=hg_lb, out_dtype=F32)
    iv = _inproj(h1, w_in, (4,), "none")
    gates = _inproj(h1, w_in, (6, 7, 8, 9), "gate", extra=b_gate)

    ya, yb, w_a, w_b, w_o, w_fi, w_fo = _mixer(
        zg, zs, lf, iv, gmlp_ln_g, gmlp_ln_b, gmlp_ws[0], gmlp_bs[0].T, hg_norm_g, batch,
        casts=(w_branch_gmlp, w_branch_hg, w_out, w_ffn_in, w_ffn_out))

    x1, h2 = _merge(ya, yb, gates, x, mod6, w_a, w_b, w_o, norm2_g)
    return _ffn(h2, w_fi, w_fo, x1, mod6, final_norm_g.reshape(1, d))
```

```python
import functools

import numpy as np
import jax
import jax.numpy as jnp
from jax import lax
from jax.experimental import pallas as pl
from jax.experimental.pallas import tpu as pltpu

F32 = jnp.float32
BF16 = jnp.bfloat16
EPS = 1e-6
LOG2E = 1.4426950408889634

GROUP = 128
CHUNK = 128
HEADS_PER_PASS = 4
LEVELS = (64, 32, 16, 8, 4, 2, 1)
VMEM_LIMIT = 56 * 1024 * 1024


def _dot(a, b):
    return jnp.dot(a, b, preferred_element_type=F32)


def _dot_nt(a, b):
    return lax.dot_general(a, b, (((1,), (1,)), ((), ())), preferred_element_type=F32)


def _dot_tn(a, b):
    return lax.dot_general(a, b, (((0,), (0,)), ((), ())), preferred_element_type=F32)


def _rms(x):
    return x * lax.rsqrt(jnp.mean(x * x, axis=-1, keepdims=True) + EPS)


def _split_bf16(x, parts):
    out = []
    for _ in range(parts - 1):
        p = x.astype(BF16)
        out.append(p)
        x = x - p.astype(F32)
    out.append(x.astype(BF16))
    return out


def _ada_kernel(c_ref, w_ref, b_ref, o_ref):
    rows = c_ref.shape[0]
    ca = jnp.concatenate(_split_bf16(jax.nn.silu(c_ref[...]), 2), axis=0)
    acc = _dot(ca, w_ref[...].astype(BF16))
    o_ref[...] = acc[:rows] + acc[rows:] + b_ref[...]


def _ada(c_pad, w_ada, b_ada, tn=1024):
    _, d, n = w_ada.shape
    return pl.pallas_call(
        _ada_kernel,
        grid=(n // tn,),
        in_specs=[pl.BlockSpec((8, d), lambda j: (0, 0)),
                  pl.BlockSpec((None, d, tn), lambda j: (0, 0, j)),
                  pl.BlockSpec((1, tn), lambda j: (0, j))],
        out_specs=pl.BlockSpec((8, tn), lambda j: (0, j)),
        out_shape=jax.ShapeDtypeStruct((8, n), F32),
        compiler_params=pltpu.CompilerParams(dimension_semantics=("arbitrary",), vmem_limit_bytes=VMEM_LIMIT),
        name="ada",
    )(c_pad, w_ada, b_ada)


def _prenorm_kernel(x_ref, mod_ref, g_ref, h_ref):
    hn = _rms(x_ref[0]) * g_ref[...]
    h_ref[0] = (hn * (1.0 + mod_ref[0, 1]) + mod_ref[0, 0]).astype(BF16)


def _prenorm(x, mod6, g, tm=512):
    b, s, d = x.shape
    return pl.pallas_call(
        _prenorm_kernel,
        grid=(b, s // tm),
        in_specs=[pl.BlockSpec((1, tm, d), lambda i, r: (i, r, 0)),
                  pl.BlockSpec((1, 6, 1, d), lambda i, r: (i, 0, 0, 0)),
                  pl.BlockSpec((1, d), lambda i, r: (0, 0))],
        out_specs=pl.BlockSpec((1, tm, d), lambda i, r: (i, r, 0)),
        out_shape=jax.ShapeDtypeStruct((b, s, d), BF16),
        compiler_params=pltpu.CompilerParams(dimension_semantics=("arbitrary", "arbitrary"),
                                             vmem_limit_bytes=VMEM_LIMIT),
        name="prenorm",
    )(x, mod6, g)


IN_TILES = ((0, "gelu"), (1, "gelu"), (2, "silu"), (5, "silu"), (6, "gate"), (7, "gate"), (8, "gate"), (9, "gate"),
            (4, "none"), (3, "logf"))
Z_U, Z_V, Z_Q, Z_OG, Z_GATE, Z_IV = 0, 1, 2, 3, 4, 8


def _inproj_kernel(h_ref, w_ref, bg_ref, lb_ref, z_ref, lf_ref, wb_ref):
    j = pl.program_id(0)

    @pl.when(pl.program_id(1) == 0)
    def _():
        wb_ref[...] = w_ref[...].astype(BF16)

    def tile(act):
        acc = _dot(h_ref[...], wb_ref[...])
        if act == "gelu":
            z_ref[...] = jax.nn.gelu(acc).astype(BF16)
        elif act == "silu":
            z_ref[...] = jax.nn.silu(acc).astype(BF16)
        elif act == "gate":
            z_ref[...] = jax.nn.sigmoid(acc + bg_ref[...]).astype(BF16)
        elif act == "none":
            z_ref[...] = acc.astype(BF16)
        else:
            lb = jax.nn.softmax(lb_ref[...], axis=0)[0:1, :]
            lf_ref[...] = jnp.log(lb + (1.0 - lb) * jax.nn.sigmoid(acc))

    acts = [a for _, a in IN_TILES]
    for act in dict.fromkeys(acts):
        first = acts.index(act)
        last = len(acts) - 1 - acts[::-1].index(act)
        pl.when((j >= first) & (j <= last))(functools.partial(tile, act))


def _inproj(h, w, b_gate, hg_lb, tm=1024, tn=1024):
    m, d = h.shape
    ni, nj = m // tm, len(IN_TILES)
    acts = [a for _, a in IN_TILES]
    assert acts[-1] == "logf" and acts.count("logf") == 1
    gate0 = acts.index("gate")
    ngate = acts.count("gate")

    def wcol(j):
        col = j
        for k, (c, _) in enumerate(IN_TILES):
            col = jnp.where(j == k, c, col)
        return col

    last = nj - 1
    return pl.pallas_call(
        _inproj_kernel,
        grid=(nj, ni),
        in_specs=[pl.BlockSpec((tm, d), lambda j, i: (i, 0)),
                  pl.BlockSpec((None, d, tn), lambda j, i: (0, 0, wcol(j))),
                  pl.BlockSpec((1, tn), lambda j, i: (0, jnp.clip(j - gate0, 0, ngate - 1))),
                  pl.BlockSpec(hg_lb.shape, lambda j, i: (0, 0))],
        out_specs=[pl.BlockSpec((tm, tn), lambda j, i: (jnp.where(j < last, i, ni - 1), jnp.minimum(j, last - 1))),
                   pl.BlockSpec((tm, tn), lambda j, i: (jnp.where(j < last, 0, i), 0))],
        out_shape=[jax.ShapeDtypeStruct((m, (nj - 1) * tn), BF16), jax.ShapeDtypeStruct((m, tn), F32)],
        scratch_shapes=[pltpu.VMEM((d, tn), BF16)],
        compiler_params=pltpu.CompilerParams(dimension_semantics=("arbitrary", "arbitrary"),
                                             vmem_limit_bytes=VMEM_LIMIT),
        name="inproj",
    )(h, w, b_gate, hg_lb)


def _level_masks():
    t = np.arange(CHUNK)[:, None]
    s = np.arange(CHUNK)[None, :]
    out = []
    for m in LEVELS:
        out.append((t // (2 * m) == s // (2 * m)) & ((t // m) % 2 == 1) & ((s // m) % 2 == 0))
    out.append(t == s)
    return np.stack(out).astype(np.float32)


def _decay_matrix():
    t = np.arange(CHUNK)[:, None]
    j = np.arange(CHUNK)[None, :]
    blocks = [j <= t]
    for m in LEVELS:
        r = (t // (2 * m)) * (2 * m) + m - 1
        blocks.append(np.where((t // m) % 2 == 1, (j > r) & (j <= t), (j > t) & (j <= r)))
    blocks.append(j > t)
    w = np.concatenate(blocks, axis=0).astype(np.float32)
    return np.concatenate([w, w], axis=1)


def _mixer_kernel(u_ref, v_ref, q_ref, og_ref, lf_ref, iv_ref, lng_ref, lnb_ref, ws_ref, bst_ref, ng_ref,
                  cw_ref, msk_ref, *rest, n_cast):
    src, (ya_ref, yb_ref), dst, st_ref = rest[:n_cast], rest[n_cast:n_cast + 2], rest[n_cast + 2:-1], rest[-1]
    rows = u_ref.shape[0]
    nchunk = rows // CHUNK
    ngroup = u_ref.shape[1] // GROUP

    @pl.when(pl.program_id(1) == 0)
    def _():
        st_ref[...] = jnp.zeros_like(st_ref)

    for s_ref, d_ref in zip(src, dst):
        d_ref[...] = s_ref[...].astype(BF16)

    v = v_ref[...].astype(F32)
    mu = jnp.mean(v, axis=-1, keepdims=True)
    vc = v - mu
    var = jnp.mean(vc * vc, axis=-1, keepdims=True)
    vn = (vc * lax.rsqrt(var + EPS) * lng_ref[...] + lnb_ref[...]).astype(BF16)
    tri = lax.broadcasted_iota(jnp.int32, (CHUNK, CHUNK), 0) >= lax.broadcasted_iota(jnp.int32, (CHUNK, CHUNK), 1)
    for g in range(ngroup):
        cols = slice(g * GROUP, (g + 1) * GROUP)
        w = jnp.where(tri, ws_ref[g], 0.0).astype(BF16)
        rhs = jnp.concatenate([vn[c * CHUNK:(c + 1) * CHUNK, cols] for c in range(nchunk)], axis=1)
        sg = _dot(w, rhs) + bst_ref[:, g:g + 1]
        for c in range(nchunk):
            r = slice(c * CHUNK, (c + 1) * CHUNK)
            ya_ref[r, cols] = (u_ref[r, cols].astype(F32) * sg[:, c * CHUNK:(c + 1) * CHUNK]).astype(BF16)

    nlev = len(LEVELS)
    for c in range(nchunk):
        r = slice(c * CHUNK, (c + 1) * CHUNK)
        lf2 = lf_ref[r, :] * LOG2E
        ex = _dot(cw_ref[...], jnp.concatenate(_split_bf16(lf2, 2), axis=0))
        for h0 in range(0, ngroup, HEADS_PER_PASS):
            hs = range(h0, h0 + HEADS_PER_PASS)
            cs = {h: slice(h * GROUP, (h + 1) * GROUP) for h in hs}
            e2 = lambda h, i: jnp.exp2(ex[i * CHUNK:(i + 1) * CHUNK, cs[h]]).astype(BF16)
            qb = {h: q_ref[r, cs[h]] for h in hs}
            kb = {h: (1.0 - jnp.exp2(lf2[:, cs[h]])).astype(BF16) for h in hs}
            st = {h: st_ref[h] for h in hs}
            o = {h: _dot_nt(qb[h] * e2(h, 0), st[h].astype(BF16)) for h in hs}
            a = {h: msk_ref[nlev] * _dot_nt(qb[h], kb[h]) for h in hs}
            for li in range(nlev):
                for h in hs:
                    e = e2(h, 1 + li)
                    a[h] = a[h] + msk_ref[li] * _dot_nt(qb[h] * e, kb[h] * e)
            for h in hs:
                iv = iv_ref[r, cs[h]]
                oh = o[h] + _dot(a[h].astype(BF16), iv)
                b_last = ex[CHUNK - 1:CHUNK, cs[h]]
                st_ref[h] = st[h] * jnp.exp2(b_last) + _dot_tn(iv, kb[h] * e2(h, nlev + 1))
                y = _rms(oh) * ng_ref[:, cs[h]] * og_ref[r, cs[h]].astype(F32)
                yb_ref[r, cs[h]] = y.astype(BF16)


def _mixer(z, lf, ln_g, ln_b, ws, bs_t, norm_g, batch, casts, rows=256):
    m, width = lf.shape
    nr = m // batch // rows
    nsteps = batch * nr
    ngroup = width // GROUP
    cw = jnp.asarray(_decay_matrix(), BF16)
    msk = jnp.asarray(_level_masks(), F32)
    row_blk = lambda col: pl.BlockSpec((rows, width), lambda b, r: (b * nr + r, col))
    full = lambda a: pl.BlockSpec(a.shape, lambda b, r: (0,) * a.ndim)
    in_specs = [row_blk(Z_U), row_blk(Z_V), row_blk(Z_Q), row_blk(Z_OG), row_blk(0), row_blk(Z_IV),
                full(ln_g), full(ln_b), full(ws), full(bs_t), full(norm_g), full(cw), full(msk)]
    out_specs = [row_blk(0), row_blk(0)]
    out_shape = [jax.ShapeDtypeStruct((m, width), BF16), jax.ShapeDtypeStruct((m, width), BF16)]
    for a in casts:
        _, n, w = a.shape
        slab = n // nsteps
        assert slab * nsteps == n and slab % 16 == 0
        in_specs.append(pl.BlockSpec((None, slab, w), lambda b, r: (0, b * nr + r, 0)))
        out_specs.append(pl.BlockSpec((slab, w), lambda b, r: (b * nr + r, 0)))
        out_shape.append(jax.ShapeDtypeStruct((n, w), BF16))
    return pl.pallas_call(
        functools.partial(_mixer_kernel, n_cast=len(casts)),
        grid=(batch, nr),
        in_specs=in_specs,
        out_specs=out_specs,
        out_shape=out_shape,
        scratch_shapes=[pltpu.VMEM((ngroup, GROUP, GROUP), F32)],
        compiler_params=pltpu.CompilerParams(dimension_semantics=("arbitrary", "arbitrary"),
                                             vmem_limit_bytes=VMEM_LIMIT),
        name="mixer",
    )(z, z, z, z, lf, z, ln_g, ln_b, ws, bs_t, norm_g, cw, msk, *casts)


def _merge_kernel(ya_ref, yb_ref, ga_ref, gb_ref, x_ref, mod_ref, wa_ref, wb_ref, wo_ref, g_ref, x1_ref, h2_ref):
    pa = _dot(ya_ref[...], wa_ref[...])
    pb = _dot(yb_ref[...], wb_ref[...])
    y = (ga_ref[...].astype(F32) * pa + gb_ref[...].astype(F32) * pb).astype(BF16)
    x1 = x_ref[0] + mod_ref[0, 2] * _dot(y, wo_ref[...])
    x1_ref[0] = x1
    hn = _rms(x1) * g_ref[...]
    h2_ref[0] = (hn * (1.0 + mod_ref[0, 4]) + mod_ref[0, 3]).astype(BF16)


def _merge(ya, yb, z, x, mod6, wa, wb, wo, g, tm=256):
    b, s, d = x.shape
    nr = s // tm
    width = ya.shape[1]
    resident = lambda a: pl.BlockSpec(a.shape, lambda i, r: (0,) * a.ndim, pipeline_mode=pl.Buffered(1))
    return pl.pallas_call(
        _merge_kernel,
        grid=(b, nr),
        in_specs=[pl.BlockSpec((tm, width), lambda i, r: (i * nr + r, 0)),
                  pl.BlockSpec((tm, width), lambda i, r: (i * nr + r, 0)),
                  pl.BlockSpec((tm, d), lambda i, r: (i * nr + r, Z_GATE * width // d)),
                  pl.BlockSpec((tm, d), lambda i, r: (i * nr + r, Z_GATE * width // d + 1)),
                  pl.BlockSpec((1, tm, d), lambda i, r: (i, r, 0)),
                  pl.BlockSpec((1, 6, 1, d), lambda i, r: (i, 0, 0, 0)),
                  resident(wa), resident(wb), resident(wo),
                  pl.BlockSpec((1, d), lambda i, r: (0, 0))],
        out_specs=[pl.BlockSpec((1, tm, d), lambda i, r: (i, r, 0)),
                   pl.BlockSpec((1, tm, d), lambda i, r: (i, r, 0))],
        out_shape=[jax.ShapeDtypeStruct((b, s, d), F32), jax.ShapeDtypeStruct((b, s, d), BF16)],
        compiler_params=pltpu.CompilerParams(dimension_semantics=("arbitrary", "arbitrary"),
                                             vmem_limit_bytes=VMEM_LIMIT),
        name="merge",
    )(ya, yb, z, z, x, mod6, wa, wb, wo, g)


def _ffn_kernel(h_ref, wa_ref, wu_ref, wo_ref, x1_ref, mod_ref, g_ref, o_ref, acc_ref, *, nt):
    t = pl.program_id(2)

    @pl.when(t == 0)
    def _():
        acc_ref[...] = jnp.zeros_like(acc_ref)

    @pl.when(t < nt)
    def _():
        h = h_ref[0]
        a = _dot(h, wa_ref[...])
        up = _dot(h, wu_ref[...])
        acc_ref[...] += _dot((jax.nn.silu(a) * up).astype(BF16), wo_ref[...])

    @pl.when(t >= nt)
    def _():
        half = o_ref.shape[1]
        rows = pl.ds(pl.multiple_of((t - nt) * half, half), half)
        x2 = x1_ref[0] + mod_ref[0, 5] * acc_ref[rows, :]
        o_ref[0] = _rms(x2) * g_ref[...]


def _ffn(h2, w_in, w_out, x1, mod6, g, tm=1024, th=512):
    b, s, d = x1.shape
    hidden = w_out.shape[0]
    nt = hidden // th
    half = tm // 2
    wt = lambda t: jnp.minimum(t, nt - 1)
    fin = lambda r, t: 2 * r + jnp.clip(t - nt, 0, 1)
    return pl.pallas_call(
        functools.partial(_ffn_kernel, nt=nt),
        grid=(b, s // tm, nt + 2),
        in_specs=[pl.BlockSpec((1, tm, d), lambda i, r, t: (i, r, 0)),
                  pl.BlockSpec((d, th), lambda i, r, t: (0, wt(t))),
                  pl.BlockSpec((d, th), lambda i, r, t: (0, nt + wt(t))),
                  pl.BlockSpec((th, d), lambda i, r, t: (wt(t), 0)),
                  pl.BlockSpec((1, half, d), lambda i, r, t: (i, fin(r, t), 0)),
                  pl.BlockSpec((1, 6, 1, d), lambda i, r, t: (i, 0, 0, 0)),
                  pl.BlockSpec((1, d), lambda i, r, t: (0, 0))],
        out_specs=pl.BlockSpec((1, half, d), lambda i, r, t: (i, fin(r, t), 0)),
        out_shape=jax.ShapeDtypeStruct((b, s, d), F32),
        scratch_shapes=[pltpu.VMEM((tm, d), F32)],
        compiler_params=pltpu.CompilerParams(dimension_semantics=("arbitrary", "arbitrary", "arbitrary"),
                                             vmem_limit_bytes=VMEM_LIMIT),
        name="ffn",
    )(h2, w_in, w_in, w_out, x1, mod6, g)


def kernel(x, c, w_ada, b_ada, norm1_g, w_in, b_gate, gmlp_ln_g, gmlp_ln_b, gmlp_ws, gmlp_bs, hg_lb, hg_norm_g,
           w_branch_gmlp, w_branch_hg, w_out, norm2_g, w_ffn_in, w_ffn_out, final_norm_g):
    batch, seq, d = x.shape
    depth = w_ada.shape[0]
    width = w_branch_gmlp.shape[1]
    assert depth == 1 and width == 8 * GROUP and gmlp_ws.shape[2] == CHUNK and seq % 256 == 0
    assert w_in.shape[2] == 6 * width + 2 * d and hg_lb.shape[0] == depth + 1
    m = batch * seq
    tn = 1024
    assert width == tn and d == 2 * tn

    c_pad = jnp.zeros((8, d), F32).at[:batch].set(c)
    mod = _ada(c_pad, w_ada, b_ada)
    mod6 = mod[:batch].reshape(batch, 6, 1, d)

    h1 = _prenorm(x, mod6, norm1_g).reshape(m, d)
    z, lf = _inproj(h1, w_in, b_gate, hg_lb)

    ya, yb, w_a, w_b, w_o, w_fi, w_fo = _mixer(
        z, lf, gmlp_ln_g, gmlp_ln_b, gmlp_ws[0], gmlp_bs[0].T, hg_norm_g, batch,
        casts=(w_branch_gmlp, w_branch_hg, w_out, w_ffn_in, w_ffn_out))

    x1, h2 = _merge(ya, yb, z, x, mod6, w_a, w_b, w_o, norm2_g)
    return _ffn(h2, w_fi, w_fo, x1, mod6, final_norm_g.reshape(1, d))
```

```python
import functools

import numpy as np
import jax
import jax.numpy as jnp
from jax import lax
from jax.experimental import pallas as pl
from jax.experimental.pallas import tpu as pltpu

F32 = jnp.float32
BF16 = jnp.bfloat16
EPS = 1e-6
LOG2E = 1.4426950408889634
GELU_C1 = 0.7978845608028654
GELU_C3 = 0.044715 * GELU_C1

GROUP = 128
CHUNK = 128
MERGE_PARTS = 2
ROW_GROUP = 16
HEADS_PER_PASS = 4
LEVELS = (64, 32, 16, 8, 4, 2, 1)
VMEM_LIMIT = 56 * 1024 * 1024


def _dot(a, b):
    return jnp.dot(a, b, preferred_element_type=F32)


def _dot_nt(a, b):
    return lax.dot_general(a, b, (((1,), (1,)), ((), ())), preferred_element_type=F32)


def _dot_tn(a, b):
    return lax.dot_general(a, b, (((0,), (0,)), ((), ())), preferred_element_type=F32)


def _rms(x):
    return x * lax.rsqrt(jnp.mean(x * x, axis=-1, keepdims=True) + EPS)


def _split_bf16(x, parts):
    out = []
    for _ in range(parts - 1):
        p = x.astype(BF16)
        out.append(p)
        x = x - p.astype(F32)
    out.append(x.astype(BF16))
    return out


def _ada_kernel(c_ref, w_ref, b_ref, o_ref):
    rows = c_ref.shape[0]
    ca = jnp.concatenate(_split_bf16(jax.nn.silu(c_ref[...]), 2), axis=0)
    acc = _dot(ca, w_ref[...].astype(BF16))
    o_ref[...] = acc[:rows] + acc[rows:] + b_ref[...]


def _ada(c_pad, w_ada, b_ada, tn=1024):
    _, d, n = w_ada.shape
    return pl.pallas_call(
        _ada_kernel,
        grid=(n // tn,),
        in_specs=[pl.BlockSpec((8, d), lambda j: (0, 0)),
                  pl.BlockSpec((None, d, tn), lambda j: (0, 0, j)),
                  pl.BlockSpec((1, tn), lambda j: (0, j))],
        out_specs=pl.BlockSpec((8, tn), lambda j: (0, j)),
        out_shape=jax.ShapeDtypeStruct((8, n), F32),
        compiler_params=pltpu.CompilerParams(dimension_semantics=("arbitrary",), vmem_limit_bytes=VMEM_LIMIT),
        name="ada",
    )(c_pad, w_ada, b_ada)


def _prenorm_kernel(x_ref, mod_ref, g_ref, h_ref):
    scale = g_ref[...] * (1.0 + mod_ref[0, 1])
    shift = mod_ref[0, 0]

    def body(i, carry):
        rows = pl.ds(pl.multiple_of(i * ROW_GROUP, ROW_GROUP), ROW_GROUP)
        h_ref[0, rows, :] = (_rms(x_ref[0, rows, :]) * scale + shift).astype(BF16)
        return carry

    lax.fori_loop(0, x_ref.shape[1] // ROW_GROUP, body, 0, unroll=8)


def _prenorm(x, mod6, g, tm=512):
    b, s, d = x.shape
    return pl.pallas_call(
        _prenorm_kernel,
        grid=(b, s // tm),
        in_specs=[pl.BlockSpec((1, tm, d), lambda i, r: (i, r, 0)),
                  pl.BlockSpec((1, 6, 1, d), lambda i, r: (i, 0, 0, 0)),
                  pl.BlockSpec((1, d), lambda i, r: (0, 0))],
        out_specs=pl.BlockSpec((1, tm, d), lambda i, r: (i, r, 0)),
        out_shape=jax.ShapeDtypeStruct((b, s, d), BF16),
        compiler_params=pltpu.CompilerParams(dimension_semantics=("arbitrary", "arbitrary"),
                                             vmem_limit_bytes=VMEM_LIMIT),
        name="prenorm",
    )(x, mod6, g)


IN_TILES = ((0, "gelu"), (1, "gelu"), (2, "silu"), (5, "silu"), (6, "gate"), (7, "gate"), (8, "gate"), (9, "gate"),
            (4, "none"), (3, "logf"))
Z_U, Z_V, Z_Q, Z_OG, Z_GATE, Z_IV = 0, 1, 2, 3, 4, 8


def _inproj_kernel(h_ref, w_ref, bg_ref, lb_ref, z_ref, lf_ref, wb_ref):
    j = pl.program_id(0)

    @pl.when(pl.program_id(1) == 0)
    def _():
        wb_ref[...] = w_ref[...].astype(BF16)

    def tile(act):
        acc = _dot(h_ref[...], wb_ref[...])
        if act == "gelu":
            hx = 0.5 * acc
            t = acc * (GELU_C1 + GELU_C3 * (acc * acc))
            z_ref[...] = (hx * jnp.tanh(t) + hx).astype(BF16)
        elif act == "silu":
            hx = 0.5 * acc
            z_ref[...] = (hx * jnp.tanh(hx) + hx).astype(BF16)
        elif act == "gate":
            z_ref[...] = (0.5 * jnp.tanh(0.5 * (acc + bg_ref[...])) + 0.5).astype(BF16)
        elif act == "none":
            z_ref[...] = acc.astype(BF16)
        else:
            lb = jax.nn.softmax(lb_ref[...], axis=0)[0:1, :]
            lf_ref[...] = jnp.log((0.5 + 0.5 * lb) + (0.5 - 0.5 * lb) * jnp.tanh(0.5 * acc))

    acts = [a for _, a in IN_TILES]
    for act in dict.fromkeys(acts):
        first = acts.index(act)
        last = len(acts) - 1 - acts[::-1].index(act)
        pl.when((j >= first) & (j <= last))(functools.partial(tile, act))


def _inproj(h, w, b_gate, hg_lb, tm=1024, tn=1024):
    m, d = h.shape
    ni, nj = m // tm, len(IN_TILES)
    acts = [a for _, a in IN_TILES]
    assert acts[-1] == "logf" and acts.count("logf") == 1
    gate0 = acts.index("gate")
    ngate = acts.count("gate")

    def wcol(j):
        col = j
        for k, (c, _) in enumerate(IN_TILES):
            col = jnp.where(j == k, c, col)
        return col

    last = nj - 1
    return pl.pallas_call(
        _inproj_kernel,
        grid=(nj, ni),
        in_specs=[pl.BlockSpec((tm, d), lambda j, i: (i, 0)),
                  pl.BlockSpec((None, d, tn), lambda j, i: (0, 0, wcol(j))),
                  pl.BlockSpec((1, tn), lambda j, i: (0, jnp.clip(j - gate0, 0, ngate - 1))),
                  pl.BlockSpec(hg_lb.shape, lambda j, i: (0, 0))],
        out_specs=[pl.BlockSpec((tm, tn), lambda j, i: (jnp.where(j < last, i, ni - 1), jnp.minimum(j, last - 1))),
                   pl.BlockSpec((tm, tn), lambda j, i: (jnp.where(j < last, 0, i), 0))],
        out_shape=[jax.ShapeDtypeStruct((m, (nj - 1) * tn), BF16), jax.ShapeDtypeStruct((m, tn), F32)],
        scratch_shapes=[pltpu.VMEM((d, tn), BF16)],
        compiler_params=pltpu.CompilerParams(dimension_semantics=("arbitrary", "arbitrary"),
                                             vmem_limit_bytes=VMEM_LIMIT),
        name="inproj",
    )(h, w, b_gate, hg_lb)


def _level_masks():
    t = np.arange(CHUNK)[:, None]
    s = np.arange(CHUNK)[None, :]
    out = []
    for m in LEVELS:
        out.append((t // (2 * m) == s // (2 * m)) & ((t // m) % 2 == 1) & ((s // m) % 2 == 0))
    out.append(t == s)
    return np.stack(out).astype(np.float32)


def _decay_matrix():
    t = np.arange(CHUNK)[:, None]
    j = np.arange(CHUNK)[None, :]
    blocks = [j <= t]
    for m in LEVELS:
        r = (t // (2 * m)) * (2 * m) + m - 1
        blocks.append(np.where((t // m) % 2 == 1, (j > r) & (j <= t), (j > t) & (j <= r)))
    blocks.append(j > t)
    w = np.concatenate(blocks, axis=0).astype(np.float32)
    return np.concatenate([w, w], axis=1)


def _mixer_kernel(u_ref, v_ref, q_ref, og_ref, lf_ref, iv_ref, lng_ref, lnb_ref, ws_ref, bst_ref, ng_ref,
                  cw_ref, msk_ref, *rest, n_cast):
    src, (ya_ref, yb_ref), dst, st_ref = rest[:n_cast], rest[n_cast:n_cast + 2], rest[n_cast + 2:-1], rest[-1]
    rows = u_ref.shape[0]
    nchunk = rows // CHUNK
    ngroup = u_ref.shape[1] // GROUP

    @pl.when(pl.program_id(1) == 0)
    def _():
        st_ref[...] = jnp.zeros_like(st_ref)

    for s_ref, d_ref in zip(src, dst):
        d_ref[...] = s_ref[...].astype(BF16)

    v = v_ref[...].astype(F32)
    mu = jnp.mean(v, axis=-1, keepdims=True)
    vc = v - mu
    var = jnp.mean(vc * vc, axis=-1, keepdims=True)
    vn = (vc * lax.rsqrt(var + EPS) * lng_ref[...] + lnb_ref[...]).astype(BF16)
    tri = lax.broadcasted_iota(jnp.int32, (CHUNK, CHUNK), 0) >= lax.broadcasted_iota(jnp.int32, (CHUNK, CHUNK), 1)
    for g in range(ngroup):
        cols = slice(g * GROUP, (g + 1) * GROUP)
        w = jnp.where(tri, ws_ref[g], 0.0).astype(BF16)
        rhs = jnp.concatenate([vn[c * CHUNK:(c + 1) * CHUNK, cols] for c in range(nchunk)], axis=1)
        sg = _dot(w, rhs) + bst_ref[:, g:g + 1]
        for c in range(nchunk):
            r = slice(c * CHUNK, (c + 1) * CHUNK)
            ya_ref[r, cols] = (u_ref[r, cols].astype(F32) * sg[:, c * CHUNK:(c + 1) * CHUNK]).astype(BF16)

    nlev = len(LEVELS)
    for c in range(nchunk):
        r = slice(c * CHUNK, (c + 1) * CHUNK)
        lf2 = lf_ref[r, :] * LOG2E
        ex = _dot(cw_ref[...], jnp.concatenate(_split_bf16(lf2, 2), axis=0))
        for h0 in range(0, ngroup, HEADS_PER_PASS):
            hs = range(h0, h0 + HEADS_PER_PASS)
            cs = {h: slice(h * GROUP, (h + 1) * GROUP) for h in hs}
            e2 = lambda h, i: jnp.exp2(ex[i * CHUNK:(i + 1) * CHUNK, cs[h]]).astype(BF16)
            qb = {h: q_ref[r, cs[h]] for h in hs}
            kb = {h: (1.0 - jnp.exp2(lf2[:, cs[h]])).astype(BF16) for h in hs}
            st = {h: st_ref[h] for h in hs}
            o = {h: _dot_nt(qb[h] * e2(h, 0), st[h].astype(BF16)) for h in hs}
            a = {h: msk_ref[nlev] * _dot_nt(qb[h], kb[h]) for h in hs}
            for li in range(nlev):
                for h in hs:
                    e = e2(h, 1 + li)
                    a[h] = a[h] + msk_ref[li] * _dot_nt(qb[h] * e, kb[h] * e)
            for h in hs:
                iv = iv_ref[r, cs[h]]
                oh = o[h] + _dot(a[h].astype(BF16), iv)
                b_last = ex[CHUNK - 1:CHUNK, cs[h]]
                st_ref[h] = st[h] * jnp.exp2(b_last) + _dot_tn(iv, kb[h] * e2(h, nlev + 1))
                y = _rms(oh) * ng_ref[:, cs[h]] * og_ref[r, cs[h]].astype(F32)
                yb_ref[r, cs[h]] = y.astype(BF16)


def _mixer(z, lf, ln_g, ln_b, ws, bs_t, norm_g, batch, casts, rows=256):
    m, width = lf.shape
    nr = m // batch // rows
    nsteps = batch * nr
    ngroup = width // GROUP
    cw = jnp.asarray(_decay_matrix(), BF16)
    msk = jnp.asarray(_level_masks(), F32)
    row_blk = lambda col: pl.BlockSpec((rows, width), lambda b, r: (b * nr + r, col))
    full = lambda a: pl.BlockSpec(a.shape, lambda b, r: (0,) * a.ndim)
    in_specs = [row_blk(Z_U), row_blk(Z_V), row_blk(Z_Q), row_blk(Z_OG), row_blk(0), row_blk(Z_IV),
                full(ln_g), full(ln_b), full(ws), full(bs_t), full(norm_g), full(cw), full(msk)]
    out_specs = [row_blk(0), row_blk(0)]
    out_shape = [jax.ShapeDtypeStruct((m, width), BF16), jax.ShapeDtypeStruct((m, width), BF16)]
    for a in casts:
        _, n, w = a.shape
        slab = n // nsteps
        assert slab * nsteps == n and slab % 16 == 0
        in_specs.append(pl.BlockSpec((None, slab, w), lambda b, r: (0, b * nr + r, 0)))
        out_specs.append(pl.BlockSpec((slab, w), lambda b, r: (b * nr + r, 0)))
        out_shape.append(jax.ShapeDtypeStruct((n, w), BF16))
    return pl.pallas_call(
        functools.partial(_mixer_kernel, n_cast=len(casts)),
        grid=(batch, nr),
        in_specs=in_specs,
        out_specs=out_specs,
        out_shape=out_shape,
        scratch_shapes=[pltpu.VMEM((ngroup, GROUP, GROUP), F32)],
        compiler_params=pltpu.CompilerParams(dimension_semantics=("arbitrary", "arbitrary"),
                                             vmem_limit_bytes=VMEM_LIMIT),
        name="mixer",
    )(z, z, z, z, lf, z, ln_g, ln_b, ws, bs_t, norm_g, cw, msk, *casts)


def _merge_kernel(ya_ref, yb_ref, ga_ref, gb_ref, x_ref, mod_ref, wa_ref, wb_ref, wo_ref, g_ref, x1_ref, h2_ref):
    tm = ya_ref.shape[0]
    scale = g_ref[...] * (1.0 + mod_ref[0, 4])
    for p in range(MERGE_PARTS):
        r = slice(p * tm // MERGE_PARTS, (p + 1) * tm // MERGE_PARTS)
        pa = _dot(ya_ref[r, :], wa_ref[...])
        pb = _dot(yb_ref[r, :], wb_ref[...])
        y = (ga_ref[r, :].astype(F32) * pa + gb_ref[r, :].astype(F32) * pb).astype(BF16)
        x1 = x_ref[0, r, :] + mod_ref[0, 2] * _dot(y, wo_ref[...])
        x1_ref[0, r, :] = x1
        h2_ref[0, r, :] = (_rms(x1) * scale + mod_ref[0, 3]).astype(BF16)


def _merge(ya, yb, z, x, mod6, wa, wb, wo, g, tm=512):
    b, s, d = x.shape
    nr = s // tm
    width = ya.shape[1]
    resident = lambda a: pl.BlockSpec(a.shape, lambda i, r: (0,) * a.ndim, pipeline_mode=pl.Buffered(1))
    return pl.pallas_call(
        _merge_kernel,
        grid=(b, nr),
        in_specs=[pl.BlockSpec((tm, width), lambda i, r: (i * nr + r, 0)),
                  pl.BlockSpec((tm, width), lambda i, r: (i * nr + r, 0)),
                  pl.BlockSpec((tm, d), lambda i, r: (i * nr + r, Z_GATE * width // d)),
                  pl.BlockSpec((tm, d), lambda i, r: (i * nr + r, Z_GATE * width // d + 1)),
                  pl.BlockSpec((1, tm, d), lambda i, r: (i, r, 0)),
                  pl.BlockSpec((1, 6, 1, d), lambda i, r: (i, 0, 0, 0)),
                  resident(wa), resident(wb), resident(wo),
                  pl.BlockSpec((1, d), lambda i, r: (0, 0))],
        out_specs=[pl.BlockSpec((1, tm, d), lambda i, r: (i, r, 0)),
                   pl.BlockSpec((1, tm, d), lambda i, r: (i, r, 0))],
        out_shape=[jax.ShapeDtypeStruct((b, s, d), F32), jax.ShapeDtypeStruct((b, s, d), BF16)],
        compiler_params=pltpu.CompilerParams(dimension_semantics=("arbitrary", "arbitrary"),
                                             vmem_limit_bytes=VMEM_LIMIT),
        name="merge",
    )(ya, yb, z, z, x, mod6, wa, wb, wo, g)


def _ffn_kernel(h_ref, wa_ref, wu_ref, wo_ref, x1_ref, mod_ref, g_ref, o_ref, acc_ref, *, nt):
    t = pl.program_id(2)

    @pl.when(t == 0)
    def _():
        acc_ref[...] = jnp.zeros_like(acc_ref)

    @pl.when(t < nt)
    def _():
        h = h_ref[0]
        a = _dot(h, wa_ref[...])
        up = _dot(h, wu_ref[...])
        ha = 0.5 * a
        acc_ref[...] += _dot(((ha * jnp.tanh(ha) + ha) * up).astype(BF16), wo_ref[...])

    @pl.when(t >= nt)
    def _():
        half = o_ref.shape[1]
        base = (t - nt) * half
        gate = mod_ref[0, 5]
        g = g_ref[...]

        def body(i, carry):
            rows = pl.ds(pl.multiple_of(i * ROW_GROUP, ROW_GROUP), ROW_GROUP)
            arows = pl.ds(pl.multiple_of(base + i * ROW_GROUP, ROW_GROUP), ROW_GROUP)
            o_ref[0, rows, :] = _rms(x1_ref[0, rows, :] + gate * acc_ref[arows, :]) * g
            return carry

        lax.fori_loop(0, half // ROW_GROUP, body, 0, unroll=8)


def _ffn(h2, w_in, w_out, x1, mod6, g, tm=1024, th=512):
    b, s, d = x1.shape
    hidden = w_out.shape[0]
    nt = hidden // th
    half = tm // 2
    nr = s // tm
    wt = lambda t: jnp.where(t < nt, t, 0)
    fin = lambda r, t: 2 * r + jnp.clip(t - nt, 0, 1)

    def h2_blk(i, r, t):
        lin = jnp.minimum(i * nr + r + (t >= nt).astype(jnp.int32), b * nr - 1)
        return (lin // nr, lin % nr, 0)

    return pl.pallas_call(
        functools.partial(_ffn_kernel, nt=nt),
        grid=(b, s // tm, nt + 2),
        in_specs=[pl.BlockSpec((1, tm, d), h2_blk),
                  pl.BlockSpec((d, th), lambda i, r, t: (0, wt(t))),
                  pl.BlockSpec((d, th), lambda i, r, t: (0, nt + wt(t))),
                  pl.BlockSpec((th, d), lambda i, r, t: (wt(t), 0)),
                  pl.BlockSpec((1, half, d), lambda i, r, t: (i, fin(r, t), 0)),
                  pl.BlockSpec((1, 6, 1, d), lambda i, r, t: (i, 0, 0, 0)),
                  pl.BlockSpec((1, d), lambda i, r, t: (0, 0))],
        out_specs=pl.BlockSpec((1, half, d), lambda i, r, t: (i, fin(r, t), 0)),
        out_shape=jax.ShapeDtypeStruct((b, s, d), F32),
        scratch_shapes=[pltpu.VMEM((tm, d), F32)],
        compiler_params=pltpu.CompilerParams(dimension_semantics=("arbitrary", "arbitrary", "arbitrary"),
                                             vmem_limit_bytes=VMEM_LIMIT),
        name="ffn",
    )(h2, w_in, w_in, w_out, x1, mod6, g)


def kernel(x, c, w_ada, b_ada, norm1_g, w_in, b_gate, gmlp_ln_g, gmlp_ln_b, gmlp_ws, gmlp_bs, hg_lb, hg_norm_g,
           w_branch_gmlp, w_branch_hg, w_out, norm2_g, w_ffn_in, w_ffn_out, final_norm_g):
    batch, seq, d = x.shape
    depth = w_ada.shape[0]
    width = w_branch_gmlp.shape[1]
    assert depth == 1 and width == 8 * GROUP and gmlp_ws.shape[2] == CHUNK and seq % 256 == 0
    assert w_in.shape[2] == 6 * width + 2 * d and hg_lb.shape[0] == depth + 1
    m = batch * seq
    tn = 1024
    assert width == tn and d == 2 * tn

    c_pad = jnp.zeros((8, d), F32).at[:batch].set(c)
    mod = _ada(c_pad, w_ada, b_ada)
    mod6 = mod[:batch].reshape(batch, 6, 1, d)

    h1 = _prenorm(x, mod6, norm1_g).reshape(m, d)
    z, lf = _inproj(h1, w_in, b_gate, hg_lb)

    ya, yb, w_a, w_b, w_o, w_fi, w_fo = _mixer(
        z, lf, gmlp_ln_g, gmlp_ln_b, gmlp_ws[0], gmlp_bs[0].T, hg_norm_g, batch,
        casts=(w_branch_gmlp, w_branch_hg, w_out, w_ffn_in, w_ffn_out))

    x1, h2 = _merge(ya, yb, z, x, mod6, w_a, w_b, w_o, norm2_g)
    return _ffn(h2, w_fi, w_fo, x1, mod6, final_norm_g.reshape(1, d))
```

```python
import functools

import numpy as np
import jax
import jax.numpy as jnp
from jax import lax
from jax.experimental import pallas as pl
from jax.experimental.pallas import tpu as pltpu

F32 = jnp.float32
BF16 = jnp.bfloat16
EPS = 1e-6
LOG2E = 1.4426950408889634
GELU_C1 = 0.7978845608028654
GELU_C3 = 0.044715 * GELU_C1

GROUP = 128
CHUNK = 128
MERGE_PARTS = 2
ROW_GROUP = 16
HEADS_PER_PASS = 4
LEVELS = (64, 32, 16, 8, 4, 2)
VMEM_LIMIT = 56 * 1024 * 1024


def _dot(a, b):
    return jnp.dot(a, b, preferred_element_type=F32)


def _dot_nt(a, b):
    return lax.dot_general(a, b, (((1,), (1,)), ((), ())), preferred_element_type=F32)


def _dot_tn(a, b):
    return lax.dot_general(a, b, (((0,), (0,)), ((), ())), preferred_element_type=F32)


def _rms(x):
    return x * lax.rsqrt(jnp.mean(x * x, axis=-1, keepdims=True) + EPS)


def _split_bf16(x, parts):
    out = []
    for _ in range(parts - 1):
        p = x.astype(BF16)
        out.append(p)
        x = x - p.astype(F32)
    out.append(x.astype(BF16))
    return out


def _ada_kernel(c_ref, w_ref, b_ref, o_ref):
    rows = c_ref.shape[0]
    ca = jnp.concatenate(_split_bf16(jax.nn.silu(c_ref[...]), 2), axis=0)
    acc = _dot(ca, w_ref[...].astype(BF16))
    o_ref[...] = acc[:rows] + acc[rows:] + b_ref[...]


def _ada(c_pad, w_ada, b_ada, tn=512):
    _, d, n = w_ada.shape
    return pl.pallas_call(
        _ada_kernel,
        grid=(n // tn,),
        in_specs=[pl.BlockSpec((8, d), lambda j: (0, 0)),
                  pl.BlockSpec((None, d, tn), lambda j: (0, 0, j)),
                  pl.BlockSpec((1, tn), lambda j: (0, j))],
        out_specs=pl.BlockSpec((8, tn), lambda j: (0, j)),
        out_shape=jax.ShapeDtypeStruct((8, n), F32),
        compiler_params=pltpu.CompilerParams(dimension_semantics=("arbitrary",), vmem_limit_bytes=VMEM_LIMIT),
        name="ada",
    )(c_pad, w_ada, b_ada)


def _prenorm_kernel(x_ref, mod_ref, g_ref, h_ref):
    scale = g_ref[...] * (1.0 + mod_ref[0, 1])
    shift = mod_ref[0, 0]

    def body(i, carry):
        rows = pl.ds(pl.multiple_of(i * ROW_GROUP, ROW_GROUP), ROW_GROUP)
        h_ref[0, rows, :] = (_rms(x_ref[0, rows, :]) * scale + shift).astype(BF16)
        return carry

    lax.fori_loop(0, x_ref.shape[1] // ROW_GROUP, body, 0, unroll=8)


def _prenorm(x, mod6, g, tm=1024):
    b, s, d = x.shape
    return pl.pallas_call(
        _prenorm_kernel,
        grid=(b, s // tm),
        in_specs=[pl.BlockSpec((1, tm, d), lambda i, r: (i, r, 0)),
                  pl.BlockSpec((1, 6, 1, d), lambda i, r: (i, 0, 0, 0)),
                  pl.BlockSpec((1, d), lambda i, r: (0, 0))],
        out_specs=pl.BlockSpec((1, tm, d), lambda i, r: (i, r, 0)),
        out_shape=jax.ShapeDtypeStruct((b, s, d), BF16),
        compiler_params=pltpu.CompilerParams(dimension_semantics=("arbitrary", "arbitrary"),
                                             vmem_limit_bytes=VMEM_LIMIT),
        name="prenorm",
    )(x, mod6, g)


IN_TILES = ((0, "gelu"), (1, "gelu"), (2, "silu"), (5, "silu"), (6, "gate"), (7, "gate"), (8, "gate"), (9, "gate"),
            (4, "none"), (3, "logf"))
Z_U, Z_V, Z_Q, Z_OG, Z_GATE, Z_IV = 0, 1, 2, 3, 4, 8


def _inproj_kernel(h_ref, w_ref, bg_ref, lb_ref, z_ref, lf_ref, wb_ref):
    j = pl.program_id(0)

    @pl.when(pl.program_id(1) == 0)
    def _():
        wb_ref[...] = w_ref[...].astype(BF16)

    def tile(act):
        acc = _dot(h_ref[...], wb_ref[...])
        if act == "gelu":
            hx = 0.5 * acc
            t = acc * (GELU_C1 + GELU_C3 * (acc * acc))
            z_ref[...] = (hx * jnp.tanh(t) + hx).astype(BF16)
        elif act == "silu":
            hx = 0.5 * acc
            z_ref[...] = (hx * jnp.tanh(hx) + hx).astype(BF16)
        elif act == "gate":
            z_ref[...] = (0.5 * jnp.tanh(0.5 * (acc + bg_ref[...])) + 0.5).astype(BF16)
        elif act == "none":
            z_ref[...] = acc.astype(BF16)
        else:
            lb = jax.nn.softmax(lb_ref[...], axis=0)[0:1, :]
            lf_ref[...] = jnp.log((0.5 + 0.5 * lb) + (0.5 - 0.5 * lb) * jnp.tanh(0.5 * acc))

    acts = [a for _, a in IN_TILES]
    for act in dict.fromkeys(acts):
        first = acts.index(act)
        last = len(acts) - 1 - acts[::-1].index(act)
        pl.when((j >= first) & (j <= last))(functools.partial(tile, act))


def _inproj(h, w, b_gate, hg_lb, tm=1024, tn=1024):
    m, d = h.shape
    ni, nj = m // tm, len(IN_TILES)
    acts = [a for _, a in IN_TILES]
    assert acts[-1] == "logf" and acts.count("logf") == 1
    gate0 = acts.index("gate")
    ngate = acts.count("gate")

    def wcol(j):
        col = j
        for k, (c, _) in enumerate(IN_TILES):
            col = jnp.where(j == k, c, col)
        return col

    last = nj - 1
    return pl.pallas_call(
        _inproj_kernel,
        grid=(nj, ni),
        in_specs=[pl.BlockSpec((tm, d), lambda j, i: (i, 0)),
                  pl.BlockSpec((None, d, tn), lambda j, i: (0, 0, wcol(j))),
                  pl.BlockSpec((1, tn), lambda j, i: (0, jnp.clip(j - gate0, 0, ngate - 1))),
                  pl.BlockSpec(hg_lb.shape, lambda j, i: (0, 0))],
        out_specs=[pl.BlockSpec((tm, tn), lambda j, i: (jnp.where(j < last, i, ni - 1), jnp.minimum(j, last - 1))),
                   pl.BlockSpec((tm, tn), lambda j, i: (jnp.where(j < last, 0, i), 0))],
        out_shape=[jax.ShapeDtypeStruct((m, (nj - 1) * tn), BF16), jax.ShapeDtypeStruct((m, tn), F32)],
        scratch_shapes=[pltpu.VMEM((d, tn), BF16)],
        compiler_params=pltpu.CompilerParams(dimension_semantics=("arbitrary", "arbitrary"),
                                             vmem_limit_bytes=VMEM_LIMIT),
        name="inproj",
    )(h, w, b_gate, hg_lb)


def _level_masks():
    t = np.arange(CHUNK)[:, None]
    s = np.arange(CHUNK)[None, :]
    out = [(t // (2 * m) == s // (2 * m)) & ((t // m) % 2 == 1) & ((s // m) % 2 == 0) for m in LEVELS]
    return np.stack(out).astype(np.float32)


def _decay_matrix():
    t = np.arange(CHUNK)[:, None]
    j = np.arange(CHUNK)[None, :]
    blocks = [j <= t]
    for m in LEVELS:
        r = (t // (2 * m)) * (2 * m) + m - 1
        blocks.append(np.where((t // m) % 2 == 1, (j > r) & (j <= t), (j > t) & (j <= r)))
    blocks.append(j > t)
    w = np.concatenate(blocks, axis=0).astype(np.float32)
    return np.concatenate([w, w], axis=1)


def _mixer_kernel(u_ref, v_ref, q_ref, og_ref, lf_ref, iv_ref, lng_ref, lnb_ref, ws_ref, bst_ref, ng_ref,
                  cw_ref, msk_ref, *rest, n_cast):
    src, (ya_ref, yb_ref), dst, st_ref = rest[:n_cast], rest[n_cast:n_cast + 2], rest[n_cast + 2:-1], rest[-1]
    rows = u_ref.shape[0]
    nchunk = rows // CHUNK
    ngroup = u_ref.shape[1] // GROUP

    @pl.when(pl.program_id(1) == 0)
    def _():
        st_ref[...] = jnp.zeros_like(st_ref)

    for s_ref, d_ref in zip(src, dst):
        d_ref[...] = s_ref[...].astype(BF16)

    v = v_ref[...].astype(F32)
    mu = jnp.mean(v, axis=-1, keepdims=True)
    vc = v - mu
    var = jnp.mean(vc * vc, axis=-1, keepdims=True)
    vn = (vc * lax.rsqrt(var + EPS) * lng_ref[...] + lnb_ref[...]).astype(BF16)
    tri = lax.broadcasted_iota(jnp.int32, (CHUNK, CHUNK), 0) >= lax.broadcasted_iota(jnp.int32, (CHUNK, CHUNK), 1)
    for g in range(ngroup):
        cols = slice(g * GROUP, (g + 1) * GROUP)
        w = jnp.where(tri, ws_ref[g], 0.0).astype(BF16)
        rhs = jnp.concatenate([vn[c * CHUNK:(c + 1) * CHUNK, cols] for c in range(nchunk)], axis=1)
        sg = _dot(w, rhs) + bst_ref[:, g:g + 1]
        for c in range(nchunk):
            r = slice(c * CHUNK, (c + 1) * CHUNK)
            ya_ref[r, cols] = (u_ref[r, cols].astype(F32) * sg[:, c * CHUNK:(c + 1) * CHUNK]).astype(BF16)

    nlev = len(LEVELS)
    odd = lax.broadcasted_iota(jnp.int32, (CHUNK, GROUP), 0) % 2 == 1
    for c in range(nchunk):
        r = slice(c * CHUNK, (c + 1) * CHUNK)
        lf2 = lf_ref[r, :] * LOG2E
        ex = _dot(cw_ref[...], jnp.concatenate(_split_bf16(lf2, 2), axis=0))
        for h0 in range(0, ngroup, HEADS_PER_PASS):
            hs = range(h0, h0 + HEADS_PER_PASS)
            cs = {h: slice(h * GROUP, (h + 1) * GROUP) for h in hs}
            e2 = lambda h, i: jnp.exp2(ex[i * CHUNK:(i + 1) * CHUNK, cs[h]]).astype(BF16)
            qb = {h: q_ref[r, cs[h]] for h in hs}
            kb = {h: (1.0 - jnp.exp2(lf2[:, cs[h]])).astype(BF16) for h in hs}
            st = {h: st_ref[h] for h in hs}
            o = {h: _dot_nt(qb[h] * e2(h, 0), st[h].astype(BF16)) for h in hs}
            a = {}
            for li in range(nlev):
                for h in hs:
                    e = e2(h, 1 + li)
                    p = msk_ref[li] * _dot_nt(qb[h] * e, kb[h] * e)
                    a[h] = p if li == 0 else a[h] + p
            for h in hs:
                iv = iv_ref[r, cs[h]]
                ivf = iv.astype(F32)
                qf = qb[h].astype(F32)
                fh = jnp.exp2(lf2[:, cs[h]])
                kf = 1.0 - fh
                c0 = jnp.sum(qf * kf, axis=-1, keepdims=True)
                c1 = jnp.sum(jnp.where(odd, qf * fh * pltpu.roll(kf, 1, 0), 0.0), axis=-1, keepdims=True)
                oh = o[h] + _dot(a[h].astype(BF16), iv) + c0 * ivf + c1 * pltpu.roll(ivf, 1, 0)
                b_last = ex[CHUNK - 1:CHUNK, cs[h]]
                st_ref[h] = st[h] * jnp.exp2(b_last) + _dot_tn(iv, kb[h] * e2(h, nlev + 1))
                y = _rms(oh) * ng_ref[:, cs[h]] * og_ref[r, cs[h]].astype(F32)
                yb_ref[r, cs[h]] = y.astype(BF16)


def _mixer(z, lf, ln_g, ln_b, ws, bs_t, norm_g, batch, casts, rows=256):
    m, width = lf.shape
    nr = m // batch // rows
    nsteps = batch * nr
    ngroup = width // GROUP
    cw = jnp.asarray(_decay_matrix(), BF16)
    msk = jnp.asarray(_level_masks(), F32)
    row_blk = lambda col: pl.BlockSpec((rows, width), lambda b, r: (b * nr + r, col))
    full = lambda a: pl.BlockSpec(a.shape, lambda b, r: (0,) * a.ndim)
    in_specs = [row_blk(Z_U), row_blk(Z_V), row_blk(Z_Q), row_blk(Z_OG), row_blk(0), row_blk(Z_IV),
                full(ln_g), full(ln_b), full(ws), full(bs_t), full(norm_g), full(cw), full(msk)]
    out_specs = [row_blk(0), row_blk(0)]
    out_shape = [jax.ShapeDtypeStruct((m, width), BF16), jax.ShapeDtypeStruct((m, width), BF16)]
    for a in casts:
        _, n, w = a.shape
        slab = n // nsteps
        assert slab * nsteps == n and slab % 16 == 0
        in_specs.append(pl.BlockSpec((None, slab, w), lambda b, r: (0, b * nr + r, 0)))
        out_specs.append(pl.BlockSpec((slab, w), lambda b, r: (b * nr + r, 0)))
        out_shape.append(jax.ShapeDtypeStruct((n, w), BF16))
    return pl.pallas_call(
        functools.partial(_mixer_kernel, n_cast=len(casts)),
        grid=(batch, nr),
        in_specs=in_specs,
        out_specs=out_specs,
        out_shape=out_shape,
        scratch_shapes=[pltpu.VMEM((ngroup, GROUP, GROUP), F32)],
        compiler_params=pltpu.CompilerParams(dimension_semantics=("arbitrary", "arbitrary"),
                                             vmem_limit_bytes=VMEM_LIMIT),
        name="mixer",
    )(z, z, z, z, lf, z, ln_g, ln_b, ws, bs_t, norm_g, cw, msk, *casts)


def _merge_kernel(ya_ref, yb_ref, ga_ref, gb_ref, x_ref, mod_ref, wa_ref, wb_ref, wo_ref, g_ref, x1_ref, h2_ref):
    tm = ya_ref.shape[0]
    scale = g_ref[...] * (1.0 + mod_ref[0, 4])
    for p in range(MERGE_PARTS):
        r = slice(p * tm // MERGE_PARTS, (p + 1) * tm // MERGE_PARTS)
        pa = _dot(ya_ref[r, :], wa_ref[...])
        pb = _dot(yb_ref[r, :], wb_ref[...])
        y = (ga_ref[r, :].astype(F32) * pa + gb_ref[r, :].astype(F32) * pb).astype(BF16)
        x1 = x_ref[0, r, :] + mod_ref[0, 2] * _dot(y, wo_ref[...])
        x1_ref[0, r, :] = x1
        h2_ref[0, r, :] = (_rms(x1) * scale + mod_ref[0, 3]).astype(BF16)


def _merge(ya, yb, z, x, mod6, wa, wb, wo, g, tm=512):
    b, s, d = x.shape
    nr = s // tm
    width = ya.shape[1]
    resident = lambda a: pl.BlockSpec(a.shape, lambda i, r: (0,) * a.ndim, pipeline_mode=pl.Buffered(1))
    return pl.pallas_call(
        _merge_kernel,
        grid=(b, nr),
        in_specs=[pl.BlockSpec((tm, width), lambda i, r: (i * nr + r, 0)),
                  pl.BlockSpec((tm, width), lambda i, r: (i * nr + r, 0)),
                  pl.BlockSpec((tm, d), lambda i, r: (i * nr + r, Z_GATE * width // d)),
                  pl.BlockSpec((tm, d), lambda i, r: (i * nr + r, Z_GATE * width // d + 1)),
                  pl.BlockSpec((1, tm, d), lambda i, r: (i, r, 0)),
                  pl.BlockSpec((1, 6, 1, d), lambda i, r: (i, 0, 0, 0)),
                  resident(wa), resident(wb), resident(wo),
                  pl.BlockSpec((1, d), lambda i, r: (0, 0))],
        out_specs=[pl.BlockSpec((1, tm, d), lambda i, r: (i, r, 0)),
                   pl.BlockSpec((1, tm, d), lambda i, r: (i, r, 0))],
        out_shape=[jax.ShapeDtypeStruct((b, s, d), F32), jax.ShapeDtypeStruct((b, s, d), BF16)],
        compiler_params=pltpu.CompilerParams(dimension_semantics=("arbitrary", "arbitrary"),
                                             vmem_limit_bytes=VMEM_LIMIT),
        name="merge",
    )(ya, yb, z, z, x, mod6, wa, wb, wo, g)


def _ffn_kernel(h_ref, wa_ref, wu_ref, wo_ref, x1_ref, mod_ref, g_ref, o_ref, acc_ref, *, nt):
    t = pl.program_id(2)

    @pl.when(t == 0)
    def _():
        acc_ref[...] = jnp.zeros_like(acc_ref)

    @pl.when(t < nt)
    def _():
        h = h_ref[0]
        a = _dot(h, wa_ref[...])
        up = _dot(h, wu_ref[...])
        ha = 0.5 * a
        acc_ref[...] += _dot(((ha * jnp.tanh(ha) + ha) * up).astype(BF16), wo_ref[...])

    @pl.when(t >= nt)
    def _():
        half = o_ref.shape[1]
        base = (t - nt) * half
        gate = mod_ref[0, 5]
        g = g_ref[...]

        def body(i, carry):
            rows = pl.ds(pl.multiple_of(i * ROW_GROUP, ROW_GROUP), ROW_GROUP)
            arows = pl.ds(pl.multiple_of(base + i * ROW_GROUP, ROW_GROUP), ROW_GROUP)
            o_ref[0, rows, :] = _rms(x1_ref[0, rows, :] + gate * acc_ref[arows, :]) * g
            return carry

        lax.fori_loop(0, half // ROW_GROUP, body, 0, unroll=8)


def _ffn(h2, w_in, w_out, x1, mod6, g, tm=1024, th=512):
    b, s, d = x1.shape
    hidden = w_out.shape[0]
    nt = hidden // th
    half = tm // 2
    nr = s // tm
    wt = lambda t: jnp.where(t < nt, t, 0)
    fin = lambda r, t: 2 * r + jnp.clip(t - nt, 0, 1)

    def h2_blk(i, r, t):
        lin = jnp.minimum(i * nr + r + (t >= nt).astype(jnp.int32), b * nr - 1)
        return (lin // nr, lin % nr, 0)

    return pl.pallas_call(
        functools.partial(_ffn_kernel, nt=nt),
        grid=(b, s // tm, nt + 2),
        in_specs=[pl.BlockSpec((1, tm, d), h2_blk),
                  pl.BlockSpec((d, th), lambda i, r, t: (0, wt(t))),
                  pl.BlockSpec((d, th), lambda i, r, t: (0, nt + wt(t))),
                  pl.BlockSpec((th, d), lambda i, r, t: (wt(t), 0)),
                  pl.BlockSpec((1, half, d), lambda i, r, t: (i, fin(r, t), 0)),
                  pl.BlockSpec((1, 6, 1, d), lambda i, r, t: (i, 0, 0, 0)),
                  pl.BlockSpec((1, d), lambda i, r, t: (0, 0))],
        out_specs=pl.BlockSpec((1, half, d), lambda i, r, t: (i, fin(r, t), 0)),
        out_shape=jax.ShapeDtypeStruct((b, s, d), F32),
        scratch_shapes=[pltpu.VMEM((tm, d), F32)],
        compiler_params=pltpu.CompilerParams(dimension_semantics=("arbitrary", "arbitrary", "arbitrary"),
                                             vmem_limit_bytes=VMEM_LIMIT),
        name="ffn",
    )(h2, w_in, w_in, w_out, x1, mod6, g)


def kernel(x, c, w_ada, b_ada, norm1_g, w_in, b_gate, gmlp_ln_g, gmlp_ln_b, gmlp_ws, gmlp_bs, hg_lb, hg_norm_g,
           w_branch_gmlp, w_branch_hg, w_out, norm2_g, w_ffn_in, w_ffn_out, final_norm_g):
    batch, seq, d = x.shape
    depth = w_ada.shape[0]
    width = w_branch_gmlp.shape[1]
    assert depth == 1 and width == 8 * GROUP and gmlp_ws.shape[2] == CHUNK and seq % 256 == 0
    assert w_in.shape[2] == 6 * width + 2 * d and hg_lb.shape[0] == depth + 1
    m = batch * seq
    tn = 1024
    assert width == tn and d == 2 * tn

    c_pad = jnp.zeros((8, d), F32).at[:batch].set(c)
    mod = _ada(c_pad, w_ada, b_ada)
    mod6 = mod[:batch].reshape(batch, 6, 1, d)

    h1 = _prenorm(x, mod6, norm1_g).reshape(m, d)
    z, lf = _inproj(h1, w_in, b_gate, hg_lb)

    ya, yb, w_a, w_b, w_o, w_fi, w_fo = _mixer(
        z, lf, gmlp_ln_g, gmlp_ln_b, gmlp_ws[0], gmlp_bs[0].T, hg_norm_g, batch,
        casts=(w_branch_gmlp, w_branch_hg, w_out, w_ffn_in, w_ffn_out))

    x1, h2 = _merge(ya, yb, z, x, mod6, w_a, w_b, w_o, norm2_g)
    return _ffn(h2, w_fi, w_fo, x1, mod6, final_norm_g.reshape(1, d))
```

```python
import functools

import numpy as np
import jax
import jax.numpy as jnp
from jax import lax
from jax.experimental import pallas as pl
from jax.experimental.pallas import tpu as pltpu

F32 = jnp.float32
BF16 = jnp.bfloat16
EPS = 1e-6
LOG2E = 1.4426950408889634
GELU_C1 = 0.7978845608028654
GELU_C3 = 0.044715 * GELU_C1

GROUP = 128
CHUNK = 128
MERGE_PARTS = 2
ROW_GROUP = 16
HEADS_PER_PASS = 4
LEVELS = (64, 32, 16, 8, 4, 2)
VMEM_LIMIT = 56 * 1024 * 1024


def _dot(a, b):
    return jnp.dot(a, b, preferred_element_type=F32)


def _dot_nt(a, b):
    return lax.dot_general(a, b, (((1,), (1,)), ((), ())), preferred_element_type=F32)


def _dot_tn(a, b):
    return lax.dot_general(a, b, (((0,), (0,)), ((), ())), preferred_element_type=F32)


def _rms(x):
    return x * lax.rsqrt(jnp.mean(x * x, axis=-1, keepdims=True) + EPS)


def _split_bf16(x, parts):
    out = []
    for _ in range(parts - 1):
        p = x.astype(BF16)
        out.append(p)
        x = x - p.astype(F32)
    out.append(x.astype(BF16))
    return out


def _ada_kernel(c_ref, w_ref, b_ref, o_ref):
    rows = c_ref.shape[0]
    ca = jnp.concatenate(_split_bf16(jax.nn.silu(c_ref[...]), 2), axis=0)
    acc = _dot(ca, w_ref[...].astype(BF16))
    o_ref[...] = acc[:rows] + acc[rows:] + b_ref[...]


def _ada(c_pad, w_ada, b_ada, tn=1024):
    _, d, n = w_ada.shape
    return pl.pallas_call(
        _ada_kernel,
        grid=(n // tn,),
        in_specs=[pl.BlockSpec((8, d), lambda j: (0, 0)),
                  pl.BlockSpec((None, d, tn), lambda j: (0, 0, j)),
                  pl.BlockSpec((1, tn), lambda j: (0, j))],
        out_specs=pl.BlockSpec((8, tn), lambda j: (0, j)),
        out_shape=jax.ShapeDtypeStruct((8, n), F32),
        compiler_params=pltpu.CompilerParams(dimension_semantics=("arbitrary",), vmem_limit_bytes=VMEM_LIMIT),
        name="ada",
    )(c_pad, w_ada, b_ada)


def _gelu(x):
    hx = 0.5 * x
    return hx * jnp.tanh(x * (GELU_C1 + GELU_C3 * (x * x))) + hx


def _first_kernel(x_ref, mod_ref, g_ref, w_ref, zg_ref, h_ref, wb_ref, *, ncol):
    j = pl.program_id(2)
    first = (pl.program_id(0) == 0) & (pl.program_id(1) == 0)
    tn = zg_ref.shape[1]

    def tile(jj):
        if jj == 0:
            scale = g_ref[...] * (1.0 + mod_ref[0, 1])
            h_ref[...] = (_rms(x_ref[0]) * scale + mod_ref[0, 0]).astype(BF16)
        zg_ref[...] = _gelu(_dot(h_ref[...], wb_ref[jj])).astype(BF16)

    for jj in range(ncol):
        @pl.when(first & (j == jj))
        def _(jj=jj):
            wb_ref[jj] = w_ref[:, jj * tn:(jj + 1) * tn].astype(BF16)

        pl.when(j == jj)(functools.partial(tile, jj))


def _first(x, mod6, g, w, ncol=2, tm=512, tn=1024):
    b, s, d = x.shape
    nr = s // tm
    return pl.pallas_call(
        functools.partial(_first_kernel, ncol=ncol),
        grid=(b, nr, ncol),
        in_specs=[pl.BlockSpec((1, tm, d), lambda i, r, j: (i, r, 0)),
                  pl.BlockSpec((1, 6, 1, d), lambda i, r, j: (i, 0, 0, 0)),
                  pl.BlockSpec((1, d), lambda i, r, j: (0, 0)),
                  pl.BlockSpec((None, d, ncol * tn), lambda i, r, j: (0, 0, 0), pipeline_mode=pl.Buffered(1))],
        out_specs=[pl.BlockSpec((tm, tn), lambda i, r, j: (i * nr + r, j)),
                   pl.BlockSpec((tm, d), lambda i, r, j: (i * nr + r, 0))],
        out_shape=[jax.ShapeDtypeStruct((b * s, ncol * tn), BF16), jax.ShapeDtypeStruct((b * s, d), BF16)],
        scratch_shapes=[pltpu.VMEM((ncol, d, tn), BF16)],
        compiler_params=pltpu.CompilerParams(dimension_semantics=("arbitrary", "arbitrary", "arbitrary"),
                                             vmem_limit_bytes=VMEM_LIMIT),
        name="first",
    )(x, mod6, g, w)


IN_TILES = ((2, "silu"), (5, "silu"), (6, "gate"), (7, "gate"), (8, "gate"), (9, "gate"), (4, "none"), (3, "logf"))
Z_Q, Z_OG, Z_GATE, Z_IV = 0, 1, 2, 6


def _inproj_kernel(h_ref, w_ref, bg_ref, lb_ref, z_ref, lf_ref, wb_ref):
    j = pl.program_id(0)

    @pl.when(pl.program_id(1) == 0)
    def _():
        wb_ref[...] = w_ref[...].astype(BF16)

    def tile(act):
        acc = _dot(h_ref[...], wb_ref[...])
        if act == "silu":
            hx = 0.5 * acc
            z_ref[...] = (hx * jnp.tanh(hx) + hx).astype(BF16)
        elif act == "gate":
            z_ref[...] = (0.5 * jnp.tanh(0.5 * (acc + bg_ref[...])) + 0.5).astype(BF16)
        elif act == "none":
            z_ref[...] = acc.astype(BF16)
        else:
            lb = jax.nn.softmax(lb_ref[...], axis=0)[0:1, :]
            lf_ref[...] = jnp.log((0.5 + 0.5 * lb) + (0.5 - 0.5 * lb) * jnp.tanh(0.5 * acc))

    acts = [a for _, a in IN_TILES]
    for act in dict.fromkeys(acts):
        first = acts.index(act)
        last = len(acts) - 1 - acts[::-1].index(act)
        pl.when((j >= first) & (j <= last))(functools.partial(tile, act))


def _inproj(h, w, b_gate, hg_lb, tm=1024, tn=1024):
    m, d = h.shape
    ni, nj = m // tm, len(IN_TILES)
    acts = [a for _, a in IN_TILES]
    assert acts[-1] == "logf" and acts.count("logf") == 1
    gate0 = acts.index("gate")
    ngate = acts.count("gate")

    def wcol(j):
        col = j
        for k, (c, _) in enumerate(IN_TILES):
            col = jnp.where(j == k, c, col)
        return col

    last = nj - 1
    return pl.pallas_call(
        _inproj_kernel,
        grid=(nj, ni),
        in_specs=[pl.BlockSpec((tm, d), lambda j, i: (i, 0)),
                  pl.BlockSpec((None, d, tn), lambda j, i: (0, 0, wcol(j))),
                  pl.BlockSpec((1, tn), lambda j, i: (0, jnp.clip(j - gate0, 0, ngate - 1))),
                  pl.BlockSpec(hg_lb.shape, lambda j, i: (0, 0))],
        out_specs=[pl.BlockSpec((tm, tn), lambda j, i: (jnp.where(j < last, i, ni - 1), jnp.minimum(j, last - 1))),
                   pl.BlockSpec((tm, tn), lambda j, i: (jnp.where(j < last, 0, i), 0))],
        out_shape=[jax.ShapeDtypeStruct((m, (nj - 1) * tn), BF16), jax.ShapeDtypeStruct((m, tn), F32)],
        scratch_shapes=[pltpu.VMEM((d, tn), BF16)],
        compiler_params=pltpu.CompilerParams(dimension_semantics=("arbitrary", "arbitrary"),
                                             vmem_limit_bytes=VMEM_LIMIT),
        name="inproj",
    )(h, w, b_gate, hg_lb)


def _level_masks():
    t = np.arange(CHUNK)[:, None]
    s = np.arange(CHUNK)[None, :]
    out = [(t // (2 * m) == s // (2 * m)) & ((t // m) % 2 == 1) & ((s // m) % 2 == 0) for m in LEVELS]
    return np.stack(out).astype(np.float32)


def _decay_matrix():
    t = np.arange(CHUNK)[:, None]
    j = np.arange(CHUNK)[None, :]
    blocks = [j <= t]
    for m in LEVELS:
        r = (t // (2 * m)) * (2 * m) + m - 1
        blocks.append(np.where((t // m) % 2 == 1, (j > r) & (j <= t), (j > t) & (j <= r)))
    blocks.append(j > t)
    w = np.concatenate(blocks, axis=0).astype(np.float32)
    return np.concatenate([w, w], axis=1)


def _mixer_kernel(u_ref, v_ref, q_ref, og_ref, lf_ref, iv_ref, lng_ref, lnb_ref, ws_ref, bst_ref, ng_ref,
                  cw_ref, msk_ref, *rest, n_cast):
    src, (ya_ref, yb_ref), dst, st_ref = rest[:n_cast], rest[n_cast:n_cast + 2], rest[n_cast + 2:-1], rest[-1]
    rows = u_ref.shape[0]
    nchunk = rows // CHUNK
    ngroup = u_ref.shape[1] // GROUP

    @pl.when(pl.program_id(1) == 0)
    def _():
        st_ref[...] = jnp.zeros_like(st_ref)

    for s_ref, d_ref in zip(src, dst):
        d_ref[...] = s_ref[...].astype(BF16)

    v = v_ref[...].astype(F32)
    mu = jnp.mean(v, axis=-1, keepdims=True)
    vc = v - mu
    var = jnp.mean(vc * vc, axis=-1, keepdims=True)
    vn = (vc * lax.rsqrt(var + EPS) * lng_ref[...] + lnb_ref[...]).astype(BF16)
    tri = lax.broadcasted_iota(jnp.int32, (CHUNK, CHUNK), 0) >= lax.broadcasted_iota(jnp.int32, (CHUNK, CHUNK), 1)
    for g in range(ngroup):
        cols = slice(g * GROUP, (g + 1) * GROUP)
        w = jnp.where(tri, ws_ref[g], 0.0).astype(BF16)
        rhs = jnp.concatenate([vn[c * CHUNK:(c + 1) * CHUNK, cols] for c in range(nchunk)], axis=1)
        sg = _dot(w, rhs) + bst_ref[:, g:g + 1]
        for c in range(nchunk):
            r = slice(c * CHUNK, (c + 1) * CHUNK)
            ya_ref[r, cols] = (u_ref[r, cols].astype(F32) * sg[:, c * CHUNK:(c + 1) * CHUNK]).astype(BF16)

    nlev = len(LEVELS)
    odd = lax.broadcasted_iota(jnp.int32, (CHUNK, GROUP), 0) % 2 == 1
    for c in range(nchunk):
        r = slice(c * CHUNK, (c + 1) * CHUNK)
        lf2 = lf_ref[r, :] * LOG2E
        ex = _dot(cw_ref[...], jnp.concatenate(_split_bf16(lf2, 2), axis=0))
        for h0 in range(0, ngroup, HEADS_PER_PASS):
            hs = range(h0, h0 + HEADS_PER_PASS)
            cs = {h: slice(h * GROUP, (h + 1) * GROUP) for h in hs}
            e2 = lambda h, i: jnp.exp2(ex[i * CHUNK:(i + 1) * CHUNK, cs[h]]).astype(BF16)
            qb = {h: q_ref[r, cs[h]] for h in hs}
            kb = {h: (1.0 - jnp.exp2(lf2[:, cs[h]])).astype(BF16) for h in hs}
            st = {h: st_ref[h] for h in hs}
            o = {h: _dot_nt(qb[h] * e2(h, 0), st[h].astype(BF16)) for h in hs}
            a = {}
            for li in range(nlev):
                for h in hs:
                    e = e2(h, 1 + li)
                    p = msk_ref[li] * _dot_nt(qb[h] * e, kb[h] * e)
                    a[h] = p if li == 0 else a[h] + p
            for h in hs:
                iv = iv_ref[r, cs[h]]
                ivf = iv.astype(F32)
                qf = qb[h].astype(F32)
                fh = jnp.exp2(lf2[:, cs[h]])
                kf = 1.0 - fh
                c0 = jnp.sum(qf * kf, axis=-1, keepdims=True)
                c1 = jnp.sum(jnp.where(odd, qf * fh * pltpu.roll(kf, 1, 0), 0.0), axis=-1, keepdims=True)
                oh = o[h] + _dot(a[h].astype(BF16), iv) + c0 * ivf + c1 * pltpu.roll(ivf, 1, 0)
                b_last = ex[CHUNK - 1:CHUNK, cs[h]]
                st_ref[h] = st[h] * jnp.exp2(b_last) + _dot_tn(iv, kb[h] * e2(h, nlev + 1))
                y = _rms(oh) * ng_ref[:, cs[h]] * og_ref[r, cs[h]].astype(F32)
                yb_ref[r, cs[h]] = y.astype(BF16)


def _mixer(zg, z, lf, ln_g, ln_b, ws, bs_t, norm_g, batch, casts, rows=256):
    m, width = lf.shape
    nr = m // batch // rows
    nsteps = batch * nr
    ngroup = width // GROUP
    cw = jnp.asarray(_decay_matrix(), BF16)
    msk = jnp.asarray(_level_masks(), F32)
    row_blk = lambda col: pl.BlockSpec((rows, width), lambda b, r: (b * nr + r, col))
    full = lambda a: pl.BlockSpec(a.shape, lambda b, r: (0,) * a.ndim)
    in_specs = [row_blk(0), row_blk(1), row_blk(Z_Q), row_blk(Z_OG), row_blk(0), row_blk(Z_IV),
                full(ln_g), full(ln_b), full(ws), full(bs_t), full(norm_g), full(cw), full(msk)]
    out_specs = [row_blk(0), row_blk(0)]
    out_shape = [jax.ShapeDtypeStruct((m, width), BF16), jax.ShapeDtypeStruct((m, width), BF16)]
    for a in casts:
        _, n, w = a.shape
        slab = n // nsteps
        assert slab * nsteps == n and slab % 16 == 0
        in_specs.append(pl.BlockSpec((None, slab, w), lambda b, r: (0, b * nr + r, 0)))
        out_specs.append(pl.BlockSpec((slab, w), lambda b, r: (b * nr + r, 0)))
        out_shape.append(jax.ShapeDtypeStruct((n, w), BF16))
    return pl.pallas_call(
        functools.partial(_mixer_kernel, n_cast=len(casts)),
        grid=(batch, nr),
        in_specs=in_specs,
        out_specs=out_specs,
        out_shape=out_shape,
        scratch_shapes=[pltpu.VMEM((ngroup, GROUP, GROUP), F32)],
        compiler_params=pltpu.CompilerParams(dimension_semantics=("arbitrary", "arbitrary"),
                                             vmem_limit_bytes=VMEM_LIMIT),
        name="mixer",
    )(zg, zg, z, z, lf, z, ln_g, ln_b, ws, bs_t, norm_g, cw, msk, *casts)


def _merge_kernel(ya_ref, yb_ref, ga_ref, gb_ref, x_ref, mod_ref, wa_ref, wb_ref, wo_ref, g_ref, x1_ref, h2_ref):
    tm = ya_ref.shape[0]
    scale = g_ref[...] * (1.0 + mod_ref[0, 4])
    for p in range(MERGE_PARTS):
        r = slice(p * tm // MERGE_PARTS, (p + 1) * tm // MERGE_PARTS)
        pa = _dot(ya_ref[r, :], wa_ref[...])
        pb = _dot(yb_ref[r, :], wb_ref[...])
        y = (ga_ref[r, :].astype(F32) * pa + gb_ref[r, :].astype(F32) * pb).astype(BF16)
        x1 = x_ref[0, r, :] + mod_ref[0, 2] * _dot(y, wo_ref[...])
        x1_ref[0, r, :] = x1
        h2_ref[0, r, :] = (_rms(x1) * scale + mod_ref[0, 3]).astype(BF16)


def _merge(ya, yb, z, x, mod6, wa, wb, wo, g, tm=512):
    b, s, d = x.shape
    nr = s // tm
    width = ya.shape[1]
    resident = lambda a: pl.BlockSpec(a.shape, lambda i, r: (0,) * a.ndim, pipeline_mode=pl.Buffered(1))
    return pl.pallas_call(
        _merge_kernel,
        grid=(b, nr),
        in_specs=[pl.BlockSpec((tm, width), lambda i, r: (i * nr + r, 0)),
                  pl.BlockSpec((tm, width), lambda i, r: (i * nr + r, 0)),
                  pl.BlockSpec((tm, d), lambda i, r: (i * nr + r, Z_GATE * width // d)),
                  pl.BlockSpec((tm, d), lambda i, r: (i * nr + r, Z_GATE * width // d + 1)),
                  pl.BlockSpec((1, tm, d), lambda i, r: (i, r, 0)),
                  pl.BlockSpec((1, 6, 1, d), lambda i, r: (i, 0, 0, 0)),
                  resident(wa), resident(wb), resident(wo),
                  pl.BlockSpec((1, d), lambda i, r: (0, 0))],
        out_specs=[pl.BlockSpec((1, tm, d), lambda i, r: (i, r, 0)),
                   pl.BlockSpec((1, tm, d), lambda i, r: (i, r, 0))],
        out_shape=[jax.ShapeDtypeStruct((b, s, d), F32), jax.ShapeDtypeStruct((b, s, d), BF16)],
        compiler_params=pltpu.CompilerParams(dimension_semantics=("arbitrary", "arbitrary"),
                                             vmem_limit_bytes=VMEM_LIMIT),
        name="merge",
    )(ya, yb, z, z, x, mod6, wa, wb, wo, g)


def _ffn_kernel(h_ref, wa_ref, wu_ref, wo_ref, x1_ref, mod_ref, g_ref, o_ref, acc_ref, *, nt):
    t = pl.program_id(2)

    @pl.when(t == 0)
    def _():
        acc_ref[...] = jnp.zeros_like(acc_ref)

    @pl.when(t < nt)
    def _():
        h = h_ref[0]
        a = _dot(h, wa_ref[...])
        up = _dot(h, wu_ref[...])
        ha = 0.5 * a
        acc_ref[...] += _dot(((ha * jnp.tanh(ha) + ha) * up).astype(BF16), wo_ref[...])

    @pl.when(t >= nt)
    def _():
        half = o_ref.shape[1]
        base = (t - nt) * half
        gate = mod_ref[0, 5]
        g = g_ref[...]

        def body(i, carry):
            rows = pl.ds(pl.multiple_of(i * ROW_GROUP, ROW_GROUP), ROW_GROUP)
            arows = pl.ds(pl.multiple_of(base + i * ROW_GROUP, ROW_GROUP), ROW_GROUP)
            o_ref[0, rows, :] = _rms(x1_ref[0, rows, :] + gate * acc_ref[arows, :]) * g
            return carry

        lax.fori_loop(0, half // ROW_GROUP, body, 0, unroll=8)


def _ffn(h2, w_in, w_out, x1, mod6, g, tm=1024, th=512):
    b, s, d = x1.shape
    hidden = w_out.shape[0]
    nt = hidden // th
    half = tm // 2
    nr = s // tm
    wt = lambda t: jnp.where(t < nt, t, 0)
    fin = lambda r, t: 2 * r + jnp.clip(t - nt, 0, 1)

    def h2_blk(i, r, t):
        lin = jnp.minimum(i * nr + r + (t >= nt).astype(jnp.int32), b * nr - 1)
        return (lin // nr, lin % nr, 0)

    return pl.pallas_call(
        functools.partial(_ffn_kernel, nt=nt),
        grid=(b, s // tm, nt + 2),
        in_specs=[pl.BlockSpec((1, tm, d), h2_blk),
                  pl.BlockSpec((d, th), lambda i, r, t: (0, wt(t))),
                  pl.BlockSpec((d, th), lambda i, r, t: (0, nt + wt(t))),
                  pl.BlockSpec((th, d), lambda i, r, t: (wt(t), 0)),
                  pl.BlockSpec((1, half, d), lambda i, r, t: (i, fin(r, t), 0)),
                  pl.BlockSpec((1, 6, 1, d), lambda i, r, t: (i, 0, 0, 0)),
                  pl.BlockSpec((1, d), lambda i, r, t: (0, 0))],
        out_specs=pl.BlockSpec((1, half, d), lambda i, r, t: (i, fin(r, t), 0)),
        out_shape=jax.ShapeDtypeStruct((b, s, d), F32),
        scratch_shapes=[pltpu.VMEM((tm, d), F32)],
        compiler_params=pltpu.CompilerParams(dimension_semantics=("arbitrary", "arbitrary", "arbitrary"),
                                             vmem_limit_bytes=VMEM_LIMIT),
        name="ffn",
    )(h2, w_in, w_in, w_out, x1, mod6, g)


def kernel(x, c, w_ada, b_ada, norm1_g, w_in, b_gate, gmlp_ln_g, gmlp_ln_b, gmlp_ws, gmlp_bs, hg_lb, hg_norm_g,
           w_branch_gmlp, w_branch_hg, w_out, norm2_g, w_ffn_in, w_ffn_out, final_norm_g):
    batch, seq, d = x.shape
    depth = w_ada.shape[0]
    width = w_branch_gmlp.shape[1]
    assert depth == 1 and width == 8 * GROUP and gmlp_ws.shape[2] == CHUNK and seq % 256 == 0
    assert w_in.shape[2] == 6 * width + 2 * d and hg_lb.shape[0] == depth + 1
    tn = 1024
    assert width == tn and d == 2 * tn

    c_pad = jnp.zeros((8, d), F32).at[:batch].set(c)
    mod = _ada(c_pad, w_ada, b_ada)
    mod6 = mod[:batch].reshape(batch, 6, 1, d)

    zg, h1 = _first(x, mod6, norm1_g, w_in)
    z, lf = _inproj(h1, w_in, b_gate, hg_lb)

    ya, yb, w_a, w_b, w_o, w_fi, w_fo = _mixer(
        zg, z, lf, gmlp_ln_g, gmlp_ln_b, gmlp_ws[0], gmlp_bs[0].T, hg_norm_g, batch,
        casts=(w_branch_gmlp, w_branch_hg, w_out, w_ffn_in, w_ffn_out))

    x1, h2 = _merge(ya, yb, z, x, mod6, w_a, w_b, w_o, norm2_g)
    return _ffn(h2, w_fi, w_fo, x1, mod6, final_norm_g.reshape(1, d))
```

```python
import functools

import numpy as np
import jax
import jax.numpy as jnp
from jax import lax
from jax.experimental import pallas as pl
from jax.experimental.pallas import tpu as pltpu

F32 = jnp.float32
BF16 = jnp.bfloat16
EPS = 1e-6
LOG2E = 1.4426950408889634
GELU_C1 = 0.7978845608028654
GELU_C3 = 0.044715 * GELU_C1

SH1, SC1 = 0, 1
GT1, SH2, SC2, GT2 = 0, 1, 2, 3

GROUP = 128
CHUNK = 128
MERGE_PARTS = 2
ROW_GROUP = 16
HEADS_PER_PASS = 4
LEVELS = (64, 32, 16, 8, 4, 2)
VMEM_LIMIT = 56 * 1024 * 1024


def _dot(a, b):
    return jnp.dot(a, b, preferred_element_type=F32)


def _dot_nt(a, b):
    return lax.dot_general(a, b, (((1,), (1,)), ((), ())), preferred_element_type=F32)


def _dot_tn(a, b):
    return lax.dot_general(a, b, (((0,), (0,)), ((), ())), preferred_element_type=F32)


def _rms(x):
    return x * lax.rsqrt(jnp.mean(x * x, axis=-1, keepdims=True) + EPS)


def _split_bf16(x, parts):
    out = []
    for _ in range(parts - 1):
        p = x.astype(BF16)
        out.append(p)
        x = x - p.astype(F32)
    out.append(x.astype(BF16))
    return out


def _ada_tile(c_ref, w_ref, b_ref):
    rows = c_ref.shape[0]
    ca = jnp.concatenate(_split_bf16(jax.nn.silu(c_ref[...]), 2), axis=0)
    acc = _dot(ca, w_ref[...].astype(BF16))
    return acc[:rows] + acc[rows:] + b_ref[...]


def _ada_kernel(c_ref, w_ref, b_ref, o_ref):
    o_ref[...] = _ada_tile(c_ref, w_ref, b_ref)


def _ada(c, w_ada, b_ada, ncols, tn=1024):
    _, d, _ = w_ada.shape
    rows = c.shape[0]
    return pl.pallas_call(
        _ada_kernel,
        grid=(ncols // tn,),
        in_specs=[pl.BlockSpec((rows, d), lambda j: (0, 0)),
                  pl.BlockSpec((None, d, tn), lambda j: (0, 0, j)),
                  pl.BlockSpec((1, tn), lambda j: (0, j))],
        out_specs=pl.BlockSpec((rows, tn), lambda j: (0, j)),
        out_shape=jax.ShapeDtypeStruct((rows, ncols), F32),
        compiler_params=pltpu.CompilerParams(dimension_semantics=("arbitrary",), vmem_limit_bytes=VMEM_LIMIT),
        name="ada",
    )(c, w_ada, b_ada)


def _gelu(x):
    hx = 0.5 * x
    return hx * jnp.tanh(x * (GELU_C1 + GELU_C3 * (x * x))) + hx


def _first_kernel(x_ref, mod_ref, g_ref, w_ref, c_ref, wada_ref, bada_ref, zg_ref, h_ref, modb_ref, wb_ref, *, ncol):
    j = pl.program_id(2)
    first = (pl.program_id(0) == 0) & (pl.program_id(1) == 0)
    tn = zg_ref.shape[1]

    def tile(jj):
        modb_ref[...] = _ada_tile(c_ref, wada_ref, bada_ref)
        if jj == 0:
            scale = g_ref[...] * (1.0 + mod_ref[0, SC1])
            h_ref[...] = (_rms(x_ref[0]) * scale + mod_ref[0, SH1]).astype(BF16)
        zg_ref[...] = _gelu(_dot(h_ref[...], wb_ref[jj])).astype(BF16)

    for jj in range(ncol):
        @pl.when(first & (j == jj))
        def _(jj=jj):
            wb_ref[jj] = w_ref[:, jj * tn:(jj + 1) * tn].astype(BF16)

        pl.when(j == jj)(functools.partial(tile, jj))


def _first(x, mod_a, g, w, c, w_ada, b_ada, ncol=2, tm=512, tn=1024):
    b, s, d = x.shape
    nr = s // tm
    steps = b * nr * ncol
    done = mod_a.shape[1] * d
    slab = (w_ada.shape[2] - done) // steps
    assert slab * steps == w_ada.shape[2] - done and slab % 128 == 0 and done % slab == 0
    step = lambda i, r, j: (i * nr + r) * ncol + j
    return pl.pallas_call(
        functools.partial(_first_kernel, ncol=ncol),
        grid=(b, nr, ncol),
        in_specs=[pl.BlockSpec((1, tm, d), lambda i, r, j: (i, r, 0)),
                  pl.BlockSpec((1, mod_a.shape[1], 1, d), lambda i, r, j: (i, 0, 0, 0)),
                  pl.BlockSpec((1, d), lambda i, r, j: (0, 0)),
                  pl.BlockSpec((None, d, ncol * tn), lambda i, r, j: (0, 0, 0), pipeline_mode=pl.Buffered(1)),
                  pl.BlockSpec(c.shape, lambda i, r, j: (0, 0)),
                  pl.BlockSpec((None, d, slab), lambda i, r, j: (0, 0, done // slab + step(i, r, j))),
                  pl.BlockSpec((1, slab), lambda i, r, j: (0, done // slab + step(i, r, j)))],
        out_specs=[pl.BlockSpec((tm, tn), lambda i, r, j: (i * nr + r, j)),
                   pl.BlockSpec((tm, d), lambda i, r, j: (i * nr + r, 0)),
                   pl.BlockSpec((c.shape[0], slab), lambda i, r, j: (0, step(i, r, j)))],
        out_shape=[jax.ShapeDtypeStruct((b * s, ncol * tn), BF16), jax.ShapeDtypeStruct((b * s, d), BF16),
                   jax.ShapeDtypeStruct((c.shape[0], w_ada.shape[2] - done), F32)],
        scratch_shapes=[pltpu.VMEM((ncol, d, tn), BF16)],
        compiler_params=pltpu.CompilerParams(dimension_semantics=("arbitrary", "arbitrary", "arbitrary"),
                                             vmem_limit_bytes=VMEM_LIMIT),
        name="first",
    )(x, mod_a, g, w, c, w_ada, b_ada)


IN_TILES = ((2, "silu"), (5, "silu"), (6, "gate"), (7, "gate"), (8, "gate"), (9, "gate"), (4, "none"), (3, "logf"))
Z_Q, Z_OG, Z_GATE, Z_IV = 0, 1, 2, 6


def _inproj_kernel(h_ref, w_ref, bg_ref, lb_ref, z_ref, lf_ref, wb_ref):
    j = pl.program_id(0)

    @pl.when(pl.program_id(1) == 0)
    def _():
        wb_ref[...] = w_ref[...].astype(BF16)

    def tile(act):
        acc = _dot(h_ref[...], wb_ref[...])
        if act == "silu":
            hx = 0.5 * acc
            z_ref[...] = (hx * jnp.tanh(hx) + hx).astype(BF16)
        elif act == "gate":
            z_ref[...] = (0.5 * jnp.tanh(0.5 * (acc + bg_ref[...])) + 0.5).astype(BF16)
        elif act == "none":
            z_ref[...] = acc.astype(BF16)
        else:
            lb = jax.nn.softmax(lb_ref[...], axis=0)[0:1, :]
            lf_ref[...] = jnp.log((0.5 + 0.5 * lb) + (0.5 - 0.5 * lb) * jnp.tanh(0.5 * acc))

    acts = [a for _, a in IN_TILES]
    for act in dict.fromkeys(acts):
        first = acts.index(act)
        last = len(acts) - 1 - acts[::-1].index(act)
        pl.when((j >= first) & (j <= last))(functools.partial(tile, act))


def _inproj(h, w, b_gate, hg_lb, tm=1024, tn=1024):
    m, d = h.shape
    ni, nj = m // tm, len(IN_TILES)
    acts = [a for _, a in IN_TILES]
    assert acts[-1] == "logf" and acts.count("logf") == 1
    gate0 = acts.index("gate")
    ngate = acts.count("gate")

    def wcol(j):
        col = j
        for k, (c, _) in enumerate(IN_TILES):
            col = jnp.where(j == k, c, col)
        return col

    last = nj - 1
    return pl.pallas_call(
        _inproj_kernel,
        grid=(nj, ni),
        in_specs=[pl.BlockSpec((tm, d), lambda j, i: (i, 0)),
                  pl.BlockSpec((None, d, tn), lambda j, i: (0, 0, wcol(j))),
                  pl.BlockSpec((1, tn), lambda j, i: (0, jnp.clip(j - gate0, 0, ngate - 1))),
                  pl.BlockSpec(hg_lb.shape, lambda j, i: (0, 0))],
        out_specs=[pl.BlockSpec((tm, tn), lambda j, i: (jnp.where(j < last, i, ni - 1), jnp.minimum(j, last - 1))),
                   pl.BlockSpec((tm, tn), lambda j, i: (jnp.where(j < last, 0, i), 0))],
        out_shape=[jax.ShapeDtypeStruct((m, (nj - 1) * tn), BF16), jax.ShapeDtypeStruct((m, tn), F32)],
        scratch_shapes=[pltpu.VMEM((d, tn), BF16)],
        compiler_params=pltpu.CompilerParams(dimension_semantics=("arbitrary", "arbitrary"),
                                             vmem_limit_bytes=VMEM_LIMIT),
        name="inproj",
    )(h, w, b_gate, hg_lb)


def _level_masks():
    t = np.arange(CHUNK)[:, None]
    s = np.arange(CHUNK)[None, :]
    out = [(t // (2 * m) == s // (2 * m)) & ((t // m) % 2 == 1) & ((s // m) % 2 == 0) for m in LEVELS]
    return np.stack(out).astype(np.float32)


def _decay_matrix():
    t = np.arange(CHUNK)[:, None]
    j = np.arange(CHUNK)[None, :]
    blocks = [j <= t]
    for m in LEVELS:
        r = (t // (2 * m)) * (2 * m) + m - 1
        blocks.append(np.where((t // m) % 2 == 1, (j > r) & (j <= t), (j > t) & (j <= r)))
    blocks.append(j > t)
    w = np.concatenate(blocks, axis=0).astype(np.float32)
    return np.concatenate([w, w], axis=1)


def _mixer_kernel(u_ref, v_ref, q_ref, og_ref, lf_ref, iv_ref, lng_ref, lnb_ref, ws_ref, bst_ref, ng_ref,
                  cw_ref, msk_ref, *rest, n_cast):
    src, (ya_ref, yb_ref), dst, st_ref = rest[:n_cast], rest[n_cast:n_cast + 2], rest[n_cast + 2:-1], rest[-1]
    rows = u_ref.shape[0]
    nchunk = rows // CHUNK
    ngroup = u_ref.shape[1] // GROUP

    @pl.when(pl.program_id(1) == 0)
    def _():
        st_ref[...] = jnp.zeros_like(st_ref)

    for s_ref, d_ref in zip(src, dst):
        d_ref[...] = s_ref[...].astype(BF16)

    v = v_ref[...].astype(F32)
    mu = jnp.mean(v, axis=-1, keepdims=True)
    vc = v - mu
    var = jnp.mean(vc * vc, axis=-1, keepdims=True)
    vn = (vc * lax.rsqrt(var + EPS) * lng_ref[...] + lnb_ref[...]).astype(BF16)
    tri = lax.broadcasted_iota(jnp.int32, (CHUNK, CHUNK), 0) >= lax.broadcasted_iota(jnp.int32, (CHUNK, CHUNK), 1)
    for g in range(ngroup):
        cols = slice(g * GROUP, (g + 1) * GROUP)
        w = jnp.where(tri, ws_ref[g], 0.0).astype(BF16)
        rhs = jnp.concatenate([vn[c * CHUNK:(c + 1) * CHUNK, cols] for c in range(nchunk)], axis=1)
        sg = _dot(w, rhs) + bst_ref[:, g:g + 1]
        for c in range(nchunk):
            r = slice(c * CHUNK, (c + 1) * CHUNK)
            ya_ref[r, cols] = (u_ref[r, cols].astype(F32) * sg[:, c * CHUNK:(c + 1) * CHUNK]).astype(BF16)

    nlev = len(LEVELS)
    odd = lax.broadcasted_iota(jnp.int32, (CHUNK, GROUP), 0) % 2 == 1
    for c in range(nchunk):
        r = slice(c * CHUNK, (c + 1) * CHUNK)
        lf2 = lf_ref[r, :] * LOG2E
        ex = _dot(cw_ref[...], jnp.concatenate(_split_bf16(lf2, 2), axis=0))
        for h0 in range(0, ngroup, HEADS_PER_PASS):
            hs = range(h0, h0 + HEADS_PER_PASS)
            cs = {h: slice(h * GROUP, (h + 1) * GROUP) for h in hs}
            e2 = lambda h, i: jnp.exp2(ex[i * CHUNK:(i + 1) * CHUNK, cs[h]]).astype(BF16)
            qb = {h: q_ref[r, cs[h]] for h in hs}
            kb = {h: (1.0 - jnp.exp2(lf2[:, cs[h]])).astype(BF16) for h in hs}
            st = {h: st_ref[h] for h in hs}
            o = {h: _dot_nt(qb[h] * e2(h, 0), st[h].astype(BF16)) for h in hs}
            a = {}
            for li in range(nlev):
                for h in hs:
                    e = e2(h, 1 + li)
                    p = msk_ref[li] * _dot_nt(qb[h] * e, kb[h] * e)
                    a[h] = p if li == 0 else a[h] + p
            for h in hs:
                iv = iv_ref[r, cs[h]]
                ivf = iv.astype(F32)
                qf = qb[h].astype(F32)
                fh = jnp.exp2(lf2[:, cs[h]])
                kf = 1.0 - fh
                c0 = jnp.sum(qf * kf, axis=-1, keepdims=True)
                c1 = jnp.sum(jnp.where(odd, qf * fh * pltpu.roll(kf, 1, 0), 0.0), axis=-1, keepdims=True)
                oh = o[h] + _dot(a[h].astype(BF16), iv) + c0 * ivf + c1 * pltpu.roll(ivf, 1, 0)
                b_last = ex[CHUNK - 1:CHUNK, cs[h]]
                st_ref[h] = st[h] * jnp.exp2(b_last) + _dot_tn(iv, kb[h] * e2(h, nlev + 1))
                y = _rms(oh) * ng_ref[:, cs[h]] * og_ref[r, cs[h]].astype(F32)
                yb_ref[r, cs[h]] = y.astype(BF16)


def _mixer(zg, z, lf, ln_g, ln_b, ws, bs_t, norm_g, batch, casts, rows=256):
    m, width = lf.shape
    nr = m // batch // rows
    nsteps = batch * nr
    ngroup = width // GROUP
    cw = jnp.asarray(_decay_matrix(), BF16)
    msk = jnp.asarray(_level_masks(), F32)
    row_blk = lambda col: pl.BlockSpec((rows, width), lambda b, r: (b * nr + r, col))
    full = lambda a: pl.BlockSpec(a.shape, lambda b, r: (0,) * a.ndim)
    in_specs = [row_blk(0), row_blk(1), row_blk(Z_Q), row_blk(Z_OG), row_blk(0), row_blk(Z_IV),
                full(ln_g), full(ln_b), full(ws), full(bs_t), full(norm_g), full(cw), full(msk)]
    out_specs = [row_blk(0), row_blk(0)]
    out_shape = [jax.ShapeDtypeStruct((m, width), BF16), jax.ShapeDtypeStruct((m, width), BF16)]
    for a in casts:
        _, n, w = a.shape
        slab = n // nsteps
        assert slab * nsteps == n and slab % 16 == 0
        in_specs.append(pl.BlockSpec((None, slab, w), lambda b, r: (0, b * nr + r, 0)))
        out_specs.append(pl.BlockSpec((slab, w), lambda b, r: (b * nr + r, 0)))
        out_shape.append(jax.ShapeDtypeStruct((n, w), BF16))
    return pl.pallas_call(
        functools.partial(_mixer_kernel, n_cast=len(casts)),
        grid=(batch, nr),
        in_specs=in_specs,
        out_specs=out_specs,
        out_shape=out_shape,
        scratch_shapes=[pltpu.VMEM((ngroup, GROUP, GROUP), F32)],
        compiler_params=pltpu.CompilerParams(dimension_semantics=("arbitrary", "arbitrary"),
                                             vmem_limit_bytes=VMEM_LIMIT),
        name="mixer",
    )(zg, zg, z, z, lf, z, ln_g, ln_b, ws, bs_t, norm_g, cw, msk, *casts)


def _merge_kernel(ya_ref, yb_ref, ga_ref, gb_ref, x_ref, mod_ref, wa_ref, wb_ref, wo_ref, g_ref, x1_ref, h2_ref):
    tm = ya_ref.shape[0]
    scale = g_ref[...] * (1.0 + mod_ref[0, SC2])
    for p in range(MERGE_PARTS):
        r = slice(p * tm // MERGE_PARTS, (p + 1) * tm // MERGE_PARTS)
        pa = _dot(ya_ref[r, :], wa_ref[...])
        pb = _dot(yb_ref[r, :], wb_ref[...])
        y = (ga_ref[r, :].astype(F32) * pa + gb_ref[r, :].astype(F32) * pb).astype(BF16)
        x1 = x_ref[0, r, :] + mod_ref[0, GT1] * _dot(y, wo_ref[...])
        x1_ref[0, r, :] = x1
        h2_ref[0, r, :] = (_rms(x1) * scale + mod_ref[0, SH2]).astype(BF16)


def _merge(ya, yb, z, x, mod_b, wa, wb, wo, g, tm=512):
    b, s, d = x.shape
    nr = s // tm
    width = ya.shape[1]
    resident = lambda a: pl.BlockSpec(a.shape, lambda i, r: (0,) * a.ndim, pipeline_mode=pl.Buffered(1))
    return pl.pallas_call(
        _merge_kernel,
        grid=(b, nr),
        in_specs=[pl.BlockSpec((tm, width), lambda i, r: (i * nr + r, 0)),
                  pl.BlockSpec((tm, width), lambda i, r: (i * nr + r, 0)),
                  pl.BlockSpec((tm, d), lambda i, r: (i * nr + r, Z_GATE * width // d)),
                  pl.BlockSpec((tm, d), lambda i, r: (i * nr + r, Z_GATE * width // d + 1)),
                  pl.BlockSpec((1, tm, d), lambda i, r: (i, r, 0)),
                  pl.BlockSpec((1, 4, 1, d), lambda i, r: (i, 0, 0, 0)),
                  resident(wa), resident(wb), resident(wo),
                  pl.BlockSpec((1, d), lambda i, r: (0, 0))],
        out_specs=[pl.BlockSpec((1, tm, d), lambda i, r: (i, r, 0)),
                   pl.BlockSpec((1, tm, d), lambda i, r: (i, r, 0))],
        out_shape=[jax.ShapeDtypeStruct((b, s, d), F32), jax.ShapeDtypeStruct((b, s, d), BF16)],
        compiler_params=pltpu.CompilerParams(dimension_semantics=("arbitrary", "arbitrary"),
                                             vmem_limit_bytes=VMEM_LIMIT),
        name="merge",
    )(ya, yb, z, z, x, mod_b, wa, wb, wo, g)


def _ffn_kernel(h_ref, wa_ref, wu_ref, wo_ref, x1_ref, mod_ref, g_ref, o_ref, acc_ref, *, nt):
    t = pl.program_id(2)

    @pl.when(t == 0)
    def _():
        acc_ref[...] = jnp.zeros_like(acc_ref)

    @pl.when(t < nt)
    def _():
        h = h_ref[0]
        a = _dot(h, wa_ref[...])
        up = _dot(h, wu_ref[...])
        ha = 0.5 * a
        acc_ref[...] += _dot(((ha * jnp.tanh(ha) + ha) * up).astype(BF16), wo_ref[...])

    @pl.when(t >= nt)
    def _():
        half = o_ref.shape[1]
        base = (t - nt) * half
        gate = mod_ref[0, GT2]
        g = g_ref[...]

        def body(i, carry):
            rows = pl.ds(pl.multiple_of(i * ROW_GROUP, ROW_GROUP), ROW_GROUP)
            arows = pl.ds(pl.multiple_of(base + i * ROW_GROUP, ROW_GROUP), ROW_GROUP)
            o_ref[0, rows, :] = _rms(x1_ref[0, rows, :] + gate * acc_ref[arows, :]) * g
            return carry

        lax.fori_loop(0, half // ROW_GROUP, body, 0, unroll=8)


def _ffn(h2, w_in, w_out, x1, mod_b, g, tm=1024, th=512):
    b, s, d = x1.shape
    hidden = w_out.shape[0]
    nt = hidden // th
    half = tm // 2
    nr = s // tm
    wt = lambda t: jnp.where(t < nt, t, 0)
    fin = lambda r, t: 2 * r + jnp.clip(t - nt, 0, 1)

    def h2_blk(i, r, t):
        lin = jnp.minimum(i * nr + r + (t >= nt).astype(jnp.int32), b * nr - 1)
        return (lin // nr, lin % nr, 0)

    return pl.pallas_call(
        functools.partial(_ffn_kernel, nt=nt),
        grid=(b, s // tm, nt + 2),
        in_specs=[pl.BlockSpec((1, tm, d), h2_blk),
                  pl.BlockSpec((d, th), lambda i, r, t: (0, wt(t))),
                  pl.BlockSpec((d, th), lambda i, r, t: (0, nt + wt(t))),
                  pl.BlockSpec((th, d), lambda i, r, t: (wt(t), 0)),
                  pl.BlockSpec((1, half, d), lambda i, r, t: (i, fin(r, t), 0)),
                  pl.BlockSpec((1, 4, 1, d), lambda i, r, t: (i, 0, 0, 0)),
                  pl.BlockSpec((1, d), lambda i, r, t: (0, 0))],
        out_specs=pl.BlockSpec((1, half, d), lambda i, r, t: (i, fin(r, t), 0)),
        out_shape=jax.ShapeDtypeStruct((b, s, d), F32),
        scratch_shapes=[pltpu.VMEM((tm, d), F32)],
        compiler_params=pltpu.CompilerParams(dimension_semantics=("arbitrary", "arbitrary", "arbitrary"),
                                             vmem_limit_bytes=VMEM_LIMIT),
        name="ffn",
    )(h2, w_in, w_in, w_out, x1, mod_b, g)


def kernel(x, c, w_ada, b_ada, norm1_g, w_in, b_gate, gmlp_ln_g, gmlp_ln_b, gmlp_ws, gmlp_bs, hg_lb, hg_norm_g,
           w_branch_gmlp, w_branch_hg, w_out, norm2_g, w_ffn_in, w_ffn_out, final_norm_g):
    batch, seq, d = x.shape
    depth = w_ada.shape[0]
    width = w_branch_gmlp.shape[1]
    assert depth == 1 and width == 8 * GROUP and gmlp_ws.shape[2] == CHUNK and seq % 256 == 0
    assert w_in.shape[2] == 6 * width + 2 * d and hg_lb.shape[0] == depth + 1
    tn = 1024
    assert width == tn and d == 2 * tn

    mod_a = _ada(c, w_ada, b_ada, 2 * d).reshape(batch, 2, 1, d)
    zg, h1, mod_b = _first(x, mod_a, norm1_g, w_in, c, w_ada, b_ada)
    mod_b = mod_b.reshape(batch, 4, 1, d)
    z, lf = _inproj(h1, w_in, b_gate, hg_lb)

    ya, yb, w_a, w_b, w_o, w_fi, w_fo = _mixer(
        zg, z, lf, gmlp_ln_g, gmlp_ln_b, gmlp_ws[0], gmlp_bs[0].T, hg_norm_g, batch,
        casts=(w_branch_gmlp, w_branch_hg, w_out, w_ffn_in, w_ffn_out))

    x1, h2 = _merge(ya, yb, z, x, mod_b, w_a, w_b, w_o, norm2_g)
    return _ffn(h2, w_fi, w_fo, x1, mod_b, final_norm_g.reshape(1, d))
```

```python
import functools

import numpy as np
import jax
import jax.numpy as jnp
from jax import lax
from jax.experimental import pallas as pl
from jax.experimental.pallas import tpu as pltpu

F32 = jnp.float32
BF16 = jnp.bfloat16
EPS = 1e-6
LOG2E = 1.4426950408889634
GELU_C1 = 0.7978845608028654
GELU_C3 = 0.044715 * GELU_C1

SH1, SC1 = 0, 1
GT1, SH2, SC2, GT2 = 0, 1, 2, 3

GROUP = 128
CHUNK = 128
MERGE_PARTS = 2
ROW_GROUP = 16
HEADS_PER_PASS = 4
LEVELS = (64, 32, 16, 8, 4, 2)
VMEM_LIMIT = 56 * 1024 * 1024


def _dot(a, b):
    return jnp.dot(a, b, preferred_element_type=F32)


def _dot_nt(a, b):
    return lax.dot_general(a, b, (((1,), (1,)), ((), ())), preferred_element_type=F32)


def _dot_tn(a, b):
    return lax.dot_general(a, b, (((0,), (0,)), ((), ())), preferred_element_type=F32)


def _rms(x):
    return x * lax.rsqrt(jnp.mean(x * x, axis=-1, keepdims=True) + EPS)


def _split_bf16(x, parts):
    out = []
    for _ in range(parts - 1):
        p = x.astype(BF16)
        out.append(p)
        x = x - p.astype(F32)
    out.append(x.astype(BF16))
    return out


def _ada_tile(c_ref, w_ref, b_ref):
    rows = c_ref.shape[0]
    ca = jnp.concatenate(_split_bf16(jax.nn.silu(c_ref[...]), 2), axis=0)
    acc = _dot(ca, w_ref[...].astype(BF16))
    return acc[:rows] + acc[rows:] + b_ref[...]


def _ada_kernel(c_ref, w_ref, b_ref, o_ref):
    o_ref[...] = _ada_tile(c_ref, w_ref, b_ref)


def _ada(c, w_ada, b_ada, ncols, tn=1024):
    _, d, _ = w_ada.shape
    rows = c.shape[0]
    return pl.pallas_call(
        _ada_kernel,
        grid=(ncols // tn,),
        in_specs=[pl.BlockSpec((rows, d), lambda j: (0, 0)),
                  pl.BlockSpec((None, d, tn), lambda j: (0, 0, j)),
                  pl.BlockSpec((1, tn), lambda j: (0, j))],
        out_specs=pl.BlockSpec((rows, tn), lambda j: (0, j)),
        out_shape=jax.ShapeDtypeStruct((rows, ncols), F32),
        compiler_params=pltpu.CompilerParams(dimension_semantics=("arbitrary",), vmem_limit_bytes=VMEM_LIMIT),
        name="ada",
    )(c, w_ada, b_ada)


def _gelu(x):
    hx = 0.5 * x
    return hx * jnp.tanh(x * (GELU_C1 + GELU_C3 * (x * x))) + hx


def _first_kernel(x_ref, mod_ref, g_ref, w_ref, c_ref, wada_ref, bada_ref, zg_ref, h_ref, modb_ref, wb_ref, *, ncol):
    j = pl.program_id(2)
    first = (pl.program_id(0) == 0) & (pl.program_id(1) == 0)
    tn = zg_ref.shape[1]

    def tile(jj):
        modb_ref[...] = _ada_tile(c_ref, wada_ref, bada_ref)
        if jj == 0:
            scale = g_ref[...] * (1.0 + mod_ref[0, SC1])
            h_ref[...] = (_rms(x_ref[0]) * scale + mod_ref[0, SH1]).astype(BF16)
        zg_ref[...] = _gelu(_dot(h_ref[...], wb_ref[jj])).astype(BF16)

    for jj in range(ncol):
        @pl.when(first & (j == jj))
        def _(jj=jj):
            wb_ref[jj] = w_ref[:, jj * tn:(jj + 1) * tn].astype(BF16)

        pl.when(j == jj)(functools.partial(tile, jj))


def _first(x, mod_a, g, w, c, w_ada, b_ada, ncol=2, tm=512, tn=1024):
    b, s, d = x.shape
    nr = s // tm
    steps = b * nr * ncol
    done = mod_a.shape[1] * d
    slab = (w_ada.shape[2] - done) // steps
    assert slab * steps == w_ada.shape[2] - done and slab % 128 == 0 and done % slab == 0
    step = lambda i, r, j: (i * nr + r) * ncol + j
    return pl.pallas_call(
        functools.partial(_first_kernel, ncol=ncol),
        grid=(b, nr, ncol),
        in_specs=[pl.BlockSpec((1, tm, d), lambda i, r, j: (i, r, 0)),
                  pl.BlockSpec((1, mod_a.shape[1], 1, d), lambda i, r, j: (i, 0, 0, 0)),
                  pl.BlockSpec((1, d), lambda i, r, j: (0, 0)),
                  pl.BlockSpec((None, d, ncol * tn), lambda i, r, j: (0, 0, 0), pipeline_mode=pl.Buffered(1)),
                  pl.BlockSpec(c.shape, lambda i, r, j: (0, 0)),
                  pl.BlockSpec((None, d, slab), lambda i, r, j: (0, 0, done // slab + step(i, r, j))),
                  pl.BlockSpec((1, slab), lambda i, r, j: (0, done // slab + step(i, r, j)))],
        out_specs=[pl.BlockSpec((tm, tn), lambda i, r, j: (i * nr + r, j)),
                   pl.BlockSpec((tm, d), lambda i, r, j: (i * nr + r, 0)),
                   pl.BlockSpec((c.shape[0], slab), lambda i, r, j: (0, step(i, r, j)))],
        out_shape=[jax.ShapeDtypeStruct((b * s, ncol * tn), BF16), jax.ShapeDtypeStruct((b * s, d), BF16),
                   jax.ShapeDtypeStruct((c.shape[0], w_ada.shape[2] - done), F32)],
        scratch_shapes=[pltpu.VMEM((ncol, d, tn), BF16)],
        compiler_params=pltpu.CompilerParams(dimension_semantics=("arbitrary", "arbitrary", "arbitrary"),
                                             vmem_limit_bytes=VMEM_LIMIT),
        name="first",
    )(x, mod_a, g, w, c, w_ada, b_ada)


IN_TILES = ((2, "silu"), (5, "silu"), (6, "gate"), (7, "gate"), (8, "gate"), (9, "gate"), (4, "none"), (3, "logf"))
Z_Q, Z_OG, Z_GATE, Z_IV = 0, 1, 2, 6


def _inproj_kernel(h_ref, w_ref, bg_ref, lb_ref, z_ref, lf_ref, wb_ref):
    j = pl.program_id(0)

    @pl.when(pl.program_id(1) == 0)
    def _():
        wb_ref[...] = w_ref[...].astype(BF16)

    def tile(act):
        acc = _dot(h_ref[...], wb_ref[...])
        if act == "silu":
            hx = 0.5 * acc
            z_ref[...] = (hx * jnp.tanh(hx) + hx).astype(BF16)
        elif act == "gate":
            z_ref[...] = (0.5 * jnp.tanh(0.5 * (acc + bg_ref[...])) + 0.5).astype(BF16)
        elif act == "none":
            z_ref[...] = acc.astype(BF16)
        else:
            lb = jax.nn.softmax(lb_ref[...], axis=0)[0:1, :]
            lf_ref[...] = jnp.log((0.5 + 0.5 * lb) + (0.5 - 0.5 * lb) * jnp.tanh(0.5 * acc))

    acts = [a for _, a in IN_TILES]
    for act in dict.fromkeys(acts):
        first = acts.index(act)
        last = len(acts) - 1 - acts[::-1].index(act)
        pl.when((j >= first) & (j <= last))(functools.partial(tile, act))


def _inproj(h, w, b_gate, hg_lb, tm=1024, tn=1024):
    m, d = h.shape
    ni, nj = m // tm, len(IN_TILES)
    acts = [a for _, a in IN_TILES]
    assert acts[-1] == "logf" and acts.count("logf") == 1
    gate0 = acts.index("gate")
    ngate = acts.count("gate")

    def wcol(j):
        col = j
        for k, (c, _) in enumerate(IN_TILES):
            col = jnp.where(j == k, c, col)
        return col

    last = nj - 1
    return pl.pallas_call(
        _inproj_kernel,
        grid=(nj, ni),
        in_specs=[pl.BlockSpec((tm, d), lambda j, i: (i, 0)),
                  pl.BlockSpec((None, d, tn), lambda j, i: (0, 0, wcol(j))),
                  pl.BlockSpec((1, tn), lambda j, i: (0, jnp.clip(j - gate0, 0, ngate - 1))),
                  pl.BlockSpec(hg_lb.shape, lambda j, i: (0, 0))],
        out_specs=[pl.BlockSpec((tm, tn), lambda j, i: (jnp.where(j < last, i, ni - 1), jnp.minimum(j, last - 1))),
                   pl.BlockSpec((tm, tn), lambda j, i: (jnp.where(j < last, 0, i), 0))],
        out_shape=[jax.ShapeDtypeStruct((m, (nj - 1) * tn), BF16), jax.ShapeDtypeStruct((m, tn), F32)],
        scratch_shapes=[pltpu.VMEM((d, tn), BF16)],
        compiler_params=pltpu.CompilerParams(dimension_semantics=("arbitrary", "arbitrary"),
                                             vmem_limit_bytes=VMEM_LIMIT),
        name="inproj",
    )(h, w, b_gate, hg_lb)


def _level_masks():
    t = np.arange(CHUNK)[:, None]
    s = np.arange(CHUNK)[None, :]
    out = [(t // (2 * m) == s // (2 * m)) & ((t // m) % 2 == 1) & ((s // m) % 2 == 0) for m in LEVELS]
    return np.stack(out).astype(np.float32)


def _decay_matrix():
    t = np.arange(CHUNK)[:, None]
    j = np.arange(CHUNK)[None, :]
    blocks = [j <= t]
    for m in LEVELS:
        r = (t // (2 * m)) * (2 * m) + m - 1
        blocks.append(np.where((t // m) % 2 == 1, (j > r) & (j <= t), (j > t) & (j <= r)))
    blocks.append(j > t)
    w = np.concatenate(blocks, axis=0).astype(np.float32)
    return np.concatenate([w, w], axis=1)


def _mixer_kernel(u_ref, v_ref, q_ref, og_ref, lf_ref, iv_ref, lng_ref, lnb_ref, ws_ref, bst_ref, ng_ref,
                  cw_ref, msk_ref, *rest, n_cast):
    src, (ya_ref, yb_ref), dst, st_ref = rest[:n_cast], rest[n_cast:n_cast + 2], rest[n_cast + 2:-1], rest[-1]
    rows = u_ref.shape[0]
    nchunk = rows // CHUNK
    ngroup = u_ref.shape[1] // GROUP

    @pl.when(pl.program_id(1) == 0)
    def _():
        st_ref[...] = jnp.zeros_like(st_ref)

    for s_ref, d_ref in zip(src, dst):
        d_ref[...] = s_ref[...].astype(BF16)

    v = v_ref[...].astype(F32)
    mu = jnp.mean(v, axis=-1, keepdims=True)
    vc = v - mu
    var = jnp.mean(vc * vc, axis=-1, keepdims=True)
    vn = (vc * lax.rsqrt(var + EPS) * lng_ref[...] + lnb_ref[...]).astype(BF16)
    tri = lax.broadcasted_iota(jnp.int32, (CHUNK, CHUNK), 0) >= lax.broadcasted_iota(jnp.int32, (CHUNK, CHUNK), 1)
    for g in range(ngroup):
        cols = slice(g * GROUP, (g + 1) * GROUP)
        w = jnp.where(tri, ws_ref[g], 0.0).astype(BF16)
        rhs = jnp.concatenate([vn[c * CHUNK:(c + 1) * CHUNK, cols] for c in range(nchunk)], axis=1)
        sg = _dot(w, rhs) + bst_ref[:, g:g + 1]
        for c in range(nchunk):
            r = slice(c * CHUNK, (c + 1) * CHUNK)
            ya_ref[r, cols] = (u_ref[r, cols].astype(F32) * sg[:, c * CHUNK:(c + 1) * CHUNK]).astype(BF16)

    nlev = len(LEVELS)
    odd = lax.broadcasted_iota(jnp.int32, (CHUNK, GROUP), 0) % 2 == 1
    for c in range(nchunk):
        r = slice(c * CHUNK, (c + 1) * CHUNK)
        lf2 = lf_ref[r, :] * LOG2E
        ex = _dot(cw_ref[...], jnp.concatenate(_split_bf16(lf2, 2), axis=0))
        for h0 in range(0, ngroup, HEADS_PER_PASS):
            hs = range(h0, h0 + HEADS_PER_PASS)
            cs = {h: slice(h * GROUP, (h + 1) * GROUP) for h in hs}
            e2 = lambda h, i: jnp.exp2(ex[i * CHUNK:(i + 1) * CHUNK, cs[h]]).astype(BF16)
            qb = {h: q_ref[r, cs[h]] for h in hs}
            kb = {h: (1.0 - jnp.exp2(lf2[:, cs[h]])).astype(BF16) for h in hs}
            st = {h: st_ref[h] for h in hs}
            o = {h: _dot_nt(qb[h] * e2(h, 0), st[h].astype(BF16)) for h in hs}
            a = {}
            for li in range(nlev):
                for h in hs:
                    e = e2(h, 1 + li)
                    p = msk_ref[li] * _dot_nt(qb[h] * e, kb[h] * e)
                    a[h] = p if li == 0 else a[h] + p
            for h in hs:
                iv = iv_ref[r, cs[h]]
                ivf = iv.astype(F32)
                qf = qb[h].astype(F32)
                fh = jnp.exp2(lf2[:, cs[h]])
                kf = 1.0 - fh
                c0 = jnp.sum(qf * kf, axis=-1, keepdims=True)
                c1 = jnp.sum(jnp.where(odd, qf * fh * pltpu.roll(kf, 1, 0), 0.0), axis=-1, keepdims=True)
                oh = o[h] + _dot(a[h].astype(BF16), iv) + c0 * ivf + c1 * pltpu.roll(ivf, 1, 0)
                b_last = ex[CHUNK - 1:CHUNK, cs[h]]
                st_ref[h] = st[h] * jnp.exp2(b_last) + _dot_tn(iv, kb[h] * e2(h, nlev + 1))
                y = _rms(oh) * ng_ref[:, cs[h]] * og_ref[r, cs[h]].astype(F32)
                yb_ref[r, cs[h]] = y.astype(BF16)


def _mixer(zg, z, lf, ln_g, ln_b, ws, bs_t, norm_g, batch, casts, rows=256):
    m, width = lf.shape
    nr = m // batch // rows
    nsteps = batch * nr
    ngroup = width // GROUP
    cw = jnp.asarray(_decay_matrix(), BF16)
    msk = jnp.asarray(_level_masks(), F32)
    row_blk = lambda col: pl.BlockSpec((rows, width), lambda b, r: (b * nr + r, col))
    full = lambda a: pl.BlockSpec(a.shape, lambda b, r: (0,) * a.ndim)
    in_specs = [row_blk(0), row_blk(1), row_blk(Z_Q), row_blk(Z_OG), row_blk(0), row_blk(Z_IV),
                full(ln_g), full(ln_b), full(ws), full(bs_t), full(norm_g), full(cw), full(msk)]
    out_specs = [row_blk(0), row_blk(0)]
    out_shape = [jax.ShapeDtypeStruct((m, width), BF16), jax.ShapeDtypeStruct((m, width), BF16)]
    for a in casts:
        _, n, w = a.shape
        slab = n // nsteps
        assert slab * nsteps == n and slab % 16 == 0
        in_specs.append(pl.BlockSpec((None, slab, w), lambda b, r: (0, b * nr + r, 0)))
        out_specs.append(pl.BlockSpec((slab, w), lambda b, r: (b * nr + r, 0)))
        out_shape.append(jax.ShapeDtypeStruct((n, w), BF16))
    return pl.pallas_call(
        functools.partial(_mixer_kernel, n_cast=len(casts)),
        grid=(batch, nr),
        in_specs=in_specs,
        out_specs=out_specs,
        out_shape=out_shape,
        scratch_shapes=[pltpu.VMEM((ngroup, GROUP, GROUP), F32)],
        compiler_params=pltpu.CompilerParams(dimension_semantics=("arbitrary", "arbitrary"),
                                             vmem_limit_bytes=VMEM_LIMIT),
        name="mixer",
    )(zg, zg, z, z, lf, z, ln_g, ln_b, ws, bs_t, norm_g, cw, msk, *casts)


def _merge_kernel(ya_ref, yb_ref, ga_ref, gb_ref, x_ref, mod_ref, wa_ref, wb_ref, wo_ref, g_ref, x1_ref, h2_ref):
    tm = ya_ref.shape[0]
    scale = g_ref[...] * (1.0 + mod_ref[0, SC2])
    for p in range(MERGE_PARTS):
        r = slice(p * tm // MERGE_PARTS, (p + 1) * tm // MERGE_PARTS)
        pa = _dot(ya_ref[r, :], wa_ref[...])
        pb = _dot(yb_ref[r, :], wb_ref[...])
        y = (ga_ref[r, :].astype(F32) * pa + gb_ref[r, :].astype(F32) * pb).astype(BF16)
        x1 = x_ref[0, r, :] + mod_ref[0, GT1] * _dot(y, wo_ref[...])
        x1_ref[0, r, :] = x1
        h2_ref[0, r, :] = (_rms(x1) * scale + mod_ref[0, SH2]).astype(BF16)


def _merge(ya, yb, z, x, mod_b, wa, wb, wo, g, tm=512):
    b, s, d = x.shape
    nr = s // tm
    width = ya.shape[1]
    resident = lambda a: pl.BlockSpec(a.shape, lambda i, r: (0,) * a.ndim, pipeline_mode=pl.Buffered(1))
    return pl.pallas_call(
        _merge_kernel,
        grid=(b, nr),
        in_specs=[pl.BlockSpec((tm, width), lambda i, r: (i * nr + r, 0)),
                  pl.BlockSpec((tm, width), lambda i, r: (i * nr + r, 0)),
                  pl.BlockSpec((tm, d), lambda i, r: (i * nr + r, Z_GATE * width // d)),
                  pl.BlockSpec((tm, d), lambda i, r: (i * nr + r, Z_GATE * width // d + 1)),
                  pl.BlockSpec((1, tm, d), lambda i, r: (i, r, 0)),
                  pl.BlockSpec((1, 4, 1, d), lambda i, r: (i, 0, 0, 0)),
                  resident(wa), resident(wb), resident(wo),
                  pl.BlockSpec((1, d), lambda i, r: (0, 0))],
        out_specs=[pl.BlockSpec((1, tm, d), lambda i, r: (i, r, 0)),
                   pl.BlockSpec((1, tm, d), lambda i, r: (i, r, 0))],
        out_shape=[jax.ShapeDtypeStruct((b, s, d), F32), jax.ShapeDtypeStruct((b, s, d), BF16)],
        compiler_params=pltpu.CompilerParams(dimension_semantics=("arbitrary", "arbitrary"),
                                             vmem_limit_bytes=VMEM_LIMIT),
        name="merge",
    )(ya, yb, z, z, x, mod_b, wa, wb, wo, g)


def _ffn_kernel(h_ref, wa_ref, wu_ref, wo_ref, x1_ref, mod_ref, g_ref, o_ref, acc_ref, *, nt):
    t = pl.program_id(2)

    def hidden_tile():
        h = h_ref[0]
        a = _dot(h, wa_ref[...])
        up = _dot(h, wu_ref[...])
        ha = 0.5 * a
        return _dot(((ha * jnp.tanh(ha) + ha) * up).astype(BF16), wo_ref[...])

    @pl.when(t == 0)
    def _():
        acc_ref[...] = hidden_tile()

    @pl.when((t > 0) & (t < nt))
    def _():
        acc_ref[...] += hidden_tile()

    @pl.when(t >= nt)
    def _():
        half = o_ref.shape[1]
        base = (t - nt) * half
        gate = mod_ref[0, GT2]
        g = g_ref[...]

        def body(i, carry):
            rows = pl.ds(pl.multiple_of(i * ROW_GROUP, ROW_GROUP), ROW_GROUP)
            arows = pl.ds(pl.multiple_of(base + i * ROW_GROUP, ROW_GROUP), ROW_GROUP)
            o_ref[0, rows, :] = _rms(x1_ref[0, rows, :] + gate * acc_ref[arows, :]) * g
            return carry

        lax.fori_loop(0, half // ROW_GROUP, body, 0, unroll=8)


def _ffn(h2, w_in, w_out, x1, mod_b, g, tm=1024, th=512):
    b, s, d = x1.shape
    hidden = w_out.shape[0]
    nt = hidden // th
    half = tm // 2
    nr = s // tm
    wt = lambda t: jnp.where(t < nt, t, 0)
    fin = lambda r, t: 2 * r + jnp.clip(t - nt, 0, 1)

    def h2_blk(i, r, t):
        lin = jnp.minimum(i * nr + r + (t >= nt).astype(jnp.int32), b * nr - 1)
        return (lin // nr, lin % nr, 0)

    return pl.pallas_call(
        functools.partial(_ffn_kernel, nt=nt),
        grid=(b, s // tm, nt + 2),
        in_specs=[pl.BlockSpec((1, tm, d), h2_blk),
                  pl.BlockSpec((d, th), lambda i, r, t: (0, wt(t))),
                  pl.BlockSpec((d, th), lambda i, r, t: (0, nt + wt(t))),
                  pl.BlockSpec((th, d), lambda i, r, t: (wt(t), 0)),
                  pl.BlockSpec((1, half, d), lambda i, r, t: (i, fin(r, t), 0)),
                  pl.BlockSpec((1, 4, 1, d), lambda i, r, t: (i, 0, 0, 0)),
                  pl.BlockSpec((1, d), lambda i, r, t: (0, 0))],
        out_specs=pl.BlockSpec((1, half, d), lambda i, r, t: (i, fin(r, t), 0)),
        out_shape=jax.ShapeDtypeStruct((b, s, d), F32),
        scratch_shapes=[pltpu.VMEM((tm, d), F32)],
        compiler_params=pltpu.CompilerParams(dimension_semantics=("arbitrary", "arbitrary", "arbitrary"),
                                             vmem_limit_bytes=VMEM_LIMIT),
        name="ffn",
    )(h2, w_in, w_in, w_out, x1, mod_b, g)


def kernel(x, c, w_ada, b_ada, norm1_g, w_in, b_gate, gmlp_ln_g, gmlp_ln_b, gmlp_ws, gmlp_bs, hg_lb, hg_norm_g,
           w_branch_gmlp, w_branch_hg, w_out, norm2_g, w_ffn_in, w_ffn_out, final_norm_g):
    batch, seq, d = x.shape
    depth = w_ada.shape[0]
    width = w_branch_gmlp.shape[1]
    assert depth == 1 and width == 8 * GROUP and gmlp_ws.shape[2] == CHUNK and seq % 256 == 0
    assert w_in.shape[2] == 6 * width + 2 * d and hg_lb.shape[0] == depth + 1
    tn = 1024
    assert width == tn and d == 2 * tn

    mod_a = _ada(c, w_ada, b_ada, 2 * d).reshape(batch, 2, 1, d)
    zg, h1, mod_b = _first(x, mod_a, norm1_g, w_in, c, w_ada, b_ada)
    mod_b = mod_b.reshape(batch, 4, 1, d)
    z, lf = _inproj(h1, w_in, b_gate, hg_lb)

    ya, yb, w_a, w_b, w_o, w_fi, w_fo = _mixer(
        zg, z, lf, gmlp_ln_g, gmlp_ln_b, gmlp_ws[0], gmlp_bs[0].T, hg_norm_g, batch,
        casts=(w_branch_gmlp, w_branch_hg, w_out, w_ffn_in, w_ffn_out))

    x1, h2 = _merge(ya, yb, z, x, mod_b, w_a, w_b, w_o, norm2_g)
    return _ffn(h2, w_fi, w_fo, x1, mod_b, final_norm_g.reshape(1, d))
```

```python
import functools

import numpy as np
import jax
import jax.numpy as jnp
from jax import lax
from jax.experimental import pallas as pl
from jax.experimental.pallas import tpu as pltpu

F32 = jnp.float32
BF16 = jnp.bfloat16
EPS = 1e-6
LOG2E = 1.4426950408889634
GELU_C1 = 0.7978845608028654
GELU_C3 = 0.044715 * GELU_C1

SH1, SC1 = 0, 1
GT1, SH2, SC2, GT2 = 0, 1, 2, 3

GROUP = 128
CHUNK = 128
MERGE_PARTS = 2
ROW_GROUP = 16
HEADS_PER_PASS = 8
LEVELS = (64, 32, 16, 8, 4, 2)
MATMUL_LEVELS = (4, 2)
VMEM_LIMIT = 56 * 1024 * 1024


def _dot(a, b):
    return jnp.dot(a, b, preferred_element_type=F32)


def _dot_nt(a, b):
    return lax.dot_general(a, b, (((1,), (1,)), ((), ())), preferred_element_type=F32)


def _dot_tn(a, b):
    return lax.dot_general(a, b, (((0,), (0,)), ((), ())), preferred_element_type=F32)


def _rms(x):
    return x * lax.rsqrt(jnp.mean(x * x, axis=-1, keepdims=True) + EPS)


def _split_bf16(x, parts):
    out = []
    for _ in range(parts - 1):
        p = x.astype(BF16)
        out.append(p)
        x = x - p.astype(F32)
    out.append(x.astype(BF16))
    return out


def _ada_tile(c_ref, w_ref, b_ref):
    rows = c_ref.shape[0]
    ca = jnp.concatenate(_split_bf16(jax.nn.silu(c_ref[...]), 2), axis=0)
    acc = _dot(ca, w_ref[...].astype(BF16))
    return acc[:rows] + acc[rows:] + b_ref[...]


def _ada_kernel(c_ref, w_ref, b_ref, o_ref):
    o_ref[...] = _ada_tile(c_ref, w_ref, b_ref)


def _ada(c, w_ada, b_ada, ncols, tn=1024):
    _, d, _ = w_ada.shape
    rows = c.shape[0]
    return pl.pallas_call(
        _ada_kernel,
        grid=(ncols // tn,),
        in_specs=[pl.BlockSpec((rows, d), lambda j: (0, 0)),
                  pl.BlockSpec((None, d, tn), lambda j: (0, 0, j)),
                  pl.BlockSpec((1, tn), lambda j: (0, j))],
        out_specs=pl.BlockSpec((rows, tn), lambda j: (0, j)),
        out_shape=jax.ShapeDtypeStruct((rows, ncols), F32),
        compiler_params=pltpu.CompilerParams(dimension_semantics=("arbitrary",), vmem_limit_bytes=VMEM_LIMIT),
        name="ada",
    )(c, w_ada, b_ada)


def _gelu(x):
    hx = 0.5 * x
    return hx * jnp.tanh(x * (GELU_C1 + GELU_C3 * (x * x))) + hx


def _first_kernel(x_ref, sh_ref, sc_ref, g_ref, w_ref, c_ref, wada_ref, bada_ref, zg_ref, h_ref, modb_ref, wb_ref, *, ncol):
    j = pl.program_id(2)
    first = (pl.program_id(0) == 0) & (pl.program_id(1) == 0)
    tn = zg_ref.shape[1]

    def tile(jj):
        modb_ref[...] = _ada_tile(c_ref, wada_ref, bada_ref)
        if jj == 0:
            row = pl.ds(pl.program_id(0), 1)
            scale = g_ref[...] * (1.0 + sc_ref[row, :])
            h_ref[...] = (_rms(x_ref[0]) * scale + sh_ref[row, :]).astype(BF16)
        zg_ref[...] = _gelu(_dot(h_ref[...], wb_ref[jj])).astype(BF16)

    for jj in range(ncol):
        @pl.when(first & (j == jj))
        def _(jj=jj):
            wb_ref[jj] = w_ref[:, jj * tn:(jj + 1) * tn].astype(BF16)

        pl.when(j == jj)(functools.partial(tile, jj))


def _first(x, mod_a, g, w, c, w_ada, b_ada, ncol=2, tm=512, tn=1024):
    b, s, d = x.shape
    nr = s // tm
    steps = b * nr * ncol
    done = mod_a.shape[1]
    slab = (w_ada.shape[2] - done) // steps
    assert slab * steps == w_ada.shape[2] - done and slab % 128 == 0 and done % slab == 0
    step = lambda i, r, j: (i * nr + r) * ncol + j
    return pl.pallas_call(
        functools.partial(_first_kernel, ncol=ncol),
        grid=(b, nr, ncol),
        in_specs=[pl.BlockSpec((1, tm, d), lambda i, r, j: (i, r, 0)),
                  pl.BlockSpec((b, d), lambda i, r, j: (0, SH1)),
                  pl.BlockSpec((b, d), lambda i, r, j: (0, SC1)),
                  pl.BlockSpec((1, d), lambda i, r, j: (0, 0)),
                  pl.BlockSpec((None, d, ncol * tn), lambda i, r, j: (0, 0, 0), pipeline_mode=pl.Buffered(1)),
                  pl.BlockSpec(c.shape, lambda i, r, j: (0, 0)),
                  pl.BlockSpec((None, d, slab), lambda i, r, j: (0, 0, done // slab + step(i, r, j))),
                  pl.BlockSpec((1, slab), lambda i, r, j: (0, done // slab + step(i, r, j)))],
        out_specs=[pl.BlockSpec((tm, tn), lambda i, r, j: (i * nr + r, j)),
                   pl.BlockSpec((tm, d), lambda i, r, j: (i * nr + r, 0)),
                   pl.BlockSpec((c.shape[0], slab), lambda i, r, j: (0, step(i, r, j)))],
        out_shape=[jax.ShapeDtypeStruct((b * s, ncol * tn), BF16), jax.ShapeDtypeStruct((b * s, d), BF16),
                   jax.ShapeDtypeStruct((c.shape[0], w_ada.shape[2] - done), F32)],
        scratch_shapes=[pltpu.VMEM((ncol, d, tn), BF16)],
        compiler_params=pltpu.CompilerParams(dimension_semantics=("arbitrary", "arbitrary", "arbitrary"),
                                             vmem_limit_bytes=VMEM_LIMIT),
        name="first",
    )(x, mod_a, mod_a, g, w, c, w_ada, b_ada)


IN_TILES = ((2, "silu"), (5, "silu"), (6, "gate"), (7, "gate"), (8, "gate"), (9, "gate"), (4, "none"), (3, "logf"))
Z_Q, Z_OG, Z_GATE, Z_IV = 0, 1, 2, 6


def _inproj_kernel(h_ref, w_ref, bg_ref, lb_ref, *rest, n_cast):
    src, (z_ref, lf_ref), dst, wb_ref = rest[:n_cast], rest[n_cast:n_cast + 2], rest[n_cast + 2:-1], rest[-1]
    j = pl.program_id(0)

    @pl.when(pl.program_id(1) == 0)
    def _():
        wb_ref[...] = w_ref[...].astype(BF16)

    def tile(act):
        for s_ref, d_ref in zip(src, dst):
            d_ref[...] = s_ref[...].astype(BF16)
        acc = _dot(h_ref[...], wb_ref[...])
        if act == "silu":
            hx = 0.5 * acc
            z_ref[...] = (hx * jnp.tanh(hx) + hx).astype(BF16)
        elif act == "gate":
            z_ref[...] = (0.5 * jnp.tanh(0.5 * (acc + bg_ref[...])) + 0.5).astype(BF16)
        elif act == "none":
            z_ref[...] = acc.astype(BF16)
        else:
            lb = jax.nn.softmax(lb_ref[...], axis=0)[0:1, :]
            lf_ref[...] = jnp.log((0.5 + 0.5 * lb) + (0.5 - 0.5 * lb) * jnp.tanh(0.5 * acc))

    acts = [a for _, a in IN_TILES]
    for act in dict.fromkeys(acts):
        first = acts.index(act)
        last = len(acts) - 1 - acts[::-1].index(act)
        pl.when((j >= first) & (j <= last))(functools.partial(tile, act))


def _inproj(h, w, b_gate, hg_lb, casts, tm=1024, tn=1024):
    m, d = h.shape
    ni, nj = m // tm, len(IN_TILES)
    acts = [a for _, a in IN_TILES]
    assert acts[-1] == "logf" and acts.count("logf") == 1
    gate0 = acts.index("gate")
    ngate = acts.count("gate")

    def wcol(j):
        col = j
        for k, (c, _) in enumerate(IN_TILES):
            col = jnp.where(j == k, c, col)
        return col

    last = nj - 1
    in_specs = [pl.BlockSpec((tm, d), lambda j, i: (i, 0)),
                pl.BlockSpec((None, d, tn), lambda j, i: (0, 0, wcol(j))),
                pl.BlockSpec((1, tn), lambda j, i: (0, jnp.clip(j - gate0, 0, ngate - 1))),
                pl.BlockSpec(hg_lb.shape, lambda j, i: (0, 0))]
    out_specs = [pl.BlockSpec((tm, tn), lambda j, i: (jnp.where(j < last, i, ni - 1), jnp.minimum(j, last - 1))),
                 pl.BlockSpec((tm, tn), lambda j, i: (jnp.where(j < last, 0, i), 0))]
    out_shape = [jax.ShapeDtypeStruct((m, (nj - 1) * tn), BF16), jax.ShapeDtypeStruct((m, tn), F32)]
    for a in casts:
        _, n, wd = a.shape
        slab = next(sl for sl in range(16, n + 1, 16) if n % sl == 0 and n // sl <= nj * ni)
        blk = lambda j, i, nb=n // slab: jnp.minimum(j * ni + i, nb - 1)
        in_specs.append(pl.BlockSpec((None, slab, wd), lambda j, i, blk=blk: (0, blk(j, i), 0)))
        out_specs.append(pl.BlockSpec((slab, wd), lambda j, i, blk=blk: (blk(j, i), 0)))
        out_shape.append(jax.ShapeDtypeStruct((n, wd), BF16))
    return pl.pallas_call(
        functools.partial(_inproj_kernel, n_cast=len(casts)),
        grid=(nj, ni),
        in_specs=in_specs,
        out_specs=out_specs,
        out_shape=out_shape,
        scratch_shapes=[pltpu.VMEM((d, tn), BF16)],
        compiler_params=pltpu.CompilerParams(dimension_semantics=("arbitrary", "arbitrary"),
                                             vmem_limit_bytes=VMEM_LIMIT),
        name="inproj",
    )(h, w, b_gate, hg_lb, *casts)


def _level_masks():
    t = np.arange(CHUNK)[:, None]
    s = np.arange(CHUNK)[None, :]
    out = [(t // (2 * m) == s // (2 * m)) & ((t // m) % 2 == 1) & ((s // m) % 2 == 0) for m in LEVELS]
    return np.stack(out).astype(np.float32)


def _decay_matrix():
    t = np.arange(CHUNK)[:, None]
    j = np.arange(CHUNK)[None, :]
    blocks = [j <= t]
    for m in MATMUL_LEVELS:
        r = (t // (2 * m)) * (2 * m) + m - 1
        blocks.append(np.where((t // m) % 2 == 1, (j > r) & (j <= t), (j > t) & (j <= r)))
    blocks.append(j > t)
    w = np.concatenate(blocks, axis=0).astype(np.float32)
    return np.concatenate([w, w], axis=1)


def _mixer_kernel(u_ref, v_ref, q_ref, og_ref, lf_ref, iv_ref, lng_ref, lnb_ref, ws_ref, bs_ref, ng_ref,
                  cw_ref, msk_ref, ya_ref, yb_ref, st_ref):
    rows = u_ref.shape[0]
    nchunk = rows // CHUNK
    ngroup = u_ref.shape[1] // GROUP

    @pl.when(pl.program_id(1) == 0)
    def _():
        st_ref[...] = jnp.zeros_like(st_ref)

    v = v_ref[...].astype(F32)
    mu = jnp.mean(v, axis=-1, keepdims=True)
    vc = v - mu
    var = jnp.mean(vc * vc, axis=-1, keepdims=True)
    vn = (vc * lax.rsqrt(var + EPS) * lng_ref[...] + lnb_ref[...]).astype(BF16)
    tri = lax.broadcasted_iota(jnp.int32, (CHUNK, CHUNK), 0) >= lax.broadcasted_iota(jnp.int32, (CHUNK, CHUNK), 1)
    bst = bs_ref[...].T
    for g in range(ngroup):
        cols = slice(g * GROUP, (g + 1) * GROUP)
        w = jnp.where(tri, ws_ref[g], 0.0).astype(BF16)
        rhs = jnp.concatenate([vn[c * CHUNK:(c + 1) * CHUNK, cols] for c in range(nchunk)], axis=1)
        sg = _dot(w, rhs) + bst[:, g:g + 1]
        for c in range(nchunk):
            r = slice(c * CHUNK, (c + 1) * CHUNK)
            ya_ref[r, cols] = (u_ref[r, cols].astype(F32) * sg[:, c * CHUNK:(c + 1) * CHUNK]).astype(BF16)

    nlev = len(LEVELS)
    odd = lax.broadcasted_iota(jnp.int32, (CHUNK, GROUP), 0) % 2 == 1
    for c in range(nchunk):
        r = slice(c * CHUNK, (c + 1) * CHUNK)
        lf2 = lf_ref[r, :] * LOG2E
        ex = _dot(cw_ref[...], jnp.concatenate(_split_bf16(lf2, 2), axis=0))
        for h0 in range(0, ngroup, HEADS_PER_PASS):
            hs = range(h0, h0 + HEADS_PER_PASS)
            cs = {h: slice(h * GROUP, (h + 1) * GROUP) for h in hs}
            blk = lambda h, i: ex[i * CHUNK:(i + 1) * CHUNK, cs[h]]
            nmm = len(MATMUL_LEVELS)

            def level_decay(h, m):
                if m in MATMUL_LEVELS:
                    return jnp.exp2(blk(h, 1 + MATMUL_LEVELS.index(m))).astype(BF16)
                b3 = blk(h, 0).reshape(CHUNK // (2 * m), 2 * m, GROUP)
                dl = (b3 - b3[:, m - 1:m, :]).reshape(CHUNK, GROUP)
                return jnp.exp2(-jnp.abs(dl)).astype(BF16)

            qb = {h: q_ref[r, cs[h]] for h in hs}
            kb = {h: (1.0 - jnp.exp2(lf2[:, cs[h]])).astype(BF16) for h in hs}
            st = {h: st_ref[h] for h in hs}
            o = {h: _dot_nt(qb[h] * jnp.exp2(blk(h, 0)).astype(BF16), st[h].astype(BF16)) for h in hs}
            a = {}
            for li in range(nlev):
                for h in hs:
                    e = level_decay(h, LEVELS[li])
                    p = msk_ref[li] * _dot_nt(qb[h] * e, kb[h] * e)
                    a[h] = p if li == 0 else a[h] + p
            for h in hs:
                iv = iv_ref[r, cs[h]]
                ivf = iv.astype(F32)
                qf = qb[h].astype(F32)
                fh = jnp.exp2(lf2[:, cs[h]])
                kf = 1.0 - fh
                c0 = jnp.sum(qf * kf, axis=-1, keepdims=True)
                c1 = jnp.sum(jnp.where(odd, qf * fh * pltpu.roll(kf, 1, 0), 0.0), axis=-1, keepdims=True)
                oh = o[h] + _dot(a[h].astype(BF16), iv) + c0 * ivf + c1 * pltpu.roll(ivf, 1, 0)
                b_last = ex[CHUNK - 1:CHUNK, cs[h]]
                st_ref[h] = st[h] * jnp.exp2(b_last) + _dot_tn(iv, kb[h] * jnp.exp2(blk(h, 1 + nmm)).astype(BF16))
                y = _rms(oh) * ng_ref[:, cs[h]] * og_ref[r, cs[h]].astype(F32)
                yb_ref[r, cs[h]] = y.astype(BF16)


def _mixer(zg, z, lf, ln_g, ln_b, ws, bs, norm_g, batch, rows=256):
    m, width = lf.shape
    nr = m // batch // rows
    ngroup = width // GROUP
    cw = jnp.asarray(_decay_matrix(), BF16)
    msk = jnp.asarray(_level_masks(), F32)
    row_blk = lambda col: pl.BlockSpec((rows, width), lambda b, r: (b * nr + r, col))
    full = lambda a: pl.BlockSpec(a.shape, lambda b, r: (0,) * a.ndim)
    return pl.pallas_call(
        _mixer_kernel,
        grid=(batch, nr),
        in_specs=[row_blk(0), row_blk(1), row_blk(Z_Q), row_blk(Z_OG), row_blk(0), row_blk(Z_IV),
                  full(ln_g), full(ln_b), full(ws), full(bs), full(norm_g), full(cw), full(msk)],
        out_specs=[row_blk(0), row_blk(0)],
        out_shape=[jax.ShapeDtypeStruct((m, width), BF16), jax.ShapeDtypeStruct((m, width), BF16)],
        scratch_shapes=[pltpu.VMEM((ngroup, GROUP, GROUP), F32)],
        compiler_params=pltpu.CompilerParams(dimension_semantics=("arbitrary", "arbitrary"),
                                             vmem_limit_bytes=VMEM_LIMIT),
        name="mixer",
    )(zg, zg, z, z, lf, z, ln_g, ln_b, ws, bs, norm_g, cw, msk)


def _merge_kernel(ya_ref, yb_ref, ga_ref, gb_ref, x_ref, gt_ref, sh_ref, sc_ref, wa_ref, wb_ref, wo_ref, g_ref, x1_ref,
                  h2_ref):
    tm = ya_ref.shape[0]
    row = pl.ds(pl.program_id(0), 1)
    scale = g_ref[...] * (1.0 + sc_ref[row, :])
    gate, shift = gt_ref[row, :], sh_ref[row, :]
    for p in range(MERGE_PARTS):
        r = slice(p * tm // MERGE_PARTS, (p + 1) * tm // MERGE_PARTS)
        pa = _dot(ya_ref[r, :], wa_ref[...])
        pb = _dot(yb_ref[r, :], wb_ref[...])
        y = (ga_ref[r, :].astype(F32) * pa + gb_ref[r, :].astype(F32) * pb).astype(BF16)
        x1 = x_ref[0, r, :] + gate * _dot(y, wo_ref[...])
        x1_ref[0, r, :] = x1
        h2_ref[0, r, :] = (_rms(x1) * scale + shift).astype(BF16)


def _merge(ya, yb, z, x, mod_b, wa, wb, wo, g, tm=512):
    b, s, d = x.shape
    nr = s // tm
    width = ya.shape[1]
    resident = lambda a: pl.BlockSpec(a.shape, lambda i, r: (0,) * a.ndim, pipeline_mode=pl.Buffered(1))
    return pl.pallas_call(
        _merge_kernel,
        grid=(b, nr),
        in_specs=[pl.BlockSpec((tm, width), lambda i, r: (i * nr + r, 0)),
                  pl.BlockSpec((tm, width), lambda i, r: (i * nr + r, 0)),
                  pl.BlockSpec((tm, d), lambda i, r: (i * nr + r, Z_GATE * width // d)),
                  pl.BlockSpec((tm, d), lambda i, r: (i * nr + r, Z_GATE * width // d + 1)),
                  pl.BlockSpec((1, tm, d), lambda i, r: (i, r, 0)),
                  pl.BlockSpec((b, d), lambda i, r: (0, GT1)),
                  pl.BlockSpec((b, d), lambda i, r: (0, SH2)),
                  pl.BlockSpec((b, d), lambda i, r: (0, SC2)),
                  resident(wa), resident(wb), resident(wo),
                  pl.BlockSpec((1, d), lambda i, r: (0, 0))],
        out_specs=[pl.BlockSpec((1, tm, d), lambda i, r: (i, r, 0)),
                   pl.BlockSpec((1, tm, d), lambda i, r: (i, r, 0))],
        out_shape=[jax.ShapeDtypeStruct((b, s, d), F32), jax.ShapeDtypeStruct((b, s, d), BF16)],
        compiler_params=pltpu.CompilerParams(dimension_semantics=("arbitrary", "arbitrary"),
                                             vmem_limit_bytes=VMEM_LIMIT),
        name="merge",
    )(ya, yb, z, z, x, mod_b, mod_b, mod_b, wa, wb, wo, g)


def _ffn_kernel(h_ref, wa_ref, wu_ref, wo_ref, x1_ref, mod_ref, g_ref, o_ref, acc_ref, *, nt):
    t = pl.program_id(2)

    def hidden_tile():
        h = h_ref[0]
        a = _dot(h, wa_ref[...])
        up = _dot(h, wu_ref[...])
        ha = 0.5 * a
        return _dot(((ha * jnp.tanh(ha) + ha) * up).astype(BF16), wo_ref[...])

    @pl.when(t == 0)
    def _():
        acc_ref[...] = hidden_tile()

    @pl.when((t > 0) & (t < nt))
    def _():
        acc_ref[...] += hidden_tile()

    @pl.when(t >= nt)
    def _():
        half = o_ref.shape[1]
        base = (t - nt) * half
        gate = mod_ref[pl.ds(pl.program_id(0), 1), :]
        g = g_ref[...]

        def body(i, carry):
            rows = pl.ds(pl.multiple_of(i * ROW_GROUP, ROW_GROUP), ROW_GROUP)
            arows = pl.ds(pl.multiple_of(base + i * ROW_GROUP, ROW_GROUP), ROW_GROUP)
            o_ref[0, rows, :] = _rms(x1_ref[0, rows, :] + gate * acc_ref[arows, :]) * g
            return carry

        lax.fori_loop(0, half // ROW_GROUP, body, 0, unroll=8)


def _ffn(h2, w_in, w_out, x1, mod_b, g, tm=1024, th=512):
    b, s, d = x1.shape
    hidden = w_out.shape[0]
    nt = hidden // th
    half = tm // 2
    nr = s // tm
    wt = lambda t: jnp.where(t < nt, t, 0)
    fin = lambda r, t: 2 * r + jnp.clip(t - nt, 0, 1)

    def h2_blk(i, r, t):
        lin = jnp.minimum(i * nr + r + (t >= nt).astype(jnp.int32), b * nr - 1)
        return (lin // nr, lin % nr, 0)

    return pl.pallas_call(
        functools.partial(_ffn_kernel, nt=nt),
        grid=(b, s // tm, nt + 2),
        in_specs=[pl.BlockSpec((1, tm, d), h2_blk),
                  pl.BlockSpec((d, th), lambda i, r, t: (0, wt(t))),
                  pl.BlockSpec((d, th), lambda i, r, t: (0, nt + wt(t))),
                  pl.BlockSpec((th, d), lambda i, r, t: (wt(t), 0)),
                  pl.BlockSpec((1, half, d), lambda i, r, t: (i, fin(r, t), 0)),
                  pl.BlockSpec((b, d), lambda i, r, t: (0, GT2)),
                  pl.BlockSpec((1, d), lambda i, r, t: (0, 0))],
        out_specs=pl.BlockSpec((1, half, d), lambda i, r, t: (i, fin(r, t), 0)),
        out_shape=jax.ShapeDtypeStruct((b, s, d), F32),
        scratch_shapes=[pltpu.VMEM((tm, d), F32)],
        compiler_params=pltpu.CompilerParams(dimension_semantics=("arbitrary", "arbitrary", "arbitrary"),
                                             vmem_limit_bytes=VMEM_LIMIT),
        name="ffn",
    )(h2, w_in, w_in, w_out, x1, mod_b, g)


def kernel(x, c, w_ada, b_ada, norm1_g, w_in, b_gate, gmlp_ln_g, gmlp_ln_b, gmlp_ws, gmlp_bs, hg_lb, hg_norm_g,
           w_branch_gmlp, w_branch_hg, w_out, norm2_g, w_ffn_in, w_ffn_out, final_norm_g):
    batch, seq, d = x.shape
    depth = w_ada.shape[0]
    width = w_branch_gmlp.shape[1]
    assert depth == 1 and width == 8 * GROUP and gmlp_ws.shape[2] == CHUNK and seq % 256 == 0
    assert w_in.shape[2] == 6 * width + 2 * d and hg_lb.shape[0] == depth + 1
    tn = 1024
    assert width == tn and d == 2 * tn

    mod_a = _ada(c, w_ada, b_ada, 2 * d)
    zg, h1, mod_b = _first(x, mod_a, norm1_g, w_in, c, w_ada, b_ada)
    z, lf, w_a, w_b, w_o, w_fi, w_fo = _inproj(
        h1, w_in, b_gate, hg_lb, casts=(w_branch_gmlp, w_branch_hg, w_out, w_ffn_in, w_ffn_out))
    ya, yb = _mixer(zg, z, lf, gmlp_ln_g, gmlp_ln_b, gmlp_ws[0], gmlp_bs[0], hg_norm_g, batch)

    x1, h2 = _merge(ya, yb, z, x, mod_b, w_a, w_b, w_o, norm2_g)
    return _ffn(h2, w_fi, w_fo, x1, mod_b, final_norm_g.reshape(1, d))
```

```python
import functools

import numpy as np
import jax
import jax.numpy as jnp
from jax import lax
from jax.experimental import pallas as pl
from jax.experimental.pallas import tpu as pltpu

F32 = jnp.float32
BF16 = jnp.bfloat16
EPS = 1e-6
LOG2E = 1.4426950408889634
GELU_C1 = 0.7978845608028654
GELU_C3 = 0.044715 * GELU_C1

SH1, SC1 = 0, 1
GT1, SH2, SC2, GT2 = 0, 1, 2, 3

GROUP = 128
CHUNK = 128
MERGE_PARTS = 2
ROW_GROUP = 16
HEADS_PER_PASS = 8
LEVELS = (64, 32, 16, 8, 4, 2)
MATMUL_LEVELS = (4, 2)
VMEM_LIMIT = 56 * 1024 * 1024


def _dot(a, b):
    return jnp.dot(a, b, preferred_element_type=F32)


def _dot_nt(a, b):
    return lax.dot_general(a, b, (((1,), (1,)), ((), ())), preferred_element_type=F32)


def _dot_tn(a, b):
    return lax.dot_general(a, b, (((0,), (0,)), ((), ())), preferred_element_type=F32)


def _rms(x):
    return x * lax.rsqrt(jnp.mean(x * x, axis=-1, keepdims=True) + EPS)


def _split_bf16(x, parts):
    out = []
    for _ in range(parts - 1):
        p = x.astype(BF16)
        out.append(p)
        x = x - p.astype(F32)
    out.append(x.astype(BF16))
    return out


def _ada_tile(c_ref, w_ref, b_ref):
    rows = c_ref.shape[0]
    ca = jnp.concatenate(_split_bf16(jax.nn.silu(c_ref[...]), 2), axis=0)
    acc = _dot(ca, w_ref[...].astype(BF16))
    return acc[:rows] + acc[rows:] + b_ref[...]


def _ada_kernel(c_ref, w_ref, b_ref, o_ref):
    o_ref[...] = _ada_tile(c_ref, w_ref, b_ref)


def _ada(c, w_ada, b_ada, ncols, tn=1024):
    _, d, _ = w_ada.shape
    rows = c.shape[0]
    return pl.pallas_call(
        _ada_kernel,
        grid=(ncols // tn,),
        in_specs=[pl.BlockSpec((rows, d), lambda j: (0, 0)),
                  pl.BlockSpec((None, d, tn), lambda j: (0, 0, j)),
                  pl.BlockSpec((1, tn), lambda j: (0, j))],
        out_specs=pl.BlockSpec((rows, tn), lambda j: (0, j)),
        out_shape=jax.ShapeDtypeStruct((rows, ncols), F32),
        compiler_params=pltpu.CompilerParams(dimension_semantics=("arbitrary",), vmem_limit_bytes=VMEM_LIMIT),
        name="ada",
    )(c, w_ada, b_ada)


def _gelu(x):
    hx = 0.5 * x
    return hx * jnp.tanh(x * (GELU_C1 + GELU_C3 * (x * x))) + hx


def _first_kernel(x_ref, sh_ref, sc_ref, g_ref, w_ref, c_ref, wada1_ref, wada2_ref, bada1_ref, bada2_ref,
                  zg_ref, h_ref, modb_ref, wb_ref, macc_ref, *, ncol):
    j = pl.program_id(2)
    first = (pl.program_id(0) == 0) & (pl.program_id(1) == 0)
    tn = zg_ref.shape[1]
    nb = c_ref.shape[0]
    half = wada1_ref.shape[1]

    @pl.when(first & (j == 0))
    def _():
        macc_ref[:, :half] = jnp.broadcast_to(bada1_ref[...], (nb, half))
        macc_ref[:, half:] = jnp.broadcast_to(bada2_ref[...], (nb, half))

    def tile(jj):
        if jj == 0:
            ca = jnp.concatenate(_split_bf16(jax.nn.silu(c_ref[...]), 2), axis=0)
            for k, wada_ref in enumerate((wada1_ref, wada2_ref)):
                part = _dot(ca, wada_ref[...].astype(BF16))
                macc_ref[:, k * half:(k + 1) * half] += part[:nb] + part[nb:]
            modb_ref[...] = macc_ref[...]
            row = pl.ds(pl.program_id(0), 1)
            scale = g_ref[...] * (1.0 + sc_ref[row, :])
            h_ref[...] = (_rms(x_ref[0]) * scale + sh_ref[row, :]).astype(BF16)
        zg_ref[...] = _gelu(_dot(h_ref[...], wb_ref[jj])).astype(BF16)

    for jj in range(ncol):
        @pl.when(first & (j == jj))
        def _(jj=jj):
            wb_ref[jj] = w_ref[:, jj * tn:(jj + 1) * tn].astype(BF16)

        pl.when(j == jj)(functools.partial(tile, jj))


def _first(x, mod_a, g, w, c, w_ada, b_ada, ncol=2, tm=512, tn=1024):
    b, s, d = x.shape
    nr = s // tm
    done = mod_a.shape[1]
    rest = w_ada.shape[2] - done
    half = rest // 2
    rs = d // (b * nr)
    assert rs * b * nr == d and rs % 128 == 0 and done == half
    tile_id = lambda i, r, j: i * nr + r
    return pl.pallas_call(
        functools.partial(_first_kernel, ncol=ncol),
        grid=(b, nr, ncol),
        in_specs=[pl.BlockSpec((1, tm, d), lambda i, r, j: (i, r, 0)),
                  pl.BlockSpec((b, d), lambda i, r, j: (0, SH1)),
                  pl.BlockSpec((b, d), lambda i, r, j: (0, SC1)),
                  pl.BlockSpec((1, d), lambda i, r, j: (0, 0)),
                  pl.BlockSpec((None, d, ncol * tn), lambda i, r, j: (0, 0, 0), pipeline_mode=pl.Buffered(1)),
                  pl.BlockSpec((b, rs), lambda i, r, j: (0, tile_id(i, r, j))),
                  pl.BlockSpec((None, rs, half), lambda i, r, j: (0, tile_id(i, r, j), 1)),
                  pl.BlockSpec((None, rs, half), lambda i, r, j: (0, tile_id(i, r, j), 2)),
                  pl.BlockSpec((1, half), lambda i, r, j: (0, 1)),
                  pl.BlockSpec((1, half), lambda i, r, j: (0, 2))],
        out_specs=[pl.BlockSpec((tm, tn), lambda i, r, j: (i * nr + r, j)),
                   pl.BlockSpec((tm, d), lambda i, r, j: (i * nr + r, 0)),
                   pl.BlockSpec((b, rest), lambda i, r, j: (0, 0))],
        out_shape=[jax.ShapeDtypeStruct((b * s, ncol * tn), BF16), jax.ShapeDtypeStruct((b * s, d), BF16),
                   jax.ShapeDtypeStruct((b, rest), F32)],
        scratch_shapes=[pltpu.VMEM((ncol, d, tn), BF16), pltpu.VMEM((b, rest), F32)],
        compiler_params=pltpu.CompilerParams(dimension_semantics=("arbitrary", "arbitrary", "arbitrary"),
                                             vmem_limit_bytes=VMEM_LIMIT),
        name="first",
    )(x, mod_a, mod_a, g, w, c, w_ada, w_ada, b_ada, b_ada)


IN_TILES = ((2, "silu"), (5, "silu"), (6, "gate"), (7, "gate"), (8, "gate"), (9, "gate"), (4, "none"), (3, "logf"))
Z_Q, Z_OG, Z_GATE, Z_IV = 0, 1, 2, 6


def _inproj_kernel(h_ref, w_ref, bg_ref, lb_ref, z_ref, lf_ref, wb_ref):
    j = pl.program_id(0)

    @pl.when(pl.program_id(1) == 0)
    def _():
        wb_ref[...] = w_ref[...].astype(BF16)

    def tile(act):
        acc = _dot(h_ref[...], wb_ref[...])
        if act == "silu":
            hx = 0.5 * acc
            z_ref[...] = (hx * jnp.tanh(hx) + hx).astype(BF16)
        elif act == "gate":
            z_ref[...] = (0.5 * jnp.tanh(0.5 * (acc + bg_ref[...])) + 0.5).astype(BF16)
        elif act == "none":
            z_ref[...] = acc.astype(BF16)
        else:
            lb = jax.nn.softmax(lb_ref[...], axis=0)[0:1, :]
            lf_ref[...] = jnp.log((0.5 + 0.5 * lb) + (0.5 - 0.5 * lb) * jnp.tanh(0.5 * acc))

    acts = [a for _, a in IN_TILES]
    for act in dict.fromkeys(acts):
        first = acts.index(act)
        last = len(acts) - 1 - acts[::-1].index(act)
        pl.when((j >= first) & (j <= last))(functools.partial(tile, act))


def _inproj(h, w, b_gate, hg_lb, tm=1024, tn=1024):
    m, d = h.shape
    ni, nj = m // tm, len(IN_TILES)
    acts = [a for _, a in IN_TILES]
    assert acts[-1] == "logf" and acts.count("logf") == 1
    gate0 = acts.index("gate")
    ngate = acts.count("gate")

    def wcol(j):
        col = j
        for k, (c, _) in enumerate(IN_TILES):
            col = jnp.where(j == k, c, col)
        return col

    last = nj - 1
    return pl.pallas_call(
        _inproj_kernel,
        grid=(nj, ni),
        in_specs=[pl.BlockSpec((tm, d), lambda j, i: (i, 0)),
                  pl.BlockSpec((None, d, tn), lambda j, i: (0, 0, wcol(j))),
                  pl.BlockSpec((1, tn), lambda j, i: (0, jnp.clip(j - gate0, 0, ngate - 1))),
                  pl.BlockSpec(hg_lb.shape, lambda j, i: (0, 0))],
        out_specs=[pl.BlockSpec((tm, tn), lambda j, i: (jnp.where(j < last, i, ni - 1), jnp.minimum(j, last - 1))),
                   pl.BlockSpec((tm, tn), lambda j, i: (jnp.where(j < last, 0, i), 0))],
        out_shape=[jax.ShapeDtypeStruct((m, (nj - 1) * tn), BF16), jax.ShapeDtypeStruct((m, tn), F32)],
        scratch_shapes=[pltpu.VMEM((d, tn), BF16)],
        compiler_params=pltpu.CompilerParams(dimension_semantics=("arbitrary", "arbitrary"),
                                             vmem_limit_bytes=VMEM_LIMIT),
        name="inproj",
    )(h, w, b_gate, hg_lb)


def _level_masks():
    t = np.arange(CHUNK)[:, None]
    s = np.arange(CHUNK)[None, :]
    out = [(t // (2 * m) == s // (2 * m)) & ((t // m) % 2 == 1) & ((s // m) % 2 == 0) for m in LEVELS]
    return np.stack(out).astype(np.float32)


def _decay_matrix():
    t = np.arange(CHUNK)[:, None]
    j = np.arange(CHUNK)[None, :]
    blocks = [j <= t]
    for m in MATMUL_LEVELS:
        r = (t // (2 * m)) * (2 * m) + m - 1
        blocks.append(np.where((t // m) % 2 == 1, (j > r) & (j <= t), (j > t) & (j <= r)))
    blocks.append(j > t)
    w = np.concatenate(blocks, axis=0).astype(np.float32)
    return np.concatenate([w, w], axis=1)


def _mixer_kernel(u_ref, v_ref, q_ref, og_ref, lf_ref, iv_ref, lng_ref, lnb_ref, ws_ref, bs_ref, ng_ref,
                  cw_ref, msk_ref, *rest, n_cast):
    src, (ya_ref, yb_ref), dst, st_ref = rest[:n_cast], rest[n_cast:n_cast + 2], rest[n_cast + 2:-1], rest[-1]
    rows = u_ref.shape[0]
    nchunk = rows // CHUNK
    ngroup = u_ref.shape[1] // GROUP

    @pl.when(pl.program_id(1) == 0)
    def _():
        st_ref[...] = jnp.zeros_like(st_ref)

    for s_ref, d_ref in zip(src, dst):
        d_ref[...] = s_ref[...].astype(BF16)

    v = v_ref[...].astype(F32)
    mu = jnp.mean(v, axis=-1, keepdims=True)
    vc = v - mu
    var = jnp.mean(vc * vc, axis=-1, keepdims=True)
    vn = (vc * lax.rsqrt(var + EPS) * lng_ref[...] + lnb_ref[...]).astype(BF16)
    tri = lax.broadcasted_iota(jnp.int32, (CHUNK, CHUNK), 0) >= lax.broadcasted_iota(jnp.int32, (CHUNK, CHUNK), 1)
    bst = bs_ref[...].T
    for g in range(ngroup):
        cols = slice(g * GROUP, (g + 1) * GROUP)
        w = jnp.where(tri, ws_ref[g], 0.0).astype(BF16)
        rhs = jnp.concatenate([vn[c * CHUNK:(c + 1) * CHUNK, cols] for c in range(nchunk)], axis=1)
        sg = _dot(w, rhs) + bst[:, g:g + 1]
        for c in range(nchunk):
            r = slice(c * CHUNK, (c + 1) * CHUNK)
            ya_ref[r, cols] = (u_ref[r, cols].astype(F32) * sg[:, c * CHUNK:(c + 1) * CHUNK]).astype(BF16)

    nlev = len(LEVELS)
    odd = lax.broadcasted_iota(jnp.int32, (CHUNK, GROUP), 0) % 2 == 1
    for c in range(nchunk):
        r = slice(c * CHUNK, (c + 1) * CHUNK)
        lf2 = lf_ref[r, :] * LOG2E
        ex = _dot(cw_ref[...], jnp.concatenate(_split_bf16(lf2, 2), axis=0))
        for h0 in range(0, ngroup, HEADS_PER_PASS):
            hs = range(h0, h0 + HEADS_PER_PASS)
            cs = {h: slice(h * GROUP, (h + 1) * GROUP) for h in hs}
            blk = lambda h, i: ex[i * CHUNK:(i + 1) * CHUNK, cs[h]]
            nmm = len(MATMUL_LEVELS)

            def level_decay(h, m):
                if m in MATMUL_LEVELS:
                    return jnp.exp2(blk(h, 1 + MATMUL_LEVELS.index(m))).astype(BF16)
                b3 = blk(h, 0).reshape(CHUNK // (2 * m), 2 * m, GROUP)
                dl = (b3 - b3[:, m - 1:m, :]).reshape(CHUNK, GROUP)
                return jnp.exp2(-jnp.abs(dl)).astype(BF16)

            qb = {h: q_ref[r, cs[h]] for h in hs}
            kb = {h: (1.0 - jnp.exp2(lf2[:, cs[h]])).astype(BF16) for h in hs}
            st = {h: st_ref[h] for h in hs}
            o = {h: _dot_nt(qb[h] * jnp.exp2(blk(h, 0)).astype(BF16), st[h].astype(BF16)) for h in hs}
            a = {}
            for li in range(nlev):
                for h in hs:
                    e = level_decay(h, LEVELS[li])
                    p = msk_ref[li] * _dot_nt(qb[h] * e, kb[h] * e)
                    a[h] = p if li == 0 else a[h] + p
            for h in hs:
                iv = iv_ref[r, cs[h]]
                ivf = iv.astype(F32)
                qf = qb[h].astype(F32)
                fh = jnp.exp2(lf2[:, cs[h]])
                kf = 1.0 - fh
                c0 = jnp.sum(qf * kf, axis=-1, keepdims=True)
                c1 = jnp.sum(jnp.where(odd, qf * fh * pltpu.roll(kf, 1, 0), 0.0), axis=-1, keepdims=True)
                oh = o[h] + _dot(a[h].astype(BF16), iv) + c0 * ivf + c1 * pltpu.roll(ivf, 1, 0)
                b_last = ex[CHUNK - 1:CHUNK, cs[h]]
                st_ref[h] = st[h] * jnp.exp2(b_last) + _dot_tn(iv, kb[h] * jnp.exp2(blk(h, 1 + nmm)).astype(BF16))
                y = _rms(oh) * ng_ref[:, cs[h]] * og_ref[r, cs[h]].astype(F32)
                yb_ref[r, cs[h]] = y.astype(BF16)


def _mixer(zg, z, lf, ln_g, ln_b, ws, bs, norm_g, batch, casts, rows=256):
    m, width = lf.shape
    nr = m // batch // rows
    nsteps = batch * nr
    ngroup = width // GROUP
    cw = jnp.asarray(_decay_matrix(), BF16)
    msk = jnp.asarray(_level_masks(), F32)
    row_blk = lambda col: pl.BlockSpec((rows, width), lambda b, r: (b * nr + r, col))
    full = lambda a: pl.BlockSpec(a.shape, lambda b, r: (0,) * a.ndim)
    in_specs = [row_blk(0), row_blk(1), row_blk(Z_Q), row_blk(Z_OG), row_blk(0), row_blk(Z_IV),
                full(ln_g), full(ln_b), full(ws), full(bs), full(norm_g), full(cw), full(msk)]
    out_specs = [row_blk(0), row_blk(0)]
    out_shape = [jax.ShapeDtypeStruct((m, width), BF16), jax.ShapeDtypeStruct((m, width), BF16)]
    for a in casts:
        _, n, w = a.shape
        slab = n // nsteps
        assert slab * nsteps == n and slab % 16 == 0
        in_specs.append(pl.BlockSpec((None, slab, w), lambda b, r: (0, b * nr + r, 0)))
        out_specs.append(pl.BlockSpec((slab, w), lambda b, r: (b * nr + r, 0)))
        out_shape.append(jax.ShapeDtypeStruct((n, w), BF16))
    return pl.pallas_call(
        functools.partial(_mixer_kernel, n_cast=len(casts)),
        grid=(batch, nr),
        in_specs=in_specs,
        out_specs=out_specs,
        out_shape=out_shape,
        scratch_shapes=[pltpu.VMEM((ngroup, GROUP, GROUP), F32)],
        compiler_params=pltpu.CompilerParams(dimension_semantics=("arbitrary", "arbitrary"),
                                             vmem_limit_bytes=VMEM_LIMIT),
        name="mixer",
    )(zg, zg, z, z, lf, z, ln_g, ln_b, ws, bs, norm_g, cw, msk, *casts)


def _merge_kernel(ya_ref, yb_ref, ga_ref, gb_ref, x_ref, gt_ref, sh_ref, sc_ref, wa_ref, wb_ref, wo_ref, g_ref, x1_ref,
                  h2_ref):
    tm = ya_ref.shape[0]
    row = pl.ds(pl.program_id(0), 1)
    scale = g_ref[...] * (1.0 + sc_ref[row, :])
    gate, shift = gt_ref[row, :], sh_ref[row, :]
    for p in range(MERGE_PARTS):
        r = slice(p * tm // MERGE_PARTS, (p + 1) * tm // MERGE_PARTS)
        pa = _dot(ya_ref[r, :], wa_ref[...])
        pb = _dot(yb_ref[r, :], wb_ref[...])
        y = (ga_ref[r, :].astype(F32) * pa + gb_ref[r, :].astype(F32) * pb).astype(BF16)
        x1 = x_ref[0, r, :] + gate * _dot(y, wo_ref[...])
        x1_ref[0, r, :] = x1
        h2_ref[0, r, :] = (_rms(x1) * scale + shift).astype(BF16)


def _merge(ya, yb, z, x, mod_b, wa, wb, wo, g, tm=512):
    b, s, d = x.shape
    nr = s // tm
    width = ya.shape[1]
    resident = lambda a: pl.BlockSpec(a.shape, lambda i, r: (0,) * a.ndim, pipeline_mode=pl.Buffered(1))
    return pl.pallas_call(
        _merge_kernel,
        grid=(b, nr),
        in_specs=[pl.BlockSpec((tm, width), lambda i, r: (i * nr + r, 0)),
                  pl.BlockSpec((tm, width), lambda i, r: (i * nr + r, 0)),
                  pl.BlockSpec((tm, d), lambda i, r: (i * nr + r, Z_GATE * width // d)),
                  pl.BlockSpec((tm, d), lambda i, r: (i * nr + r, Z_GATE * width // d + 1)),
                  pl.BlockSpec((1, tm, d), lambda i, r: (i, r, 0)),
                  pl.BlockSpec((b, d), lambda i, r: (0, GT1)),
                  pl.BlockSpec((b, d), lambda i, r: (0, SH2)),
                  pl.BlockSpec((b, d), lambda i, r: (0, SC2)),
                  resident(wa), resident(wb), resident(wo),
                  pl.BlockSpec((1, d), lambda i, r: (0, 0))],
        out_specs=[pl.BlockSpec((1, tm, d), lambda i, r: (i, r, 0)),
                   pl.BlockSpec((1, tm, d), lambda i, r: (i, r, 0))],
        out_shape=[jax.ShapeDtypeStruct((b, s, d), F32), jax.ShapeDtypeStruct((b, s, d), BF16)],
        compiler_params=pltpu.CompilerParams(dimension_semantics=("arbitrary", "arbitrary"),
                                             vmem_limit_bytes=VMEM_LIMIT),
        name="merge",
    )(ya, yb, z, z, x, mod_b, mod_b, mod_b, wa, wb, wo, g)


def _ffn_kernel(h_ref, wa_ref, wu_ref, wo_ref, x1_ref, mod_ref, g_ref, o_ref, acc_ref, *, nt):
    t = pl.program_id(2)

    def hidden_tile():
        h = h_ref[0]
        a = _dot(h, wa_ref[...])
        up = _dot(h, wu_ref[...])
        ha = 0.5 * a
        return _dot(((ha * jnp.tanh(ha) + ha) * up).astype(BF16), wo_ref[...])

    @pl.when(t == 0)
    def _():
        acc_ref[...] = hidden_tile()

    @pl.when((t > 0) & (t < nt))
    def _():
        acc_ref[...] += hidden_tile()

    @pl.when(t >= nt)
    def _():
        half = o_ref.shape[1]
        base = (t - nt) * half
        gate = mod_ref[pl.ds(pl.program_id(0), 1), :]
        g = g_ref[...]

        def body(i, carry):
            rows = pl.ds(pl.multiple_of(i * ROW_GROUP, ROW_GROUP), ROW_GROUP)
            arows = pl.ds(pl.multiple_of(base + i * ROW_GROUP, ROW_GROUP), ROW_GROUP)
            o_ref[0, rows, :] = _rms(x1_ref[0, rows, :] + gate * acc_ref[arows, :]) * g
            return carry

        lax.fori_loop(0, half // ROW_GROUP, body, 0, unroll=8)


def _ffn(h2, w_in, w_out, x1, mod_b, g, tm=1024, th=512):
    b, s, d = x1.shape
    hidden = w_out.shape[0]
    nt = hidden // th
    half = tm // 2
    nr = s // tm
    wt = lambda t: jnp.where(t < nt, t, 0)
    fin = lambda r, t: 2 * r + jnp.clip(t - nt, 0, 1)

    def h2_blk(i, r, t):
        lin = jnp.minimum(i * nr + r + (t >= nt).astype(jnp.int32), b * nr - 1)
        return (lin // nr, lin % nr, 0)

    return pl.pallas_call(
        functools.partial(_ffn_kernel, nt=nt),
        grid=(b, s // tm, nt + 2),
        in_specs=[pl.BlockSpec((1, tm, d), h2_blk),
                  pl.BlockSpec((d, th), lambda i, r, t: (0, wt(t))),
                  pl.BlockSpec((d, th), lambda i, r, t: (0, nt + wt(t))),
                  pl.BlockSpec((th, d), lambda i, r, t: (wt(t), 0)),
                  pl.BlockSpec((1, half, d), lambda i, r, t: (i, fin(r, t), 0)),
                  pl.BlockSpec((b, d), lambda i, r, t: (0, GT2)),
                  pl.BlockSpec((1, d), lambda i, r, t: (0, 0))],
        out_specs=pl.BlockSpec((1, half, d), lambda i, r, t: (i, fin(r, t), 0)),
        out_shape=jax.ShapeDtypeStruct((b, s, d), F32),
        scratch_shapes=[pltpu.VMEM((tm, d), F32)],
        compiler_params=pltpu.CompilerParams(dimension_semantics=("arbitrary", "arbitrary", "arbitrary"),
                                             vmem_limit_bytes=VMEM_LIMIT),
        name="ffn",
    )(h2, w_in, w_in, w_out, x1, mod_b, g)


def kernel(x, c, w_ada, b_ada, norm1_g, w_in, b_gate, gmlp_ln_g, gmlp_ln_b, gmlp_ws, gmlp_bs, hg_lb, hg_norm_g,
           w_branch_gmlp, w_branch_hg, w_out, norm2_g, w_ffn_in, w_ffn_out, final_norm_g):
    batch, seq, d = x.shape
    depth = w_ada.shape[0]
    width = w_branch_gmlp.shape[1]
    assert depth == 1 and width == 8 * GROUP and gmlp_ws.shape[2] == CHUNK and seq % 256 == 0
    assert w_in.shape[2] == 6 * width + 2 * d and hg_lb.shape[0] == depth + 1
    tn = 1024
    assert width == tn and d == 2 * tn

    mod_a = _ada(c, w_ada, b_ada, 2 * d)
    zg, h1, mod_b = _first(x, mod_a, norm1_g, w_in, c, w_ada, b_ada)
    z, lf = _inproj(h1, w_in, b_gate, hg_lb)

    ya, yb, w_a, w_b, w_o, w_fi, w_fo = _mixer(
        zg, z, lf, gmlp_ln_g, gmlp_ln_b, gmlp_ws[0], gmlp_bs[0], hg_norm_g, batch,
        casts=(w_branch_gmlp, w_branch_hg, w_out, w_ffn_in, w_ffn_out))

    x1, h2 = _merge(ya, yb, z, x, mod_b, w_a, w_b, w_o, norm2_g)
    return _ffn(h2, w_fi, w_fo, x1, mod_b, final_norm_g.reshape(1, d))
```

```python
import functools

import numpy as np
import jax
import jax.numpy as jnp
from jax import lax
from jax.experimental import pallas as pl
from jax.experimental.pallas import tpu as pltpu

F32 = jnp.float32
BF16 = jnp.bfloat16
EPS = 1e-6
LOG2E = 1.4426950408889634
GELU_C1 = 0.7978845608028654
GELU_C3 = 0.044715 * GELU_C1

SH1, SC1 = 0, 1
GT1, SH2, SC2, GT2 = 0, 1, 2, 3

GROUP = 128
CHUNK = 128
MERGE_PARTS = 2
ROW_GROUP = 16
HEADS_PER_PASS = 8
LEVELS = (64, 32, 16, 8, 4, 2)
MATMUL_LEVELS = (4, 2)
VMEM_LIMIT = 56 * 1024 * 1024


def _dot(a, b):
    return jnp.dot(a, b, preferred_element_type=F32)


def _dot_nt(a, b):
    return lax.dot_general(a, b, (((1,), (1,)), ((), ())), preferred_element_type=F32)


def _dot_tn(a, b):
    return lax.dot_general(a, b, (((0,), (0,)), ((), ())), preferred_element_type=F32)


def _rms(x):
    return x * lax.rsqrt(jnp.mean(x * x, axis=-1, keepdims=True) + EPS)


def _split_bf16(x, parts):
    out = []
    for _ in range(parts - 1):
        p = x.astype(BF16)
        out.append(p)
        x = x - p.astype(F32)
    out.append(x.astype(BF16))
    return out


def _ada_tile(c_ref, w_ref, b_ref):
    rows = c_ref.shape[0]
    ca = jnp.concatenate(_split_bf16(jax.nn.silu(c_ref[...]), 2), axis=0)
    acc = _dot(ca, w_ref[...].astype(BF16))
    return acc[:rows] + acc[rows:] + b_ref[...]


def _ada_kernel(c_ref, w_ref, b_ref, o_ref):
    o_ref[...] = _ada_tile(c_ref, w_ref, b_ref)


def _ada(c, w_ada, b_ada, ncols, tn=1024):
    _, d, _ = w_ada.shape
    rows = c.shape[0]
    return pl.pallas_call(
        _ada_kernel,
        grid=(ncols // tn,),
        in_specs=[pl.BlockSpec((rows, d), lambda j: (0, 0)),
                  pl.BlockSpec((None, d, tn), lambda j: (0, 0, j)),
                  pl.BlockSpec((1, tn), lambda j: (0, j))],
        out_specs=pl.BlockSpec((rows, tn), lambda j: (0, j)),
        out_shape=jax.ShapeDtypeStruct((rows, ncols), F32),
        compiler_params=pltpu.CompilerParams(dimension_semantics=("arbitrary",), vmem_limit_bytes=VMEM_LIMIT),
        name="ada",
    )(c, w_ada, b_ada)


def _gelu(x):
    hx = 0.5 * x
    return hx * jnp.tanh(x * (GELU_C1 + GELU_C3 * (x * x))) + hx


def _first_kernel(x_hbm, sh_ref, sc_ref, g_ref, w_ref, c_ref, wada_ref, bada_ref, zg_ref, h_ref, modb_ref, wb_ref,
                  xbuf_ref, xsem, *, ncol):
    i, r, j = pl.program_id(0), pl.program_id(1), pl.program_id(2)
    nr = pl.num_programs(1)
    tm = xbuf_ref.shape[1]
    t = i * nr + r
    total = pl.num_programs(0) * nr
    first = t == 0
    tn = zg_ref.shape[1]

    def x_copy(tile, slot):
        rows = pl.ds(pl.multiple_of((tile % nr) * tm, tm), tm)
        return pltpu.make_async_copy(x_hbm.at[tile // nr, rows, :], xbuf_ref.at[slot], xsem.at[slot])

    @pl.when(first & (j == 0))
    def _():
        x_copy(t, 0).start()

    @pl.when(j == 0)
    def _():
        x_copy(t, t % 2).wait()

    @pl.when((j == 0) & (t + 1 < total))
    def _():
        x_copy(t + 1, (t + 1) % 2).start()

    def tile(jj):
        modb_ref[...] = _ada_tile(c_ref, wada_ref, bada_ref)
        if jj == 0:
            row = pl.ds(i, 1)
            scale = g_ref[...] * (1.0 + sc_ref[row, :])
            h_ref[...] = (_rms(xbuf_ref[t % 2]) * scale + sh_ref[row, :]).astype(BF16)
        zg_ref[...] = _gelu(_dot(h_ref[...], wb_ref[jj])).astype(BF16)

    for jj in range(ncol):
        @pl.when(first & (j == jj))
        def _(jj=jj):
            wb_ref[jj] = w_ref[:, jj * tn:(jj + 1) * tn].astype(BF16)

        pl.when(j == jj)(functools.partial(tile, jj))


def _first(x, mod_a, g, w, c, w_ada, b_ada, ncol=2, tm=512, tn=1024):
    b, s, d = x.shape
    nr = s // tm
    steps = b * nr * ncol
    done = mod_a.shape[1]
    slab = (w_ada.shape[2] - done) // steps
    assert slab * steps == w_ada.shape[2] - done and slab % 128 == 0 and done % slab == 0
    step = lambda i, r, j: (i * nr + r) * ncol + j
    return pl.pallas_call(
        functools.partial(_first_kernel, ncol=ncol),
        grid=(b, nr, ncol),
        in_specs=[pl.BlockSpec(memory_space=pl.ANY),
                  pl.BlockSpec((b, d), lambda i, r, j: (0, SH1)),
                  pl.BlockSpec((b, d), lambda i, r, j: (0, SC1)),
                  pl.BlockSpec((1, d), lambda i, r, j: (0, 0)),
                  pl.BlockSpec((None, d, ncol * tn), lambda i, r, j: (0, 0, 0), pipeline_mode=pl.Buffered(1)),
                  pl.BlockSpec(c.shape, lambda i, r, j: (0, 0)),
                  pl.BlockSpec((None, d, slab), lambda i, r, j: (0, 0, done // slab + step(i, r, j))),
                  pl.BlockSpec((1, slab), lambda i, r, j: (0, done // slab + step(i, r, j)))],
        out_specs=[pl.BlockSpec((tm, tn), lambda i, r, j: (i * nr + r, j)),
                   pl.BlockSpec((tm, d), lambda i, r, j: (i * nr + r, 0)),
                   pl.BlockSpec((c.shape[0], slab), lambda i, r, j: (0, step(i, r, j)))],
        out_shape=[jax.ShapeDtypeStruct((b * s, ncol * tn), BF16), jax.ShapeDtypeStruct((b * s, d), BF16),
                   jax.ShapeDtypeStruct((c.shape[0], w_ada.shape[2] - done), F32)],
        scratch_shapes=[pltpu.VMEM((ncol, d, tn), BF16), pltpu.VMEM((2, tm, d), F32), pltpu.SemaphoreType.DMA((2,))],
        compiler_params=pltpu.CompilerParams(dimension_semantics=("arbitrary", "arbitrary", "arbitrary"),
                                             vmem_limit_bytes=VMEM_LIMIT),
        name="first",
    )(x, mod_a, mod_a, g, w, c, w_ada, b_ada)


IN_TILES = ((2, "silu"), (5, "silu"), (6, "gate"), (7, "gate"), (8, "gate"), (9, "gate"), (4, "none"), (3, "logf"))
Z_Q, Z_OG, Z_GATE, Z_IV = 0, 1, 2, 6


def _inproj_kernel(h_ref, w_ref, bg_ref, lb_ref, z_ref, lf_ref, wb_ref):
    j = pl.program_id(0)

    @pl.when(pl.program_id(1) == 0)
    def _():
        wb_ref[...] = w_ref[...].astype(BF16)

    def tile(act):
        acc = _dot(h_ref[...], wb_ref[...])
        if act == "silu":
            hx = 0.5 * acc
            z_ref[...] = (hx * jnp.tanh(hx) + hx).astype(BF16)
        elif act == "gate":
            z_ref[...] = (0.5 * jnp.tanh(0.5 * (acc + bg_ref[...])) + 0.5).astype(BF16)
        elif act == "none":
            z_ref[...] = acc.astype(BF16)
        else:
            lb = jax.nn.softmax(lb_ref[...], axis=0)[0:1, :]
            lf_ref[...] = jnp.log((0.5 + 0.5 * lb) + (0.5 - 0.5 * lb) * jnp.tanh(0.5 * acc))

    acts = [a for _, a in IN_TILES]
    for act in dict.fromkeys(acts):
        first = acts.index(act)
        last = len(acts) - 1 - acts[::-1].index(act)
        pl.when((j >= first) & (j <= last))(functools.partial(tile, act))


def _inproj(h, w, b_gate, hg_lb, tm=1024, tn=1024):
    m, d = h.shape
    ni, nj = m // tm, len(IN_TILES)
    acts = [a for _, a in IN_TILES]
    assert acts[-1] == "logf" and acts.count("logf") == 1
    gate0 = acts.index("gate")
    ngate = acts.count("gate")

    def wcol(j):
        col = j
        for k, (c, _) in enumerate(IN_TILES):
            col = jnp.where(j == k, c, col)
        return col

    last = nj - 1
    return pl.pallas_call(
        _inproj_kernel,
        grid=(nj, ni),
        in_specs=[pl.BlockSpec((tm, d), lambda j, i: (i, 0)),
                  pl.BlockSpec((None, d, tn), lambda j, i: (0, 0, wcol(j))),
                  pl.BlockSpec((1, tn), lambda j, i: (0, jnp.clip(j - gate0, 0, ngate - 1))),
                  pl.BlockSpec(hg_lb.shape, lambda j, i: (0, 0))],
        out_specs=[pl.BlockSpec((tm, tn), lambda j, i: (jnp.where(j < last, i, ni - 1), jnp.minimum(j, last - 1))),
                   pl.BlockSpec((tm, tn), lambda j, i: (jnp.where(j < last, 0, i), 0))],
        out_shape=[jax.ShapeDtypeStruct((m, (nj - 1) * tn), BF16), jax.ShapeDtypeStruct((m, tn), F32)],
        scratch_shapes=[pltpu.VMEM((d, tn), BF16)],
        compiler_params=pltpu.CompilerParams(dimension_semantics=("arbitrary", "arbitrary"),
                                             vmem_limit_bytes=VMEM_LIMIT),
        name="inproj",
    )(h, w, b_gate, hg_lb)


def _level_masks():
    t = np.arange(CHUNK)[:, None]
    s = np.arange(CHUNK)[None, :]
    out = [(t // (2 * m) == s // (2 * m)) & ((t // m) % 2 == 1) & ((s // m) % 2 == 0) for m in LEVELS]
    return np.stack(out).astype(np.float32)


def _decay_matrix():
    t = np.arange(CHUNK)[:, None]
    j = np.arange(CHUNK)[None, :]
    blocks = [j <= t]
    for m in MATMUL_LEVELS:
        r = (t // (2 * m)) * (2 * m) + m - 1
        blocks.append(np.where((t // m) % 2 == 1, (j > r) & (j <= t), (j > t) & (j <= r)))
    blocks.append(j > t)
    w = np.concatenate(blocks, axis=0).astype(np.float32)
    return np.concatenate([w, w], axis=1)


def _mixer_kernel(u_ref, v_ref, q_ref, og_ref, lf_ref, iv_ref, lng_ref, lnb_ref, ws_ref, bs_ref, ng_ref,
                  cw_ref, msk_ref, *rest, n_cast):
    src, (ya_ref, yb_ref), dst, st_ref = rest[:n_cast], rest[n_cast:n_cast + 2], rest[n_cast + 2:-1], rest[-1]
    rows = u_ref.shape[0]
    nchunk = rows // CHUNK
    ngroup = u_ref.shape[1] // GROUP

    @pl.when(pl.program_id(1) == 0)
    def _():
        st_ref[...] = jnp.zeros_like(st_ref)

    for s_ref, d_ref in zip(src, dst):
        d_ref[...] = s_ref[...].astype(BF16)

    v = v_ref[...].astype(F32)
    mu = jnp.mean(v, axis=-1, keepdims=True)
    vc = v - mu
    var = jnp.mean(vc * vc, axis=-1, keepdims=True)
    vn = (vc * lax.rsqrt(var + EPS) * lng_ref[...] + lnb_ref[...]).astype(BF16)
    tri = lax.broadcasted_iota(jnp.int32, (CHUNK, CHUNK), 0) >= lax.broadcasted_iota(jnp.int32, (CHUNK, CHUNK), 1)
    bst = bs_ref[...].T
    for g in range(ngroup):
        cols = slice(g * GROUP, (g + 1) * GROUP)
        w = jnp.where(tri, ws_ref[g], 0.0).astype(BF16)
        rhs = jnp.concatenate([vn[c * CHUNK:(c + 1) * CHUNK, cols] for c in range(nchunk)], axis=1)
        sg = _dot(w, rhs) + bst[:, g:g + 1]
        for c in range(nchunk):
            r = slice(c * CHUNK, (c + 1) * CHUNK)
            ya_ref[r, cols] = (u_ref[r, cols].astype(F32) * sg[:, c * CHUNK:(c + 1) * CHUNK]).astype(BF16)

    nlev = len(LEVELS)
    odd = lax.broadcasted_iota(jnp.int32, (CHUNK, GROUP), 0) % 2 == 1
    for c in range(nchunk):
        r = slice(c * CHUNK, (c + 1) * CHUNK)
        lf2 = lf_ref[r, :] * LOG2E
        ex = _dot(cw_ref[...], jnp.concatenate(_split_bf16(lf2, 2), axis=0))
        for h0 in range(0, ngroup, HEADS_PER_PASS):
            hs = range(h0, h0 + HEADS_PER_PASS)
            cs = {h: slice(h * GROUP, (h + 1) * GROUP) for h in hs}
            blk = lambda h, i: ex[i * CHUNK:(i + 1) * CHUNK, cs[h]]
            nmm = len(MATMUL_LEVELS)

            def level_decay(h, m):
                if m in MATMUL_LEVELS:
                    return jnp.exp2(blk(h, 1 + MATMUL_LEVELS.index(m))).astype(BF16)
                b3 = blk(h, 0).reshape(CHUNK // (2 * m), 2 * m, GROUP)
                dl = (b3 - b3[:, m - 1:m, :]).reshape(CHUNK, GROUP)
                return jnp.exp2(-jnp.abs(dl)).astype(BF16)

            qb = {h: q_ref[r, cs[h]] for h in hs}
            kb = {h: (1.0 - jnp.exp2(lf2[:, cs[h]])).astype(BF16) for h in hs}
            st = {h: st_ref[h] for h in hs}
            o = {h: _dot_nt(qb[h] * jnp.exp2(blk(h, 0)).astype(BF16), st[h].astype(BF16)) for h in hs}
            a = {}
            for li in range(nlev):
                for h in hs:
                    e = level_decay(h, LEVELS[li])
                    p = msk_ref[li] * _dot_nt(qb[h] * e, kb[h] * e)
                    a[h] = p if li == 0 else a[h] + p
            for h in hs:
                iv = iv_ref[r, cs[h]]
                ivf = iv.astype(F32)
                qf = qb[h].astype(F32)
                fh = jnp.exp2(lf2[:, cs[h]])
                kf = 1.0 - fh
                c0 = jnp.sum(qf * kf, axis=-1, keepdims=True)
                c1 = jnp.sum(jnp.where(odd, qf * fh * pltpu.roll(kf, 1, 0), 0.0), axis=-1, keepdims=True)
                oh = o[h] + _dot(a[h].astype(BF16), iv) + c0 * ivf + c1 * pltpu.roll(ivf, 1, 0)
                b_last = ex[CHUNK - 1:CHUNK, cs[h]]
                st_ref[h] = st[h] * jnp.exp2(b_last) + _dot_tn(iv, kb[h] * jnp.exp2(blk(h, 1 + nmm)).astype(BF16))
                y = _rms(oh) * ng_ref[:, cs[h]] * og_ref[r, cs[h]].astype(F32)
                yb_ref[r, cs[h]] = y.astype(BF16)


def _mixer(zg, z, lf, ln_g, ln_b, ws, bs, norm_g, batch, casts, rows=256):
    m, width = lf.shape
    nr = m // batch // rows
    nsteps = batch * nr
    ngroup = width // GROUP
    cw = jnp.asarray(_decay_matrix(), BF16)
    msk = jnp.asarray(_level_masks(), F32)
    row_blk = lambda col: pl.BlockSpec((rows, width), lambda b, r: (b * nr + r, col))
    full = lambda a: pl.BlockSpec(a.shape, lambda b, r: (0,) * a.ndim)
    in_specs = [row_blk(0), row_blk(1), row_blk(Z_Q), row_blk(Z_OG), row_blk(0), row_blk(Z_IV),
                full(ln_g), full(ln_b), full(ws), full(bs), full(norm_g), full(cw), full(msk)]
    out_specs = [row_blk(0), row_blk(0)]
    out_shape = [jax.ShapeDtypeStruct((m, width), BF16), jax.ShapeDtypeStruct((m, width), BF16)]
    for a in casts:
        _, n, w = a.shape
        slab = n // nsteps
        assert slab * nsteps == n and slab % 16 == 0
        in_specs.append(pl.BlockSpec((None, slab, w), lambda b, r: (0, b * nr + r, 0)))
        out_specs.append(pl.BlockSpec((slab, w), lambda b, r: (b * nr + r, 0)))
        out_shape.append(jax.ShapeDtypeStruct((n, w), BF16))
    return pl.pallas_call(
        functools.partial(_mixer_kernel, n_cast=len(casts)),
        grid=(batch, nr),
        in_specs=in_specs,
        out_specs=out_specs,
        out_shape=out_shape,
        scratch_shapes=[pltpu.VMEM((ngroup, GROUP, GROUP), F32)],
        compiler_params=pltpu.CompilerParams(dimension_semantics=("arbitrary", "arbitrary"),
                                             vmem_limit_bytes=VMEM_LIMIT),
        name="mixer",
    )(zg, zg, z, z, lf, z, ln_g, ln_b, ws, bs, norm_g, cw, msk, *casts)


def _merge_kernel(ya_ref, yb_ref, ga_ref, gb_ref, x_ref, gt_ref, sh_ref, sc_ref, wa_ref, wb_ref, wo_ref, g_ref, x1_ref,
                  h2_ref):
    tm = ya_ref.shape[0]
    row = pl.ds(pl.program_id(0), 1)
    scale = g_ref[...] * (1.0 + sc_ref[row, :])
    gate, shift = gt_ref[row, :], sh_ref[row, :]
    for p in range(MERGE_PARTS):
        r = slice(p * tm // MERGE_PARTS, (p + 1) * tm // MERGE_PARTS)
        pa = _dot(ya_ref[r, :], wa_ref[...])
        pb = _dot(yb_ref[r, :], wb_ref[...])
        y = (ga_ref[r, :].astype(F32) * pa + gb_ref[r, :].astype(F32) * pb).astype(BF16)
        x1 = x_ref[0, r, :] + gate * _dot(y, wo_ref[...])
        x1_ref[0, r, :] = x1
        h2_ref[0, r, :] = (_rms(x1) * scale + shift).astype(BF16)


def _merge(ya, yb, z, x, mod_b, wa, wb, wo, g, tm=512):
    b, s, d = x.shape
    nr = s // tm
    width = ya.shape[1]
    resident = lambda a: pl.BlockSpec(a.shape, lambda i, r: (0,) * a.ndim, pipeline_mode=pl.Buffered(1))
    return pl.pallas_call(
        _merge_kernel,
        grid=(b, nr),
        in_specs=[pl.BlockSpec((tm, width), lambda i, r: (i * nr + r, 0)),
                  pl.BlockSpec((tm, width), lambda i, r: (i * nr + r, 0)),
                  pl.BlockSpec((tm, d), lambda i, r: (i * nr + r, Z_GATE * width // d)),
                  pl.BlockSpec((tm, d), lambda i, r: (i * nr + r, Z_GATE * width // d + 1)),
                  pl.BlockSpec((1, tm, d), lambda i, r: (i, r, 0)),
                  pl.BlockSpec((b, d), lambda i, r: (0, GT1)),
                  pl.BlockSpec((b, d), lambda i, r: (0, SH2)),
                  pl.BlockSpec((b, d), lambda i, r: (0, SC2)),
                  resident(wa), resident(wb), resident(wo),
                  pl.BlockSpec((1, d), lambda i, r: (0, 0))],
        out_specs=[pl.BlockSpec((1, tm, d), lambda i, r: (i, r, 0)),
                   pl.BlockSpec((1, tm, d), lambda i, r: (i, r, 0))],
        out_shape=[jax.ShapeDtypeStruct((b, s, d), F32), jax.ShapeDtypeStruct((b, s, d), BF16)],
        compiler_params=pltpu.CompilerParams(dimension_semantics=("arbitrary", "arbitrary"),
                                             vmem_limit_bytes=VMEM_LIMIT),
        name="merge",
    )(ya, yb, z, z, x, mod_b, mod_b, mod_b, wa, wb, wo, g)


def _ffn_kernel(h_ref, wa_ref, wu_ref, wo_ref, x1_ref, mod_ref, g_ref, o_ref, acc_ref, *, nt):
    t = pl.program_id(2)

    def hidden_tile():
        h = h_ref[0]
        a = _dot(h, wa_ref[...])
        up = _dot(h, wu_ref[...])
        ha = 0.5 * a
        return _dot(((ha * jnp.tanh(ha) + ha) * up).astype(BF16), wo_ref[...])

    @pl.when(t == 0)
    def _():
        acc_ref[...] = hidden_tile()

    @pl.when((t > 0) & (t < nt))
    def _():
        acc_ref[...] += hidden_tile()

    @pl.when(t >= nt)
    def _():
        half = o_ref.shape[1]
        base = (t - nt) * half
        gate = mod_ref[pl.ds(pl.program_id(0), 1), :]
        g = g_ref[...]

        def body(i, carry):
            rows = pl.ds(pl.multiple_of(i * ROW_GROUP, ROW_GROUP), ROW_GROUP)
            arows = pl.ds(pl.multiple_of(base + i * ROW_GROUP, ROW_GROUP), ROW_GROUP)
            o_ref[0, rows, :] = _rms(x1_ref[0, rows, :] + gate * acc_ref[arows, :]) * g
            return carry

        lax.fori_loop(0, half // ROW_GROUP, body, 0, unroll=8)


def _ffn(h2, w_in, w_out, x1, mod_b, g, tm=1024, th=512):
    b, s, d = x1.shape
    hidden = w_out.shape[0]
    nt = hidden // th
    half = tm // 2
    nr = s // tm
    wt = lambda t: jnp.where(t < nt, t, 0)
    fin = lambda r, t: 2 * r + jnp.clip(t - nt, 0, 1)

    def h2_blk(i, r, t):
        lin = jnp.minimum(i * nr + r + (t >= nt).astype(jnp.int32), b * nr - 1)
        return (lin // nr, lin % nr, 0)

    return pl.pallas_call(
        functools.partial(_ffn_kernel, nt=nt),
        grid=(b, s // tm, nt + 2),
        in_specs=[pl.BlockSpec((1, tm, d), h2_blk),
                  pl.BlockSpec((d, th), lambda i, r, t: (0, wt(t))),
                  pl.BlockSpec((d, th), lambda i, r, t: (0, nt + wt(t))),
                  pl.BlockSpec((th, d), lambda i, r, t: (wt(t), 0)),
                  pl.BlockSpec((1, half, d), lambda i, r, t: (i, fin(r, t), 0)),
                  pl.BlockSpec((b, d), lambda i, r, t: (0, GT2)),
                  pl.BlockSpec((1, d), lambda i, r, t: (0, 0))],
        out_specs=pl.BlockSpec((1, half, d), lambda i, r, t: (i, fin(r, t), 0)),
        out_shape=jax.ShapeDtypeStruct((b, s, d), F32),
        scratch_shapes=[pltpu.VMEM((tm, d), F32)],
        compiler_params=pltpu.CompilerParams(dimension_semantics=("arbitrary", "arbitrary", "arbitrary"),
                                             vmem_limit_bytes=VMEM_LIMIT),
        name="ffn",
    )(h2, w_in, w_in, w_out, x1, mod_b, g)


def kernel(x, c, w_ada, b_ada, norm1_g, w_in, b_gate, gmlp_ln_g, gmlp_ln_b, gmlp_ws, gmlp_bs, hg_lb, hg_norm_g,
           w_branch_gmlp, w_branch_hg, w_out, norm2_g, w_ffn_in, w_ffn_out, final_norm_g):
    batch, seq, d = x.shape
    depth = w_ada.shape[0]
    width = w_branch_gmlp.shape[1]
    assert depth == 1 and width == 8 * GROUP and gmlp_ws.shape[2] == CHUNK and seq % 256 == 0
    assert w_in.shape[2] == 6 * width + 2 * d and hg_lb.shape[0] == depth + 1
    tn = 1024
    assert width == tn and d == 2 * tn

    mod_a = _ada(c, w_ada, b_ada, 2 * d)
    zg, h1, mod_b = _first(x, mod_a, norm1_g, w_in, c, w_ada, b_ada)
    z, lf = _inproj(h1, w_in, b_gate, hg_lb)

    ya, yb, w_a, w_b, w_o, w_fi, w_fo = _mixer(
        zg, z, lf, gmlp_ln_g, gmlp_ln_b, gmlp_ws[0], gmlp_bs[0], hg_norm_g, batch,
        casts=(w_branch_gmlp, w_branch_hg, w_out, w_ffn_in, w_ffn_out))

    x1, h2 = _merge(ya, yb, z, x, mod_b, w_a, w_b, w_o, norm2_g)
    return _ffn(h2, w_fi, w_fo, x1, mod_b, final_norm_g.reshape(1, d))
```

```python
import functools

import numpy as np
import jax
import jax.numpy as jnp
from jax import lax
from jax.experimental import pallas as pl
from jax.experimental.pallas import tpu as pltpu

F32 = jnp.float32
BF16 = jnp.bfloat16
EPS = 1e-6
LOG2E = 1.4426950408889634
GELU_C1 = 0.7978845608028654
GELU_C3 = 0.044715 * GELU_C1

SH1, SC1 = 0, 1
GT1, SH2, SC2, GT2 = 0, 1, 2, 3

GROUP = 128
CHUNK = 128
MERGE_PARTS = 2
ROW_GROUP = 16
HEADS_PER_PASS = 8
LEVELS = (64, 32, 16, 8, 4, 2)
MATMUL_LEVELS = (4, 2)
VMEM_LIMIT = 56 * 1024 * 1024


def _dot(a, b):
    return jnp.dot(a, b, preferred_element_type=F32)


def _dot_nt(a, b):
    return lax.dot_general(a, b, (((1,), (1,)), ((), ())), preferred_element_type=F32)


def _dot_tn(a, b):
    return lax.dot_general(a, b, (((0,), (0,)), ((), ())), preferred_element_type=F32)


def _rms(x):
    return x * lax.rsqrt(jnp.mean(x * x, axis=-1, keepdims=True) + EPS)


def _split_bf16(x, parts):
    out = []
    for _ in range(parts - 1):
        p = x.astype(BF16)
        out.append(p)
        x = x - p.astype(F32)
    out.append(x.astype(BF16))
    return out


def _ada_tile(c_ref, w_ref, b_ref):
    rows = c_ref.shape[0]
    ca = jnp.concatenate(_split_bf16(jax.nn.silu(c_ref[...]), 2), axis=0)
    acc = _dot(ca, w_ref[...].astype(BF16))
    return acc[:rows] + acc[rows:] + b_ref[...]


def _ada_kernel(c_ref, w_ref, b_ref, o_ref):
    o_ref[...] = _ada_tile(c_ref, w_ref, b_ref)


def _ada(c, w_ada, b_ada, ncols, tn=1024):
    _, d, _ = w_ada.shape
    rows = c.shape[0]
    return pl.pallas_call(
        _ada_kernel,
        grid=(ncols // tn,),
        in_specs=[pl.BlockSpec((rows, d), lambda j: (0, 0)),
                  pl.BlockSpec((None, d, tn), lambda j: (0, 0, j)),
                  pl.BlockSpec((1, tn), lambda j: (0, j))],
        out_specs=pl.BlockSpec((rows, tn), lambda j: (0, j)),
        out_shape=jax.ShapeDtypeStruct((rows, ncols), F32),
        compiler_params=pltpu.CompilerParams(dimension_semantics=("arbitrary",), vmem_limit_bytes=VMEM_LIMIT),
        name="ada",
    )(c, w_ada, b_ada)


def _gelu(x):
    hx = 0.5 * x
    return hx * jnp.tanh(x * (GELU_C1 + GELU_C3 * (x * x))) + hx


def _first_kernel(x_hbm, sh_ref, sc_ref, g_ref, w_ref, c_ref, wada_hbm, bada_ref, zg_ref, h_ref, modb_ref, wb_ref,
                  xbuf_ref, xsem, abuf_ref, asem, *, ncol, done):
    i, r, j = pl.program_id(0), pl.program_id(1), pl.program_id(2)
    nr = pl.num_programs(1)
    tm = xbuf_ref.shape[1]
    t = i * nr + r
    total = pl.num_programs(0) * nr
    first = t == 0
    tn = zg_ref.shape[1]

    def x_copy(tile, slot):
        rows = pl.ds(pl.multiple_of((tile % nr) * tm, tm), tm)
        return pltpu.make_async_copy(x_hbm.at[tile // nr, rows, :], xbuf_ref.at[slot], xsem.at[slot])

    n = t * ncol + j
    nsteps = total * ncol
    slab = abuf_ref.shape[2]

    def a_copy(step):
        cols = pl.ds(pl.multiple_of(done + step * slab, slab), slab)
        return pltpu.make_async_copy(wada_hbm.at[0, :, cols], abuf_ref.at[step % 3], asem.at[step % 3])

    @pl.when(first & (j == 0))
    def _():
        x_copy(t, 0).start()
        a_copy(0).start()
        a_copy(1).start()

    @pl.when(n < nsteps)
    def _():
        a_copy(n).wait()

    @pl.when(n + 2 < nsteps)
    def _():
        a_copy(n + 2).start()

    @pl.when(j == 0)
    def _():
        x_copy(t, t % 2).wait()

    @pl.when((j == 0) & (t + 1 < total))
    def _():
        x_copy(t + 1, (t + 1) % 2).start()

    def tile(jj):
        modb_ref[...] = _ada_tile(c_ref, abuf_ref.at[n % 3], bada_ref)
        if jj == 0:
            row = pl.ds(i, 1)
            scale = g_ref[...] * (1.0 + sc_ref[row, :])
            h_ref[...] = (_rms(xbuf_ref[t % 2]) * scale + sh_ref[row, :]).astype(BF16)
        zg_ref[...] = _gelu(_dot(h_ref[...], wb_ref[jj])).astype(BF16)

    for jj in range(ncol):
        @pl.when(first & (j == jj))
        def _(jj=jj):
            wb_ref[jj] = w_ref[:, jj * tn:(jj + 1) * tn].astype(BF16)

        pl.when(j == jj)(functools.partial(tile, jj))


def _first(x, mod_a, g, w, c, w_ada, b_ada, ncol=2, tm=512, tn=1024):
    b, s, d = x.shape
    nr = s // tm
    steps = b * nr * ncol
    done = mod_a.shape[1]
    slab = (w_ada.shape[2] - done) // steps
    assert slab * steps == w_ada.shape[2] - done and slab % 128 == 0 and done % slab == 0
    step = lambda i, r, j: (i * nr + r) * ncol + j
    return pl.pallas_call(
        functools.partial(_first_kernel, ncol=ncol, done=done),
        grid=(b, nr, ncol),
        in_specs=[pl.BlockSpec(memory_space=pl.ANY),
                  pl.BlockSpec((b, d), lambda i, r, j: (0, SH1)),
                  pl.BlockSpec((b, d), lambda i, r, j: (0, SC1)),
                  pl.BlockSpec((1, d), lambda i, r, j: (0, 0)),
                  pl.BlockSpec((None, d, ncol * tn), lambda i, r, j: (0, 0, 0), pipeline_mode=pl.Buffered(1)),
                  pl.BlockSpec(c.shape, lambda i, r, j: (0, 0)),
                  pl.BlockSpec(memory_space=pl.ANY),
                  pl.BlockSpec((1, slab), lambda i, r, j: (0, done // slab + step(i, r, j)))],
        out_specs=[pl.BlockSpec((tm, tn), lambda i, r, j: (i * nr + r, j)),
                   pl.BlockSpec((tm, d), lambda i, r, j: (i * nr + r, 0)),
                   pl.BlockSpec((c.shape[0], slab), lambda i, r, j: (0, step(i, r, j)))],
        out_shape=[jax.ShapeDtypeStruct((b * s, ncol * tn), BF16), jax.ShapeDtypeStruct((b * s, d), BF16),
                   jax.ShapeDtypeStruct((c.shape[0], w_ada.shape[2] - done), F32)],
        scratch_shapes=[pltpu.VMEM((ncol, d, tn), BF16), pltpu.VMEM((2, tm, d), F32), pltpu.SemaphoreType.DMA((2,)),
                        pltpu.VMEM((3, d, slab), F32), pltpu.SemaphoreType.DMA((3,))],
        compiler_params=pltpu.CompilerParams(dimension_semantics=("arbitrary", "arbitrary", "arbitrary"),
                                             vmem_limit_bytes=VMEM_LIMIT),
        name="first",
    )(x, mod_a, mod_a, g, w, c, w_ada, b_ada)


IN_TILES = ((2, "silu"), (5, "silu"), (6, "gate"), (7, "gate"), (8, "gate"), (9, "gate"), (4, "none"), (3, "logf"))
Z_Q, Z_OG, Z_GATE, Z_IV = 0, 1, 2, 6


def _inproj_kernel(h_ref, w_ref, bg_ref, lb_ref, z_ref, lf_ref, wb_ref):
    j = pl.program_id(0)

    @pl.when(pl.program_id(1) == 0)
    def _():
        wb_ref[...] = w_ref[...].astype(BF16)

    def tile(act):
        acc = _dot(h_ref[...], wb_ref[...])
        if act == "silu":
            hx = 0.5 * acc
            z_ref[...] = (hx * jnp.tanh(hx) + hx).astype(BF16)
        elif act == "gate":
            z_ref[...] = (0.5 * jnp.tanh(0.5 * (acc + bg_ref[...])) + 0.5).astype(BF16)
        elif act == "none":
            z_ref[...] = acc.astype(BF16)
        else:
            lb = jax.nn.softmax(lb_ref[...], axis=0)[0:1, :]
            lf_ref[...] = jnp.log((0.5 + 0.5 * lb) + (0.5 - 0.5 * lb) * jnp.tanh(0.5 * acc))

    acts = [a for _, a in IN_TILES]
    for act in dict.fromkeys(acts):
        first = acts.index(act)
        last = len(acts) - 1 - acts[::-1].index(act)
        pl.when((j >= first) & (j <= last))(functools.partial(tile, act))


def _inproj(h, w, b_gate, hg_lb, tm=1024, tn=1024):
    m, d = h.shape
    ni, nj = m // tm, len(IN_TILES)
    acts = [a for _, a in IN_TILES]
    assert acts[-1] == "logf" and acts.count("logf") == 1
    gate0 = acts.index("gate")
    ngate = acts.count("gate")

    def wcol(j):
        col = j
        for k, (c, _) in enumerate(IN_TILES):
            col = jnp.where(j == k, c, col)
        return col

    last = nj - 1
    return pl.pallas_call(
        _inproj_kernel,
        grid=(nj, ni),
        in_specs=[pl.BlockSpec((tm, d), lambda j, i: (i, 0)),
                  pl.BlockSpec((None, d, tn), lambda j, i: (0, 0, wcol(j))),
                  pl.BlockSpec((1, tn), lambda j, i: (0, jnp.clip(j - gate0, 0, ngate - 1))),
                  pl.BlockSpec(hg_lb.shape, lambda j, i: (0, 0))],
        out_specs=[pl.BlockSpec((tm, tn), lambda j, i: (jnp.where(j < last, i, ni - 1), jnp.minimum(j, last - 1))),
                   pl.BlockSpec((tm, tn), lambda j, i: (jnp.where(j < last, 0, i), 0))],
        out_shape=[jax.ShapeDtypeStruct((m, (nj - 1) * tn), BF16), jax.ShapeDtypeStruct((m, tn), F32)],
        scratch_shapes=[pltpu.VMEM((d, tn), BF16)],
        compiler_params=pltpu.CompilerParams(dimension_semantics=("arbitrary", "arbitrary"),
                                             vmem_limit_bytes=VMEM_LIMIT),
        name="inproj",
    )(h, w, b_gate, hg_lb)


def _level_masks():
    t = np.arange(CHUNK)[:, None]
    s = np.arange(CHUNK)[None, :]
    out = [(t // (2 * m) == s // (2 * m)) & ((t // m) % 2 == 1) & ((s // m) % 2 == 0) for m in LEVELS]
    return np.stack(out).astype(np.float32)


def _decay_matrix():
    t = np.arange(CHUNK)[:, None]
    j = np.arange(CHUNK)[None, :]
    blocks = [j <= t]
    for m in MATMUL_LEVELS:
        r = (t // (2 * m)) * (2 * m) + m - 1
        blocks.append(np.where((t // m) % 2 == 1, (j > r) & (j <= t), (j > t) & (j <= r)))
    blocks.append(j > t)
    w = np.concatenate(blocks, axis=0).astype(np.float32)
    return np.concatenate([w, w], axis=1)


def _mixer_kernel(u_ref, v_ref, q_ref, og_ref, lf_ref, iv_ref, lng_ref, lnb_ref, ws_ref, bs_ref, ng_ref,
                  cw_ref, msk_ref, *rest, n_cast):
    src, (ya_ref, yb_ref), dst, st_ref = rest[:n_cast], rest[n_cast:n_cast + 2], rest[n_cast + 2:-1], rest[-1]
    rows = u_ref.shape[0]
    nchunk = rows // CHUNK
    ngroup = u_ref.shape[1] // GROUP

    @pl.when(pl.program_id(1) == 0)
    def _():
        st_ref[...] = jnp.zeros_like(st_ref)

    for s_ref, d_ref in zip(src, dst):
        d_ref[...] = s_ref[...].astype(BF16)

    v = v_ref[...].astype(F32)
    mu = jnp.mean(v, axis=-1, keepdims=True)
    vc = v - mu
    var = jnp.mean(vc * vc, axis=-1, keepdims=True)
    vn = (vc * lax.rsqrt(var + EPS) * lng_ref[...] + lnb_ref[...]).astype(BF16)
    tri = lax.broadcasted_iota(jnp.int32, (CHUNK, CHUNK), 0) >= lax.broadcasted_iota(jnp.int32, (CHUNK, CHUNK), 1)
    bst = bs_ref[...].T
    for g in range(ngroup):
        cols = slice(g * GROUP, (g + 1) * GROUP)
        w = jnp.where(tri, ws_ref[g], 0.0).astype(BF16)
        rhs = jnp.concatenate([vn[c * CHUNK:(c + 1) * CHUNK, cols] for c in range(nchunk)], axis=1)
        sg = _dot(w, rhs) + bst[:, g:g + 1]
        for c in range(nchunk):
            r = slice(c * CHUNK, (c + 1) * CHUNK)
            ya_ref[r, cols] = (u_ref[r, cols].astype(F32) * sg[:, c * CHUNK:(c + 1) * CHUNK]).astype(BF16)

    nlev = len(LEVELS)
    odd = lax.broadcasted_iota(jnp.int32, (CHUNK, GROUP), 0) % 2 == 1
    for c in range(nchunk):
        r = slice(c * CHUNK, (c + 1) * CHUNK)
        lf2 = lf_ref[r, :] * LOG2E
        ex = _dot(cw_ref[...], jnp.concatenate(_split_bf16(lf2, 2), axis=0))
        for h0 in range(0, ngroup, HEADS_PER_PASS):
            hs = range(h0, h0 + HEADS_PER_PASS)
            cs = {h: slice(h * GROUP, (h + 1) * GROUP) for h in hs}
            blk = lambda h, i: ex[i * CHUNK:(i + 1) * CHUNK, cs[h]]
            nmm = len(MATMUL_LEVELS)

            def level_decay(h, m):
                if m in MATMUL_LEVELS:
                    return jnp.exp2(blk(h, 1 + MATMUL_LEVELS.index(m))).astype(BF16)
                b3 = blk(h, 0).reshape(CHUNK // (2 * m), 2 * m, GROUP)
                dl = (b3 - b3[:, m - 1:m, :]).reshape(CHUNK, GROUP)
                return jnp.exp2(-jnp.abs(dl)).astype(BF16)

            qb = {h: q_ref[r, cs[h]] for h in hs}
            kb = {h: (1.0 - jnp.exp2(lf2[:, cs[h]])).astype(BF16) for h in hs}
            st = {h: st_ref[h] for h in hs}
            o = {h: _dot_nt(qb[h] * jnp.exp2(blk(h, 0)).astype(BF16), st[h].astype(BF16)) for h in hs}
            a = {}
            for li in range(nlev):
                for h in hs:
                    e = level_decay(h, LEVELS[li])
                    p = msk_ref[li] * _dot_nt(qb[h] * e, kb[h] * e)
                    a[h] = p if li == 0 else a[h] + p
            for h in hs:
                iv = iv_ref[r, cs[h]]
                ivf = iv.astype(F32)
                qf = qb[h].astype(F32)
                fh = jnp.exp2(lf2[:, cs[h]])
                kf = 1.0 - fh
                c0 = jnp.sum(qf * kf, axis=-1, keepdims=True)
                c1 = jnp.sum(jnp.where(odd, qf * fh * pltpu.roll(kf, 1, 0), 0.0), axis=-1, keepdims=True)
                oh = o[h] + _dot(a[h].astype(BF16), iv) + c0 * ivf + c1 * pltpu.roll(ivf, 1, 0)
                b_last = ex[CHUNK - 1:CHUNK, cs[h]]
                st_ref[h] = st[h] * jnp.exp2(b_last) + _dot_tn(iv, kb[h] * jnp.exp2(blk(h, 1 + nmm)).astype(BF16))
                y = _rms(oh) * ng_ref[:, cs[h]] * og_ref[r, cs[h]].astype(F32)
                yb_ref[r, cs[h]] = y.astype(BF16)


def _mixer(zg, z, lf, ln_g, ln_b, ws, bs, norm_g, batch, casts, rows=256):
    m, width = lf.shape
    nr = m // batch // rows
    nsteps = batch * nr
    ngroup = width // GROUP
    cw = jnp.asarray(_decay_matrix(), BF16)
    msk = jnp.asarray(_level_masks(), F32)
    row_blk = lambda col: pl.BlockSpec((rows, width), lambda b, r: (b * nr + r, col))
    full = lambda a: pl.BlockSpec(a.shape, lambda b, r: (0,) * a.ndim)
    in_specs = [row_blk(0), row_blk(1), row_blk(Z_Q), row_blk(Z_OG), row_blk(0), row_blk(Z_IV),
                full(ln_g), full(ln_b), full(ws), full(bs), full(norm_g), full(cw), full(msk)]
    out_specs = [row_blk(0), row_blk(0)]
    out_shape = [jax.ShapeDtypeStruct((m, width), BF16), jax.ShapeDtypeStruct((m, width), BF16)]
    for a in casts:
        _, n, w = a.shape
        slab = n // nsteps
        assert slab * nsteps == n and slab % 16 == 0
        in_specs.append(pl.BlockSpec((None, slab, w), lambda b, r: (0, b * nr + r, 0)))
        out_specs.append(pl.BlockSpec((slab, w), lambda b, r: (b * nr + r, 0)))
        out_shape.append(jax.ShapeDtypeStruct((n, w), BF16))
    return pl.pallas_call(
        functools.partial(_mixer_kernel, n_cast=len(casts)),
        grid=(batch, nr),
        in_specs=in_specs,
        out_specs=out_specs,
        out_shape=out_shape,
        scratch_shapes=[pltpu.VMEM((ngroup, GROUP, GROUP), F32)],
        compiler_params=pltpu.CompilerParams(dimension_semantics=("arbitrary", "arbitrary"),
                                             vmem_limit_bytes=VMEM_LIMIT),
        name="mixer",
    )(zg, zg, z, z, lf, z, ln_g, ln_b, ws, bs, norm_g, cw, msk, *casts)


def _merge_kernel(ya_ref, yb_ref, ga_ref, gb_ref, x_ref, gt_ref, sh_ref, sc_ref, wa_ref, wb_ref, wo_ref, g_ref, x1_ref,
                  h2_ref):
    tm = ya_ref.shape[0]
    row = pl.ds(pl.program_id(0), 1)
    scale = g_ref[...] * (1.0 + sc_ref[row, :])
    gate, shift = gt_ref[row, :], sh_ref[row, :]
    for p in range(MERGE_PARTS):
        r = slice(p * tm // MERGE_PARTS, (p + 1) * tm // MERGE_PARTS)
        pa = _dot(ya_ref[r, :], wa_ref[...])
        pb = _dot(yb_ref[r, :], wb_ref[...])
        y = (ga_ref[r, :].astype(F32) * pa + gb_ref[r, :].astype(F32) * pb).astype(BF16)
        x1 = x_ref[0, r, :] + gate * _dot(y, wo_ref[...])
        x1_ref[0, r, :] = x1
        h2_ref[0, r, :] = (_rms(x1) * scale + shift).astype(BF16)


def _merge(ya, yb, z, x, mod_b, wa, wb, wo, g, tm=512):
    b, s, d = x.shape
    nr = s // tm
    width = ya.shape[1]
    resident = lambda a: pl.BlockSpec(a.shape, lambda i, r: (0,) * a.ndim, pipeline_mode=pl.Buffered(1))
    return pl.pallas_call(
        _merge_kernel,
        grid=(b, nr),
        in_specs=[pl.BlockSpec((tm, width), lambda i, r: (i * nr + r, 0)),
                  pl.BlockSpec((tm, width), lambda i, r: (i * nr + r, 0)),
                  pl.BlockSpec((tm, d), lambda i, r: (i * nr + r, Z_GATE * width // d)),
                  pl.BlockSpec((tm, d), lambda i, r: (i * nr + r, Z_GATE * width // d + 1)),
                  pl.BlockSpec((1, tm, d), lambda i, r: (i, r, 0)),
                  pl.BlockSpec((b, d), lambda i, r: (0, GT1)),
                  pl.BlockSpec((b, d), lambda i, r: (0, SH2)),
                  pl.BlockSpec((b, d), lambda i, r: (0, SC2)),
                  resident(wa), resident(wb), resident(wo),
                  pl.BlockSpec((1, d), lambda i, r: (0, 0))],
        out_specs=[pl.BlockSpec((1, tm, d), lambda i, r: (i, r, 0)),
                   pl.BlockSpec((1, tm, d), lambda i, r: (i, r, 0))],
        out_shape=[jax.ShapeDtypeStruct((b, s, d), F32), jax.ShapeDtypeStruct((b, s, d), BF16)],
        compiler_params=pltpu.CompilerParams(dimension_semantics=("arbitrary", "arbitrary"),
                                             vmem_limit_bytes=VMEM_LIMIT),
        name="merge",
    )(ya, yb, z, z, x, mod_b, mod_b, mod_b, wa, wb, wo, g)


def _ffn_kernel(h_ref, wa_ref, wu_ref, wo_ref, x1_ref, mod_ref, g_ref, o_ref, acc_ref, *, nt):
    t = pl.program_id(2)

    def hidden_tile():
        h = h_ref[0]
        a = _dot(h, wa_ref[...])
        up = _dot(h, wu_ref[...])
        ha = 0.5 * a
        return _dot(((ha * jnp.tanh(ha) + ha) * up).astype(BF16), wo_ref[...])

    @pl.when(t == 0)
    def _():
        acc_ref[...] = hidden_tile()

    @pl.when((t > 0) & (t < nt))
    def _():
        acc_ref[...] += hidden_tile()

    @pl.when(t >= nt)
    def _():
        half = o_ref.shape[1]
        base = (t - nt) * half
        gate = mod_ref[pl.ds(pl.program_id(0), 1), :]
        g = g_ref[...]

        def body(i, carry):
            rows = pl.ds(pl.multiple_of(i * ROW_GROUP, ROW_GROUP), ROW_GROUP)
            arows = pl.ds(pl.multiple_of(base + i * ROW_GROUP, ROW_GROUP), ROW_GROUP)
            o_ref[0, rows, :] = _rms(x1_ref[0, rows, :] + gate * acc_ref[arows, :]) * g
            return carry

        lax.fori_loop(0, half // ROW_GROUP, body, 0, unroll=8)


def _ffn(h2, w_in, w_out, x1, mod_b, g, tm=1024, th=512):
    b, s, d = x1.shape
    hidden = w_out.shape[0]
    nt = hidden // th
    half = tm // 2
    nr = s // tm
    wt = lambda t: jnp.where(t < nt, t, 0)
    fin = lambda r, t: 2 * r + jnp.clip(t - nt, 0, 1)

    def h2_blk(i, r, t):
        lin = jnp.minimum(i * nr + r + (t >= nt).astype(jnp.int32), b * nr - 1)
        return (lin // nr, lin % nr, 0)

    return pl.pallas_call(
        functools.partial(_ffn_kernel, nt=nt),
        grid=(b, s // tm, nt + 2),
        in_specs=[pl.BlockSpec((1, tm, d), h2_blk),
                  pl.BlockSpec((d, th), lambda i, r, t: (0, wt(t))),
                  pl.BlockSpec((d, th), lambda i, r, t: (0, nt + wt(t))),
                  pl.BlockSpec((th, d), lambda i, r, t: (wt(t), 0)),
                  pl.BlockSpec((1, half, d), lambda i, r, t: (i, fin(r, t), 0)),
                  pl.BlockSpec((b, d), lambda i, r, t: (0, GT2)),
                  pl.BlockSpec((1, d), lambda i, r, t: (0, 0))],
        out_specs=pl.BlockSpec((1, half, d), lambda i, r, t: (i, fin(r, t), 0)),
        out_shape=jax.ShapeDtypeStruct((b, s, d), F32),
        scratch_shapes=[pltpu.VMEM((tm, d), F32)],
        compiler_params=pltpu.CompilerParams(dimension_semantics=("arbitrary", "arbitrary", "arbitrary"),
                                             vmem_limit_bytes=VMEM_LIMIT),
        name="ffn",
    )(h2, w_in, w_in, w_out, x1, mod_b, g)


def kernel(x, c, w_ada, b_ada, norm1_g, w_in, b_gate, gmlp_ln_g, gmlp_ln_b, gmlp_ws, gmlp_bs, hg_lb, hg_norm_g,
           w_branch_gmlp, w_branch_hg, w_out, norm2_g, w_ffn_in, w_ffn_out, final_norm_g):
    batch, seq, d = x.shape
    depth = w_ada.shape[0]
    width = w_branch_gmlp.shape[1]
    assert depth == 1 and width == 8 * GROUP and gmlp_ws.shape[2] == CHUNK and seq % 256 == 0
    assert w_in.shape[2] == 6 * width + 2 * d and hg_lb.shape[0] == depth + 1
    tn = 1024
    assert width == tn and d == 2 * tn

    mod_a = _ada(c, w_ada, b_ada, 2 * d)
    zg, h1, mod_b = _first(x, mod_a, norm1_g, w_in, c, w_ada, b_ada)
    z, lf = _inproj(h1, w_in, b_gate, hg_lb)

    ya, yb, w_a, w_b, w_o, w_fi, w_fo = _mixer(
        zg, z, lf, gmlp_ln_g, gmlp_ln_b, gmlp_ws[0], gmlp_bs[0], hg_norm_g, batch,
        casts=(w_branch_gmlp, w_branch_hg, w_out, w_ffn_in, w_ffn_out))

    x1, h2 = _merge(ya, yb, z, x, mod_b, w_a, w_b, w_o, norm2_g)
    return _ffn(h2, w_fi, w_fo, x1, mod_b, final_norm_g.reshape(1, d))
```

```python
import functools

import numpy as np
import jax
import jax.numpy as jnp
from jax import lax
from jax.experimental import pallas as pl
from jax.experimental.pallas import tpu as pltpu

F32 = jnp.float32
BF16 = jnp.bfloat16
EPS = 1e-6
LOG2E = 1.4426950408889634
GELU_C1 = 0.7978845608028654
GELU_C3 = 0.044715 * GELU_C1

SH1, SC1 = 0, 1
GT1, SH2, SC2, GT2 = 0, 1, 2, 3

GROUP = 128
CHUNK = 128
MERGE_PARTS = 2
ROW_GROUP = 16
HEADS_PER_PASS = 8
LEVELS = (64, 32, 16, 8, 4, 2)
MATMUL_LEVELS = (4, 2)
VMEM_LIMIT = 56 * 1024 * 1024


def _dot(a, b):
    return jnp.dot(a, b, preferred_element_type=F32)


def _dot_nt(a, b):
    return lax.dot_general(a, b, (((1,), (1,)), ((), ())), preferred_element_type=F32)


def _dot_tn(a, b):
    return lax.dot_general(a, b, (((0,), (0,)), ((), ())), preferred_element_type=F32)


def _rms(x):
    return x * lax.rsqrt(jnp.mean(x * x, axis=-1, keepdims=True) + EPS)


def _split_bf16(x, parts):
    out = []
    for _ in range(parts - 1):
        p = x.astype(BF16)
        out.append(p)
        x = x - p.astype(F32)
    out.append(x.astype(BF16))
    return out


def _ada_tile(c_ref, w_ref, b_ref):
    rows = c_ref.shape[0]
    ca = jnp.concatenate(_split_bf16(jax.nn.silu(c_ref[...]), 2), axis=0)
    acc = _dot(ca, w_ref[...].astype(BF16))
    return acc[:rows] + acc[rows:] + b_ref[...]


def _ada_kernel(c_ref, w_ref, b_ref, o_ref):
    o_ref[...] = _ada_tile(c_ref, w_ref, b_ref)


def _ada(c, w_ada, b_ada, ncols, tn=1024):
    _, d, _ = w_ada.shape
    rows = c.shape[0]
    return pl.pallas_call(
        _ada_kernel,
        grid=(ncols // tn,),
        in_specs=[pl.BlockSpec((rows, d), lambda j: (0, 0)),
                  pl.BlockSpec((None, d, tn), lambda j: (0, 0, j)),
                  pl.BlockSpec((1, tn), lambda j: (0, j))],
        out_specs=pl.BlockSpec((rows, tn), lambda j: (0, j)),
        out_shape=jax.ShapeDtypeStruct((rows, ncols), F32),
        compiler_params=pltpu.CompilerParams(dimension_semantics=("arbitrary",), vmem_limit_bytes=VMEM_LIMIT),
        name="ada",
    )(c, w_ada, b_ada)


def _gelu(x):
    hx = 0.5 * x
    return hx * jnp.tanh(x * (GELU_C1 + GELU_C3 * (x * x))) + hx


def _first_kernel(x_hbm, sh_ref, sc_ref, g_ref, w_hbm, c_ref, wada_ref, bada_ref, zg_ref, h_ref, modb_ref, wb_ref,
                  xbuf_ref, xsem, wstage_ref, wsem, *, ncol):
    i, r, j = pl.program_id(0), pl.program_id(1), pl.program_id(2)
    nr = pl.num_programs(1)
    tm = xbuf_ref.shape[1]
    t = i * nr + r
    total = pl.num_programs(0) * nr
    first = t == 0
    tn = zg_ref.shape[1]

    def x_copy(tile, slot):
        rows = pl.ds(pl.multiple_of((tile % nr) * tm, tm), tm)
        return pltpu.make_async_copy(x_hbm.at[tile // nr, rows, :], xbuf_ref.at[slot], xsem.at[slot])

    def w_copy(jj):
        return pltpu.make_async_copy(w_hbm.at[0, :, pl.ds(jj * tn, tn)], wstage_ref.at[jj], wsem.at[jj])

    @pl.when(first & (j == 0))
    def _():
        for jj in range(ncol):
            w_copy(jj).start()
        x_copy(t, 0).start()

    @pl.when(j == 0)
    def _():
        x_copy(t, t % 2).wait()

    @pl.when((j == 0) & (t + 1 < total))
    def _():
        x_copy(t + 1, (t + 1) % 2).start()

    def tile(jj):
        modb_ref[...] = _ada_tile(c_ref, wada_ref, bada_ref)
        if jj == 0:
            row = pl.ds(i, 1)
            scale = g_ref[...] * (1.0 + sc_ref[row, :])
            h_ref[...] = (_rms(xbuf_ref[t % 2]) * scale + sh_ref[row, :]).astype(BF16)
        zg_ref[...] = _gelu(_dot(h_ref[...], wb_ref[jj])).astype(BF16)

    for jj in range(ncol):
        @pl.when(first & (j == jj))
        def _(jj=jj):
            w_copy(jj).wait()
            wb_ref[jj] = wstage_ref[jj].astype(BF16)

        pl.when(j == jj)(functools.partial(tile, jj))


def _first(x, mod_a, g, w, c, w_ada, b_ada, ncol=2, tm=512, tn=1024):
    b, s, d = x.shape
    nr = s // tm
    steps = b * nr * ncol
    done = mod_a.shape[1]
    slab = (w_ada.shape[2] - done) // steps
    assert slab * steps == w_ada.shape[2] - done and slab % 128 == 0 and done % slab == 0
    step = lambda i, r, j: (i * nr + r) * ncol + j
    return pl.pallas_call(
        functools.partial(_first_kernel, ncol=ncol),
        grid=(b, nr, ncol),
        in_specs=[pl.BlockSpec(memory_space=pl.ANY),
                  pl.BlockSpec((b, d), lambda i, r, j: (0, SH1)),
                  pl.BlockSpec((b, d), lambda i, r, j: (0, SC1)),
                  pl.BlockSpec((1, d), lambda i, r, j: (0, 0)),
                  pl.BlockSpec(memory_space=pl.ANY),
                  pl.BlockSpec(c.shape, lambda i, r, j: (0, 0)),
                  pl.BlockSpec((None, d, slab), lambda i, r, j: (0, 0, done // slab + step(i, r, j))),
                  pl.BlockSpec((1, slab), lambda i, r, j: (0, done // slab + step(i, r, j)))],
        out_specs=[pl.BlockSpec((tm, tn), lambda i, r, j: (i * nr + r, j)),
                   pl.BlockSpec((tm, d), lambda i, r, j: (i * nr + r, 0)),
                   pl.BlockSpec((c.shape[0], slab), lambda i, r, j: (0, step(i, r, j)))],
        out_shape=[jax.ShapeDtypeStruct((b * s, ncol * tn), BF16), jax.ShapeDtypeStruct((b * s, d), BF16),
                   jax.ShapeDtypeStruct((c.shape[0], w_ada.shape[2] - done), F32)],
        scratch_shapes=[pltpu.VMEM((ncol, d, tn), BF16), pltpu.VMEM((2, tm, d), F32), pltpu.SemaphoreType.DMA((2,)),
                        pltpu.VMEM((ncol, d, tn), F32), pltpu.SemaphoreType.DMA((ncol,))],
        compiler_params=pltpu.CompilerParams(dimension_semantics=("arbitrary", "arbitrary", "arbitrary"),
                                             vmem_limit_bytes=VMEM_LIMIT),
        name="first",
    )(x, mod_a, mod_a, g, w, c, w_ada, b_ada)


IN_TILES = ((2, "silu"), (5, "silu"), (6, "gate"), (7, "gate"), (8, "gate"), (9, "gate"), (4, "none"), (3, "logf"))
Z_Q, Z_OG, Z_GATE, Z_IV = 0, 1, 2, 6


def _inproj_kernel(h_ref, w_ref, bg_ref, lb_ref, z_ref, lf_ref, wb_ref):
    j = pl.program_id(0)

    @pl.when(pl.program_id(1) == 0)
    def _():
        wb_ref[...] = w_ref[...].astype(BF16)

    def tile(act):
        acc = _dot(h_ref[...], wb_ref[...])
        if act == "silu":
            hx = 0.5 * acc
            z_ref[...] = (hx * jnp.tanh(hx) + hx).astype(BF16)
        elif act == "gate":
            z_ref[...] = (0.5 * jnp.tanh(0.5 * (acc + bg_ref[...])) + 0.5).astype(BF16)
        elif act == "none":
            z_ref[...] = acc.astype(BF16)
        else:
            lb = jax.nn.softmax(lb_ref[...], axis=0)[0:1, :]
            lf_ref[...] = jnp.log((0.5 + 0.5 * lb) + (0.5 - 0.5 * lb) * jnp.tanh(0.5 * acc))

    acts = [a for _, a in IN_TILES]
    for act in dict.fromkeys(acts):
        first = acts.index(act)
        last = len(acts) - 1 - acts[::-1].index(act)
        pl.when((j >= first) & (j <= last))(functools.partial(tile, act))


def _inproj(h, w, b_gate, hg_lb, tm=1024, tn=1024):
    m, d = h.shape
    ni, nj = m // tm, len(IN_TILES)
    acts = [a for _, a in IN_TILES]
    assert acts[-1] == "logf" and acts.count("logf") == 1
    gate0 = acts.index("gate")
    ngate = acts.count("gate")

    def wcol(j):
        col = j
        for k, (c, _) in enumerate(IN_TILES):
            col = jnp.where(j == k, c, col)
        return col

    last = nj - 1
    return pl.pallas_call(
        _inproj_kernel,
        grid=(nj, ni),
        in_specs=[pl.BlockSpec((tm, d), lambda j, i: (i, 0)),
                  pl.BlockSpec((None, d, tn), lambda j, i: (0, 0, wcol(j))),
                  pl.BlockSpec((1, tn), lambda j, i: (0, jnp.clip(j - gate0, 0, ngate - 1))),
                  pl.BlockSpec(hg_lb.shape, lambda j, i: (0, 0))],
        out_specs=[pl.BlockSpec((tm, tn), lambda j, i: (jnp.where(j < last, i, ni - 1), jnp.minimum(j, last - 1))),
                   pl.BlockSpec((tm, tn), lambda j, i: (jnp.where(j < last, 0, i), 0))],
        out_shape=[jax.ShapeDtypeStruct((m, (nj - 1) * tn), BF16), jax.ShapeDtypeStruct((m, tn), F32)],
        scratch_shapes=[pltpu.VMEM((d, tn), BF16)],
        compiler_params=pltpu.CompilerParams(dimension_semantics=("arbitrary", "arbitrary"),
                                             vmem_limit_bytes=VMEM_LIMIT),
        name="inproj",
    )(h, w, b_gate, hg_lb)


def _level_masks():
    t = np.arange(CHUNK)[:, None]
    s = np.arange(CHUNK)[None, :]
    out = [(t // (2 * m) == s // (2 * m)) & ((t // m) % 2 == 1) & ((s // m) % 2 == 0) for m in LEVELS]
    return np.stack(out).astype(np.float32)


def _decay_matrix():
    t = np.arange(CHUNK)[:, None]
    j = np.arange(CHUNK)[None, :]
    blocks = [j <= t]
    for m in MATMUL_LEVELS:
        r = (t // (2 * m)) * (2 * m) + m - 1
        blocks.append(np.where((t // m) % 2 == 1, (j > r) & (j <= t), (j > t) & (j <= r)))
    blocks.append(j > t)
    w = np.concatenate(blocks, axis=0).astype(np.float32)
    return np.concatenate([w, w], axis=1)


def _mixer_kernel(u_ref, v_ref, q_ref, og_ref, lf_ref, iv_ref, lng_ref, lnb_ref, ws_ref, bs_ref, ng_ref,
                  cw_ref, msk_ref, *rest, n_cast):
    src, (ya_ref, yb_ref), dst, st_ref = rest[:n_cast], rest[n_cast:n_cast + 2], rest[n_cast + 2:-1], rest[-1]
    rows = u_ref.shape[0]
    nchunk = rows // CHUNK
    ngroup = u_ref.shape[1] // GROUP

    @pl.when(pl.program_id(1) == 0)
    def _():
        st_ref[...] = jnp.zeros_like(st_ref)

    for s_ref, d_ref in zip(src, dst):
        d_ref[...] = s_ref[...].astype(BF16)

    v = v_ref[...].astype(F32)
    mu = jnp.mean(v, axis=-1, keepdims=True)
    vc = v - mu
    var = jnp.mean(vc * vc, axis=-1, keepdims=True)
    vn = (vc * lax.rsqrt(var + EPS) * lng_ref[...] + lnb_ref[...]).astype(BF16)
    tri = lax.broadcasted_iota(jnp.int32, (CHUNK, CHUNK), 0) >= lax.broadcasted_iota(jnp.int32, (CHUNK, CHUNK), 1)
    bst = bs_ref[...].T
    for g in range(ngroup):
        cols = slice(g * GROUP, (g + 1) * GROUP)
        w = jnp.where(tri, ws_ref[g], 0.0).astype(BF16)
        rhs = jnp.concatenate([vn[c * CHUNK:(c + 1) * CHUNK, cols] for c in range(nchunk)], axis=1)
        sg = _dot(w, rhs) + bst[:, g:g + 1]
        for c in range(nchunk):
            r = slice(c * CHUNK, (c + 1) * CHUNK)
            ya_ref[r, cols] = (u_ref[r, cols].astype(F32) * sg[:, c * CHUNK:(c + 1) * CHUNK]).astype(BF16)

    nlev = len(LEVELS)
    odd = lax.broadcasted_iota(jnp.int32, (CHUNK, GROUP), 0) % 2 == 1
    for c in range(nchunk):
        r = slice(c * CHUNK, (c + 1) * CHUNK)
        lf2 = lf_ref[r, :] * LOG2E
        ex = _dot(cw_ref[...], jnp.concatenate(_split_bf16(lf2, 2), axis=0))
        for h0 in range(0, ngroup, HEADS_PER_PASS):
            hs = range(h0, h0 + HEADS_PER_PASS)
            cs = {h: slice(h * GROUP, (h + 1) * GROUP) for h in hs}
            blk = lambda h, i: ex[i * CHUNK:(i + 1) * CHUNK, cs[h]]
            nmm = len(MATMUL_LEVELS)

            def level_decay(h, m):
                if m in MATMUL_LEVELS:
                    return jnp.exp2(blk(h, 1 + MATMUL_LEVELS.index(m))).astype(BF16)
                b3 = blk(h, 0).reshape(CHUNK // (2 * m), 2 * m, GROUP)
                dl = (b3 - b3[:, m - 1:m, :]).reshape(CHUNK, GROUP)
                return jnp.exp2(-jnp.abs(dl)).astype(BF16)

            qb = {h: q_ref[r, cs[h]] for h in hs}
            kb = {h: (1.0 - jnp.exp2(lf2[:, cs[h]])).astype(BF16) for h in hs}
            st = {h: st_ref[h] for h in hs}
            o = {h: _dot_nt(qb[h] * jnp.exp2(blk(h, 0)).astype(BF16), st[h].astype(BF16)) for h in hs}
            a = {}
            for li in range(nlev):
                for h in hs:
                    e = level_decay(h, LEVELS[li])
                    p = msk_ref[li] * _dot_nt(qb[h] * e, kb[h] * e)
                    a[h] = p if li == 0 else a[h] + p
            for h in hs:
                iv = iv_ref[r, cs[h]]
                ivf = iv.astype(F32)
                qf = qb[h].astype(F32)
                fh = jnp.exp2(lf2[:, cs[h]])
                kf = 1.0 - fh
                c0 = jnp.sum(qf * kf, axis=-1, keepdims=True)
                c1 = jnp.sum(jnp.where(odd, qf * fh * pltpu.roll(kf, 1, 0), 0.0), axis=-1, keepdims=True)
                oh = o[h] + _dot(a[h].astype(BF16), iv) + c0 * ivf + c1 * pltpu.roll(ivf, 1, 0)
                b_last = ex[CHUNK - 1:CHUNK, cs[h]]
                st_ref[h] = st[h] * jnp.exp2(b_last) + _dot_tn(iv, kb[h] * jnp.exp2(blk(h, 1 + nmm)).astype(BF16))
                y = _rms(oh) * ng_ref[:, cs[h]] * og_ref[r, cs[h]].astype(F32)
                yb_ref[r, cs[h]] = y.astype(BF16)


def _mixer(zg, z, lf, ln_g, ln_b, ws, bs, norm_g, batch, casts, rows=256):
    m, width = lf.shape
    nr = m // batch // rows
    nsteps = batch * nr
    ngroup = width // GROUP
    cw = jnp.asarray(_decay_matrix(), BF16)
    msk = jnp.asarray(_level_masks(), F32)
    row_blk = lambda col: pl.BlockSpec((rows, width), lambda b, r: (b * nr + r, col))
    full = lambda a: pl.BlockSpec(a.shape, lambda b, r: (0,) * a.ndim)
    in_specs = [row_blk(0), row_blk(1), row_blk(Z_Q), row_blk(Z_OG), row_blk(0), row_blk(Z_IV),
                full(ln_g), full(ln_b), full(ws), full(bs), full(norm_g), full(cw), full(msk)]
    out_specs = [row_blk(0), row_blk(0)]
    out_shape = [jax.ShapeDtypeStruct((m, width), BF16), jax.ShapeDtypeStruct((m, width), BF16)]
    for a in casts:
        _, n, w = a.shape
        slab = n // nsteps
        assert slab * nsteps == n and slab % 16 == 0
        in_specs.append(pl.BlockSpec((None, slab, w), lambda b, r: (0, b * nr + r, 0)))
        out_specs.append(pl.BlockSpec((slab, w), lambda b, r: (b * nr + r, 0)))
        out_shape.append(jax.ShapeDtypeStruct((n, w), BF16))
    return pl.pallas_call(
        functools.partial(_mixer_kernel, n_cast=len(casts)),
        grid=(batch, nr),
        in_specs=in_specs,
        out_specs=out_specs,
        out_shape=out_shape,
        scratch_shapes=[pltpu.VMEM((ngroup, GROUP, GROUP), F32)],
        compiler_params=pltpu.CompilerParams(dimension_semantics=("arbitrary", "arbitrary"),
                                             vmem_limit_bytes=VMEM_LIMIT),
        name="mixer",
    )(zg, zg, z, z, lf, z, ln_g, ln_b, ws, bs, norm_g, cw, msk, *casts)


def _merge_kernel(ya_ref, yb_ref, ga_ref, gb_ref, x_ref, gt_ref, sh_ref, sc_ref, wa_ref, wb_ref, wo_ref, g_ref, x1_ref,
                  h2_ref):
    tm = ya_ref.shape[0]
    row = pl.ds(pl.program_id(0), 1)
    scale = g_ref[...] * (1.0 + sc_ref[row, :])
    gate, shift = gt_ref[row, :], sh_ref[row, :]
    for p in range(MERGE_PARTS):
        r = slice(p * tm // MERGE_PARTS, (p + 1) * tm // MERGE_PARTS)
        pa = _dot(ya_ref[r, :], wa_ref[...])
        pb = _dot(yb_ref[r, :], wb_ref[...])
        y = (ga_ref[r, :].astype(F32) * pa + gb_ref[r, :].astype(F32) * pb).astype(BF16)
        x1 = x_ref[0, r, :] + gate * _dot(y, wo_ref[...])
        x1_ref[0, r, :] = x1
        h2_ref[0, r, :] = (_rms(x1) * scale + shift).astype(BF16)


def _merge(ya, yb, z, x, mod_b, wa, wb, wo, g, tm=512):
    b, s, d = x.shape
    nr = s // tm
    width = ya.shape[1]
    resident = lambda a: pl.BlockSpec(a.shape, lambda i, r: (0,) * a.ndim, pipeline_mode=pl.Buffered(1))
    return pl.pallas_call(
        _merge_kernel,
        grid=(b, nr),
        in_specs=[pl.BlockSpec((tm, width), lambda i, r: (i * nr + r, 0)),
                  pl.BlockSpec((tm, width), lambda i, r: (i * nr + r, 0)),
                  pl.BlockSpec((tm, d), lambda i, r: (i * nr + r, Z_GATE * width // d)),
                  pl.BlockSpec((tm, d), lambda i, r: (i * nr + r, Z_GATE * width // d + 1)),
                  pl.BlockSpec((1, tm, d), lambda i, r: (i, r, 0)),
                  pl.BlockSpec((b, d), lambda i, r: (0, GT1)),
                  pl.BlockSpec((b, d), lambda i, r: (0, SH2)),
                  pl.BlockSpec((b, d), lambda i, r: (0, SC2)),
                  resident(wa), resident(wb), resident(wo),
                  pl.BlockSpec((1, d), lambda i, r: (0, 0))],
        out_specs=[pl.BlockSpec((1, tm, d), lambda i, r: (i, r, 0)),
                   pl.BlockSpec((1, tm, d), lambda i, r: (i, r, 0))],
        out_shape=[jax.ShapeDtypeStruct((b, s, d), F32), jax.ShapeDtypeStruct((b, s, d), BF16)],
        compiler_params=pltpu.CompilerParams(dimension_semantics=("arbitrary", "arbitrary"),
                                             vmem_limit_bytes=VMEM_LIMIT),
        name="merge",
    )(ya, yb, z, z, x, mod_b, mod_b, mod_b, wa, wb, wo, g)


def _ffn_kernel(h_ref, wa_ref, wu_ref, wo_ref, x1_ref, mod_ref, g_ref, o_ref, acc_ref, *, nt):
    t = pl.program_id(2)

    def hidden_tile():
        h = h_ref[0]
        a = _dot(h, wa_ref[...])
        up = _dot(h, wu_ref[...])
        ha = 0.5 * a
        return _dot(((ha * jnp.tanh(ha) + ha) * up).astype(BF16), wo_ref[...])

    @pl.when(t == 0)
    def _():
        acc_ref[...] = hidden_tile()

    @pl.when((t > 0) & (t < nt))
    def _():
        acc_ref[...] += hidden_tile()

    @pl.when(t >= nt)
    def _():
        half = o_ref.shape[1]
        base = (t - nt) * half
        gate = mod_ref[pl.ds(pl.program_id(0), 1), :]
        g = g_ref[...]

        def body(i, carry):
            rows = pl.ds(pl.multiple_of(i * ROW_GROUP, ROW_GROUP), ROW_GROUP)
            arows = pl.ds(pl.multiple_of(base + i * ROW_GROUP, ROW_GROUP), ROW_GROUP)
            o_ref[0, rows, :] = _rms(x1_ref[0, rows, :] + gate * acc_ref[arows, :]) * g
            return carry

        lax.fori_loop(0, half // ROW_GROUP, body, 0, unroll=8)


def _ffn(h2, w_in, w_out, x1, mod_b, g, tm=1024, th=512):
    b, s, d = x1.shape
    hidden = w_out.shape[0]
    nt = hidden // th
    half = tm // 2
    nr = s // tm
    wt = lambda t: jnp.where(t < nt, t, 0)
    fin = lambda r, t: 2 * r + jnp.clip(t - nt, 0, 1)

    def h2_blk(i, r, t):
        lin = jnp.minimum(i * nr + r + (t >= nt).astype(jnp.int32), b * nr - 1)
        return (lin // nr, lin % nr, 0)

    return pl.pallas_call(
        functools.partial(_ffn_kernel, nt=nt),
        grid=(b, s // tm, nt + 2),
        in_specs=[pl.BlockSpec((1, tm, d), h2_blk),
                  pl.BlockSpec((d, th), lambda i, r, t: (0, wt(t))),
                  pl.BlockSpec((d, th), lambda i, r, t: (0, nt + wt(t))),
                  pl.BlockSpec((th, d), lambda i, r, t: (wt(t), 0)),
                  pl.BlockSpec((1, half, d), lambda i, r, t: (i, fin(r, t), 0)),
                  pl.BlockSpec((b, d), lambda i, r, t: (0, GT2)),
                  pl.BlockSpec((1, d), lambda i, r, t: (0, 0))],
        out_specs=pl.BlockSpec((1, half, d), lambda i, r, t: (i, fin(r, t), 0)),
        out_shape=jax.ShapeDtypeStruct((b, s, d), F32),
        scratch_shapes=[pltpu.VMEM((tm, d), F32)],
        compiler_params=pltpu.CompilerParams(dimension_semantics=("arbitrary", "arbitrary", "arbitrary"),
                                             vmem_limit_bytes=VMEM_LIMIT),
        name="ffn",
    )(h2, w_in, w_in, w_out, x1, mod_b, g)


def kernel(x, c, w_ada, b_ada, norm1_g, w_in, b_gate, gmlp_ln_g, gmlp_ln_b, gmlp_ws, gmlp_bs, hg_lb, hg_norm_g,
           w_branch_gmlp, w_branch_hg, w_out, norm2_g, w_ffn_in, w_ffn_out, final_norm_g):
    batch, seq, d = x.shape
    depth = w_ada.shape[0]
    width = w_branch_gmlp.shape[1]
    assert depth == 1 and width == 8 * GROUP and gmlp_ws.shape[2] == CHUNK and seq % 256 == 0
    assert w_in.shape[2] == 6 * width + 2 * d and hg_lb.shape[0] == depth + 1
    tn = 1024
    assert width == tn and d == 2 * tn

    mod_a = _ada(c, w_ada, b_ada, 2 * d)
    zg, h1, mod_b = _first(x, mod_a, norm1_g, w_in, c, w_ada, b_ada)
    z, lf = _inproj(h1, w_in, b_gate, hg_lb)

    ya, yb, w_a, w_b, w_o, w_fi, w_fo = _mixer(
        zg, z, lf, gmlp_ln_g, gmlp_ln_b, gmlp_ws[0], gmlp_bs[0], hg_norm_g, batch,
        casts=(w_branch_gmlp, w_branch_hg, w_out, w_ffn_in, w_ffn_out))

    x1, h2 = _merge(ya, yb, z, x, mod_b, w_a, w_b, w_o, norm2_g)
    return _ffn(h2, w_fi, w_fo, x1, mod_b, final_norm_g.reshape(1, d))
```

```python
import functools

import numpy as np
import jax
import jax.numpy as jnp
from jax import lax
from jax.experimental import pallas as pl
from jax.experimental.pallas import tpu as pltpu

F32 = jnp.float32
BF16 = jnp.bfloat16
EPS = 1e-6
LOG2E = 1.4426950408889634
GELU_C1 = 0.7978845608028654
GELU_C3 = 0.044715 * GELU_C1

SH1, SC1 = 0, 1
GT1, SH2, SC2, GT2 = 0, 1, 2, 3

GROUP = 128
CHUNK = 128
MERGE_PARTS = 2
ROW_GROUP = 16
HEADS_PER_PASS = 8
LEVELS = (64, 32, 16, 8, 4, 2)
MATMUL_LEVELS = (4, 2)
VMEM_LIMIT = 56 * 1024 * 1024


def _dot(a, b):
    return jnp.dot(a, b, preferred_element_type=F32)


def _dot_nt(a, b):
    return lax.dot_general(a, b, (((1,), (1,)), ((), ())), preferred_element_type=F32)


def _dot_tn(a, b):
    return lax.dot_general(a, b, (((0,), (0,)), ((), ())), preferred_element_type=F32)


def _rms(x):
    return x * lax.rsqrt(jnp.mean(x * x, axis=-1, keepdims=True) + EPS)


def _split_bf16(x, parts):
    out = []
    for _ in range(parts - 1):
        p = x.astype(BF16)
        out.append(p)
        x = x - p.astype(F32)
    out.append(x.astype(BF16))
    return out


def _ada_tile(c_ref, w_ref, b_ref):
    rows = c_ref.shape[0]
    ca = jnp.concatenate(_split_bf16(jax.nn.silu(c_ref[...]), 2), axis=0)
    acc = _dot(ca, w_ref[...].astype(BF16))
    return acc[:rows] + acc[rows:] + b_ref[...]


def _ada_kernel(c_ref, w_ref, b_ref, o_ref):
    o_ref[...] = _ada_tile(c_ref, w_ref, b_ref)


def _ada(c, w_ada, b_ada, ncols, tn=1024):
    _, d, _ = w_ada.shape
    rows = c.shape[0]
    return pl.pallas_call(
        _ada_kernel,
        grid=(ncols // tn,),
        in_specs=[pl.BlockSpec((rows, d), lambda j: (0, 0)),
                  pl.BlockSpec((None, d, tn), lambda j: (0, 0, j)),
                  pl.BlockSpec((1, tn), lambda j: (0, j))],
        out_specs=pl.BlockSpec((rows, tn), lambda j: (0, j)),
        out_shape=jax.ShapeDtypeStruct((rows, ncols), F32),
        compiler_params=pltpu.CompilerParams(dimension_semantics=("arbitrary",), vmem_limit_bytes=VMEM_LIMIT),
        name="ada",
    )(c, w_ada, b_ada)


def _gelu(x):
    hx = 0.5 * x
    return hx * jnp.tanh(x * (GELU_C1 + GELU_C3 * (x * x))) + hx


def _first_kernel(x_hbm, sh_ref, sc_ref, g_ref, w_ref, c_ref, wada_ref, bada_ref, zg_ref, h_ref, modb_ref, wb_ref,
                  xbuf_ref, xsem, *, ncol):
    i, r, j = pl.program_id(0), pl.program_id(1), pl.program_id(2)
    nr = pl.num_programs(1)
    tm = xbuf_ref.shape[1]
    t = i * nr + r
    total = pl.num_programs(0) * nr
    first = t == 0
    tn = zg_ref.shape[1]

    def x_copy(tile, slot):
        rows = pl.ds(pl.multiple_of((tile % nr) * tm, tm), tm)
        return pltpu.make_async_copy(x_hbm.at[tile // nr, rows, :], xbuf_ref.at[slot], xsem.at[slot])

    @pl.when(first & (j == 0))
    def _():
        x_copy(t, 0).start()

    @pl.when(j == 0)
    def _():
        x_copy(t, t % 2).wait()

    @pl.when((j == 0) & (t + 1 < total))
    def _():
        x_copy(t + 1, (t + 1) % 2).start()

    def tile(jj):
        modb_ref[...] = _ada_tile(c_ref, wada_ref, bada_ref)
        if jj == 0:
            row = pl.ds(i, 1)
            scale = g_ref[...] * (1.0 + sc_ref[row, :])
            h_ref[...] = (_rms(xbuf_ref[t % 2]) * scale + sh_ref[row, :]).astype(BF16)
        zg_ref[...] = _gelu(_dot(h_ref[...], wb_ref[jj])).astype(BF16)

    for jj in range(ncol):
        @pl.when(first & (j == jj))
        def _(jj=jj):
            wb_ref[jj] = w_ref[:, jj * tn:(jj + 1) * tn].astype(BF16)

        pl.when(j == jj)(functools.partial(tile, jj))


def _first(x, mod_a, g, w, c, w_ada, b_ada, ncol=2, tm=512, tn=1024):
    b, s, d = x.shape
    nr = s // tm
    steps = b * nr * ncol
    done = mod_a.shape[1]
    slab = (w_ada.shape[2] - done) // steps
    assert slab * steps == w_ada.shape[2] - done and slab % 128 == 0 and done % slab == 0
    step = lambda i, r, j: (i * nr + r) * ncol + j
    return pl.pallas_call(
        functools.partial(_first_kernel, ncol=ncol),
        grid=(b, nr, ncol),
        in_specs=[pl.BlockSpec(memory_space=pl.ANY),
                  pl.BlockSpec((b, d), lambda i, r, j: (0, SH1)),
                  pl.BlockSpec((b, d), lambda i, r, j: (0, SC1)),
                  pl.BlockSpec((1, d), lambda i, r, j: (0, 0)),
                  pl.BlockSpec((None, d, ncol * tn), lambda i, r, j: (0, 0, 0), pipeline_mode=pl.Buffered(1)),
                  pl.BlockSpec(c.shape, lambda i, r, j: (0, 0)),
                  pl.BlockSpec((None, d, slab), lambda i, r, j: (0, 0, done // slab + step(i, r, j))),
                  pl.BlockSpec((1, slab), lambda i, r, j: (0, done // slab + step(i, r, j)))],
        out_specs=[pl.BlockSpec((tm, tn), lambda i, r, j: (i * nr + r, j)),
                   pl.BlockSpec((tm, d), lambda i, r, j: (i * nr + r, 0)),
                   pl.BlockSpec((c.shape[0], slab), lambda i, r, j: (0, step(i, r, j)))],
        out_shape=[jax.ShapeDtypeStruct((b * s, ncol * tn), BF16), jax.ShapeDtypeStruct((b * s, d), BF16),
                   jax.ShapeDtypeStruct((c.shape[0], w_ada.shape[2] - done), F32)],
        scratch_shapes=[pltpu.VMEM((ncol, d, tn), BF16), pltpu.VMEM((2, tm, d), F32), pltpu.SemaphoreType.DMA((2,))],
        compiler_params=pltpu.CompilerParams(dimension_semantics=("arbitrary", "arbitrary", "arbitrary"),
                                             vmem_limit_bytes=VMEM_LIMIT),
        name="first",
    )(x, mod_a, mod_a, g, w, c, w_ada, b_ada)


IN_TILES = ((2, "silu"), (5, "silu"), (6, "gate"), (7, "gate"), (8, "gate"), (9, "gate"), (4, "none"), (3, "logf"))
Z_Q, Z_OG, Z_GATE, Z_IV = 0, 1, 2, 6


def _inproj_kernel(h_ref, w_ref, bg_ref, lb_ref, z_ref, lf_ref, wb_ref):
    j = pl.program_id(0)

    @pl.when(pl.program_id(1) == 0)
    def _():
        wb_ref[...] = w_ref[...].astype(BF16)

    def tile(act):
        acc = _dot(h_ref[...], wb_ref[...])
        if act == "silu":
            hx = 0.5 * acc
            z_ref[...] = (hx * jnp.tanh(hx) + hx).astype(BF16)
        elif act == "gate":
            z_ref[...] = (0.5 * jnp.tanh(0.5 * (acc + bg_ref[...])) + 0.5).astype(BF16)
        elif act == "none":
            z_ref[...] = acc.astype(BF16)
        else:
            lb = jax.nn.softmax(lb_ref[...], axis=0)[0:1, :]
            lf_ref[...] = jnp.log((0.5 + 0.5 * lb) + (0.5 - 0.5 * lb) * jnp.tanh(0.5 * acc))

    acts = [a for _, a in IN_TILES]
    for act in dict.fromkeys(acts):
        first = acts.index(act)
        last = len(acts) - 1 - acts[::-1].index(act)
        pl.when((j >= first) & (j <= last))(functools.partial(tile, act))


def _inproj(h, w, b_gate, hg_lb, tm=1024, tn=1024):
    m, d = h.shape
    ni, nj = m // tm, len(IN_TILES)
    acts = [a for _, a in IN_TILES]
    assert acts[-1] == "logf" and acts.count("logf") == 1
    gate0 = acts.index("gate")
    ngate = acts.count("gate")

    def wcol(j):
        col = j
        for k, (c, _) in enumerate(IN_TILES):
            col = jnp.where(j == k, c, col)
        return col

    last = nj - 1
    return pl.pallas_call(
        _inproj_kernel,
        grid=(nj, ni),
        in_specs=[pl.BlockSpec((tm, d), lambda j, i: (i, 0)),
                  pl.BlockSpec((None, d, tn), lambda j, i: (0, 0, wcol(j))),
                  pl.BlockSpec((1, tn), lambda j, i: (0, jnp.clip(j - gate0, 0, ngate - 1))),
                  pl.BlockSpec(hg_lb.shape, lambda j, i: (0, 0))],
        out_specs=[pl.BlockSpec((tm, tn), lambda j, i: (jnp.where(j < last, i, ni - 1), jnp.minimum(j, last - 1))),
                   pl.BlockSpec((tm, tn), lambda j, i: (jnp.where(j < last, 0, i), 0))],
        out_shape=[jax.ShapeDtypeStruct((m, (nj - 1) * tn), BF16), jax.ShapeDtypeStruct((m, tn), F32)],
        scratch_shapes=[pltpu.VMEM((d, tn), BF16)],
        compiler_params=pltpu.CompilerParams(dimension_semantics=("arbitrary", "arbitrary"),
                                             vmem_limit_bytes=VMEM_LIMIT),
        name="inproj",
    )(h, w, b_gate, hg_lb)


def _level_masks():
    t = np.arange(CHUNK)[:, None]
    s = np.arange(CHUNK)[None, :]
    out = [(t // (2 * m) == s // (2 * m)) & ((t // m) % 2 == 1) & ((s // m) % 2 == 0) for m in LEVELS]
    return np.stack(out).astype(np.float32)


def _decay_matrix():
    t = np.arange(CHUNK)[:, None]
    j = np.arange(CHUNK)[None, :]
    blocks = [j <= t]
    for m in MATMUL_LEVELS:
        r = (t // (2 * m)) * (2 * m) + m - 1
        blocks.append(np.where((t // m) % 2 == 1, (j > r) & (j <= t), (j > t) & (j <= r)))
    blocks.append(j > t)
    w = np.concatenate(blocks, axis=0).astype(np.float32)
    return np.concatenate([w, w], axis=1)


def _mixer_kernel(u_ref, v_ref, q_ref, og_ref, lf_ref, iv_ref, lng_ref, lnb_ref, ws_ref, bs_ref, ng_ref,
                  cw_ref, msk_ref, *rest, n_cast):
    src, (ya_ref, yb_ref), dst, st_ref = rest[:n_cast], rest[n_cast:n_cast + 2], rest[n_cast + 2:-1], rest[-1]
    rows = u_ref.shape[0]
    nchunk = rows // CHUNK
    ngroup = u_ref.shape[1] // GROUP

    @pl.when(pl.program_id(1) == 0)
    def _():
        st_ref[...] = jnp.zeros_like(st_ref)

    for s_ref, d_ref in zip(src, dst):
        d_ref[...] = s_ref[...].astype(BF16)

    v = v_ref[...].astype(F32)
    mu = jnp.mean(v, axis=-1, keepdims=True)
    vc = v - mu
    var = jnp.mean(vc * vc, axis=-1, keepdims=True)
    vn = (vc * lax.rsqrt(var + EPS) * lng_ref[...] + lnb_ref[...]).astype(BF16)
    tri = lax.broadcasted_iota(jnp.int32, (CHUNK, CHUNK), 0) >= lax.broadcasted_iota(jnp.int32, (CHUNK, CHUNK), 1)
    bst = bs_ref[...].T
    for g in range(ngroup):
        cols = slice(g * GROUP, (g + 1) * GROUP)
        w = jnp.where(tri, ws_ref[g], 0.0).astype(BF16)
        rhs = jnp.concatenate([vn[c * CHUNK:(c + 1) * CHUNK, cols] for c in range(nchunk)], axis=1)
        sg = _dot(w, rhs) + bst[:, g:g + 1]
        for c in range(nchunk):
            r = slice(c * CHUNK, (c + 1) * CHUNK)
            ya_ref[r, cols] = (u_ref[r, cols].astype(F32) * sg[:, c * CHUNK:(c + 1) * CHUNK]).astype(BF16)

    nlev = len(LEVELS)
    odd = lax.broadcasted_iota(jnp.int32, (CHUNK, GROUP), 0) % 2 == 1
    for c in range(nchunk):
        r = slice(c * CHUNK, (c + 1) * CHUNK)
        lf2 = lf_ref[r, :] * LOG2E
        ex = _dot(cw_ref[...], jnp.concatenate(_split_bf16(lf2, 2), axis=0))
        for h0 in range(0, ngroup, HEADS_PER_PASS):
            hs = range(h0, h0 + HEADS_PER_PASS)
            cs = {h: slice(h * GROUP, (h + 1) * GROUP) for h in hs}
            blk = lambda h, i: ex[i * CHUNK:(i + 1) * CHUNK, cs[h]]
            nmm = len(MATMUL_LEVELS)

            def level_decay(h, m):
                if m in MATMUL_LEVELS:
                    return jnp.exp2(blk(h, 1 + MATMUL_LEVELS.index(m))).astype(BF16)
                b3 = blk(h, 0).reshape(CHUNK // (2 * m), 2 * m, GROUP)
                dl = (b3 - b3[:, m - 1:m, :]).reshape(CHUNK, GROUP)
                return jnp.exp2(-jnp.abs(dl)).astype(BF16)

            qb = {h: q_ref[r, cs[h]] for h in hs}
            kb = {h: (1.0 - jnp.exp2(lf2[:, cs[h]])).astype(BF16) for h in hs}
            st = {h: st_ref[h] for h in hs}
            o = {h: _dot_nt(qb[h] * jnp.exp2(blk(h, 0)).astype(BF16), st[h].astype(BF16)) for h in hs}
            a = {}
            for li in range(nlev):
                for h in hs:
                    e = level_decay(h, LEVELS[li])
                    p = msk_ref[li] * _dot_nt(qb[h] * e, kb[h] * e)
                    a[h] = p if li == 0 else a[h] + p
            for h in hs:
                iv = iv_ref[r, cs[h]]
                ivf = iv.astype(F32)
                qf = qb[h].astype(F32)
                fh = jnp.exp2(lf2[:, cs[h]])
                kf = 1.0 - fh
                c0 = jnp.sum(qf * kf, axis=-1, keepdims=True)
                c1 = jnp.sum(jnp.where(odd, qf * fh * pltpu.roll(kf, 1, 0), 0.0), axis=-1, keepdims=True)
                oh = o[h] + _dot(a[h].astype(BF16), iv) + c0 * ivf + c1 * pltpu.roll(ivf, 1, 0)
                b_last = ex[CHUNK - 1:CHUNK, cs[h]]
                st_ref[h] = st[h] * jnp.exp2(b_last) + _dot_tn(iv, kb[h] * jnp.exp2(blk(h, 1 + nmm)).astype(BF16))
                y = _rms(oh) * ng_ref[:, cs[h]] * og_ref[r, cs[h]].astype(F32)
                yb_ref[r, cs[h]] = y.astype(BF16)


def _mixer(zg, z, lf, ln_g, ln_b, ws, bs, norm_g, batch, casts, rows=256):
    m, width = lf.shape
    nr = m // batch // rows
    nsteps = batch * nr
    ngroup = width // GROUP
    cw = jnp.asarray(_decay_matrix(), BF16)
    msk = jnp.asarray(_level_masks(), F32)
    row_blk = lambda col: pl.BlockSpec((rows, width), lambda b, r: (b * nr + r, col))
    full = lambda a: pl.BlockSpec(a.shape, lambda b, r: (0,) * a.ndim)
    in_specs = [row_blk(0), row_blk(1), row_blk(Z_Q), row_blk(Z_OG), row_blk(0), row_blk(Z_IV),
                full(ln_g), full(ln_b), full(ws), full(bs), full(norm_g), full(cw), full(msk)]
    out_specs = [row_blk(0), row_blk(0)]
    out_shape = [jax.ShapeDtypeStruct((m, width), BF16), jax.ShapeDtypeStruct((m, width), BF16)]
    for a in casts:
        _, n, w = a.shape
        slab = n // nsteps
        assert slab * nsteps == n and slab % 16 == 0
        in_specs.append(pl.BlockSpec((None, slab, w), lambda b, r: (0, b * nr + r, 0)))
        out_specs.append(pl.BlockSpec((slab, w), lambda b, r: (b * nr + r, 0)))
        out_shape.append(jax.ShapeDtypeStruct((n, w), BF16))
    return pl.pallas_call(
        functools.partial(_mixer_kernel, n_cast=len(casts)),
        grid=(batch, nr),
        in_specs=in_specs,
        out_specs=out_specs,
        out_shape=out_shape,
        scratch_shapes=[pltpu.VMEM((ngroup, GROUP, GROUP), F32)],
        compiler_params=pltpu.CompilerParams(dimension_semantics=("arbitrary", "arbitrary"),
                                             vmem_limit_bytes=VMEM_LIMIT),
        name="mixer",
    )(zg, zg, z, z, lf, z, ln_g, ln_b, ws, bs, norm_g, cw, msk, *casts)


def _merge_kernel(ya_ref, yb_ref, ga_ref, gb_ref, x_ref, gt_ref, sh_ref, sc_ref, wa_ref, wb_ref, wo_ref, g_ref, x1_ref,
                  h2_ref):
    tm = ya_ref.shape[0]
    row = pl.ds(pl.program_id(0), 1)
    scale = g_ref[...] * (1.0 + sc_ref[row, :])
    gate, shift = gt_ref[row, :], sh_ref[row, :]
    for p in range(MERGE_PARTS):
        r = slice(p * tm // MERGE_PARTS, (p + 1) * tm // MERGE_PARTS)
        pa = _dot(ya_ref[r, :], wa_ref[...])
        pb = _dot(yb_ref[r, :], wb_ref[...])
        y = (ga_ref[r, :].astype(F32) * pa + gb_ref[r, :].astype(F32) * pb).astype(BF16)
        x1 = x_ref[0, r, :] + gate * _dot(y, wo_ref[...])
        x1_ref[0, r, :] = x1
        h2_ref[0, r, :] = (_rms(x1) * scale + shift).astype(BF16)


def _merge(ya, yb, z, x, mod_b, wa, wb, wo, g, tm=512):
    b, s, d = x.shape
    nr = s // tm
    width = ya.shape[1]
    resident = lambda a: pl.BlockSpec(a.shape, lambda i, r: (0,) * a.ndim, pipeline_mode=pl.Buffered(1))
    return pl.pallas_call(
        _merge_kernel,
        grid=(b, nr),
        in_specs=[pl.BlockSpec((tm, width), lambda i, r: (i * nr + r, 0)),
                  pl.BlockSpec((tm, width), lambda i, r: (i * nr + r, 0)),
                  pl.BlockSpec((tm, d), lambda i, r: (i * nr + r, Z_GATE * width // d)),
                  pl.BlockSpec((tm, d), lambda i, r: (i * nr + r, Z_GATE * width // d + 1)),
                  pl.BlockSpec((1, tm, d), lambda i, r: (i, r, 0)),
                  pl.BlockSpec((b, d), lambda i, r: (0, GT1)),
                  pl.BlockSpec((b, d), lambda i, r: (0, SH2)),
                  pl.BlockSpec((b, d), lambda i, r: (0, SC2)),
                  resident(wa), resident(wb), resident(wo),
                  pl.BlockSpec((1, d), lambda i, r: (0, 0))],
        out_specs=[pl.BlockSpec((1, tm, d), lambda i, r: (i, r, 0)),
                   pl.BlockSpec((1, tm, d), lambda i, r: (i, r, 0))],
        out_shape=[jax.ShapeDtypeStruct((b, s, d), F32), jax.ShapeDtypeStruct((b, s, d), BF16)],
        compiler_params=pltpu.CompilerParams(dimension_semantics=("arbitrary", "arbitrary"),
                                             vmem_limit_bytes=VMEM_LIMIT),
        name="merge",
    )(ya, yb, z, z, x, mod_b, mod_b, mod_b, wa, wb, wo, g)


def _ffn_kernel(h_ref, wa_ref, wu_ref, wo_ref, x1_hbm, mod_ref, g_ref, o_ref, acc_ref, x1buf_ref, x1sem, *, nt):
    t = pl.program_id(2)
    half = o_ref.shape[1]

    def x1_copy(k):
        rows = pl.ds(pl.multiple_of((2 * pl.program_id(1) + k) * half, half), half)
        return pltpu.make_async_copy(x1_hbm.at[pl.program_id(0), rows, :], x1buf_ref.at[k], x1sem.at[k])

    @pl.when(t == 0)
    def _():
        x1_copy(0).start()
        x1_copy(1).start()

    for k in range(2):
        @pl.when(t == nt + k)
        def _(k=k):
            x1_copy(k).wait()

    def hidden_tile():
        h = h_ref[0]
        a = _dot(h, wa_ref[...])
        up = _dot(h, wu_ref[...])
        ha = 0.5 * a
        return _dot(((ha * jnp.tanh(ha) + ha) * up).astype(BF16), wo_ref[...])

    @pl.when(t == 0)
    def _():
        acc_ref[...] = hidden_tile()

    @pl.when((t > 0) & (t < nt))
    def _():
        acc_ref[...] += hidden_tile()

    @pl.when(t >= nt)
    def _():
        base = (t - nt) * half
        gate = mod_ref[pl.ds(pl.program_id(0), 1), :]
        g = g_ref[...]

        def body(i, carry):
            rows = pl.ds(pl.multiple_of(i * ROW_GROUP, ROW_GROUP), ROW_GROUP)
            arows = pl.ds(pl.multiple_of(base + i * ROW_GROUP, ROW_GROUP), ROW_GROUP)
            o_ref[0, rows, :] = _rms(x1buf_ref[t - nt, rows, :] + gate * acc_ref[arows, :]) * g
            return carry

        lax.fori_loop(0, half // ROW_GROUP, body, 0, unroll=8)


def _ffn(h2, w_in, w_out, x1, mod_b, g, tm=1024, th=512):
    b, s, d = x1.shape
    hidden = w_out.shape[0]
    nt = hidden // th
    half = tm // 2
    nr = s // tm
    wt = lambda t: jnp.where(t < nt, t, 0)
    fin = lambda r, t: 2 * r + jnp.clip(t - nt, 0, 1)

    def h2_blk(i, r, t):
        lin = jnp.minimum(i * nr + r + (t >= nt).astype(jnp.int32), b * nr - 1)
        return (lin // nr, lin % nr, 0)

    return pl.pallas_call(
        functools.partial(_ffn_kernel, nt=nt),
        grid=(b, s // tm, nt + 2),
        in_specs=[pl.BlockSpec((1, tm, d), h2_blk),
                  pl.BlockSpec((d, th), lambda i, r, t: (0, wt(t))),
                  pl.BlockSpec((d, th), lambda i, r, t: (0, nt + wt(t))),
                  pl.BlockSpec((th, d), lambda i, r, t: (wt(t), 0)),
                  pl.BlockSpec(memory_space=pl.ANY),
                  pl.BlockSpec((b, d), lambda i, r, t: (0, GT2)),
                  pl.BlockSpec((1, d), lambda i, r, t: (0, 0))],
        out_specs=pl.BlockSpec((1, half, d), lambda i, r, t: (i, fin(r, t), 0)),
        out_shape=jax.ShapeDtypeStruct((b, s, d), F32),
        scratch_shapes=[pltpu.VMEM((tm, d), F32), pltpu.VMEM((2, half, d), F32), pltpu.SemaphoreType.DMA((2,))],
        compiler_params=pltpu.CompilerParams(dimension_semantics=("arbitrary", "arbitrary", "arbitrary"),
                                             vmem_limit_bytes=VMEM_LIMIT),
        name="ffn",
    )(h2, w_in, w_in, w_out, x1, mod_b, g)


def kernel(x, c, w_ada, b_ada, norm1_g, w_in, b_gate, gmlp_ln_g, gmlp_ln_b, gmlp_ws, gmlp_bs, hg_lb, hg_norm_g,
           w_branch_gmlp, w_branch_hg, w_out, norm2_g, w_ffn_in, w_ffn_out, final_norm_g):
    batch, seq, d = x.shape
    depth = w_ada.shape[0]
    width = w_branch_gmlp.shape[1]
    assert depth == 1 and width == 8 * GROUP and gmlp_ws.shape[2] == CHUNK and seq % 256 == 0
    assert w_in.shape[2] == 6 * width + 2 * d and hg_lb.shape[0] == depth + 1
    tn = 1024
    assert width == tn and d == 2 * tn

    mod_a = _ada(c, w_ada, b_ada, 2 * d)
    zg, h1, mod_b = _first(x, mod_a, norm1_g, w_in, c, w_ada, b_ada)
    z, lf = _inproj(h1, w_in, b_gate, hg_lb)

    ya, yb, w_a, w_b, w_o, w_fi, w_fo = _mixer(
        zg, z, lf, gmlp_ln_g, gmlp_ln_b, gmlp_ws[0], gmlp_bs[0], hg_norm_g, batch,
        casts=(w_branch_gmlp, w_branch_hg, w_out, w_ffn_in, w_ffn_out))

    x1, h2 = _merge(ya, yb, z, x, mod_b, w_a, w_b, w_o, norm2_g)
    return _ffn(h2, w_fi, w_fo, x1, mod_b, final_norm_g.reshape(1, d))
```

```python
import functools

import numpy as np
import jax
import jax.numpy as jnp
from jax import lax
from jax.experimental import pallas as pl
from jax.experimental.pallas import tpu as pltpu

F32 = jnp.float32
BF16 = jnp.bfloat16
EPS = 1e-6
LOG2E = 1.4426950408889634
GELU_C1 = 0.7978845608028654
GELU_C3 = 0.044715 * GELU_C1

SH1, SC1 = 0, 1
GT1, SH2, SC2, GT2 = 0, 1, 2, 3

GROUP = 128
CHUNK = 128
MERGE_PARTS = 2
ROW_GROUP = 16
HEADS_PER_PASS = 8
LEVELS = (64, 32, 16, 8, 4, 2)
MATMUL_LEVELS = (4, 2)
VMEM_LIMIT = 56 * 1024 * 1024


def _dot(a, b):
    return jnp.dot(a, b, preferred_element_type=F32)


def _dot_nt(a, b):
    return lax.dot_general(a, b, (((1,), (1,)), ((), ())), preferred_element_type=F32)


def _dot_tn(a, b):
    return lax.dot_general(a, b, (((0,), (0,)), ((), ())), preferred_element_type=F32)


def _rms(x):
    return x * lax.rsqrt(jnp.mean(x * x, axis=-1, keepdims=True) + EPS)


def _split_bf16(x, parts):
    out = []
    for _ in range(parts - 1):
        p = x.astype(BF16)
        out.append(p)
        x = x - p.astype(F32)
    out.append(x.astype(BF16))
    return out


def _ada_tile(c_ref, w_ref, b_ref):
    rows = c_ref.shape[0]
    ca = jnp.concatenate(_split_bf16(jax.nn.silu(c_ref[...]), 2), axis=0)
    acc = _dot(ca, w_ref[...].astype(BF16))
    return acc[:rows] + acc[rows:] + b_ref[...]


def _ada_kernel(c_ref, w_ref, b_ref, o_ref):
    o_ref[...] = _ada_tile(c_ref, w_ref, b_ref)


def _ada(c, w_ada, b_ada, ncols, tn=1024):
    _, d, _ = w_ada.shape
    rows = c.shape[0]
    return pl.pallas_call(
        _ada_kernel,
        grid=(ncols // tn,),
        in_specs=[pl.BlockSpec((rows, d), lambda j: (0, 0)),
                  pl.BlockSpec((None, d, tn), lambda j: (0, 0, j)),
                  pl.BlockSpec((1, tn), lambda j: (0, j))],
        out_specs=pl.BlockSpec((rows, tn), lambda j: (0, j)),
        out_shape=jax.ShapeDtypeStruct((rows, ncols), F32),
        compiler_params=pltpu.CompilerParams(dimension_semantics=("arbitrary",), vmem_limit_bytes=VMEM_LIMIT),
        name="ada",
    )(c, w_ada, b_ada)


def _gelu(x):
    hx = 0.5 * x
    return hx * jnp.tanh(x * (GELU_C1 + GELU_C3 * (x * x))) + hx


def _first_kernel(x_hbm, sh_ref, sc_ref, g_ref, w_ref, c_ref, wada_ref, bada_ref, zg_ref, h_ref, modb_ref, wb_ref,
                  xbuf_ref, xsem, *, ncol):
    i, r, j = pl.program_id(0), pl.program_id(1), pl.program_id(2)
    nr = pl.num_programs(1)
    tm = xbuf_ref.shape[1]
    t = i * nr + r
    total = pl.num_programs(0) * nr
    first = t == 0
    tn = zg_ref.shape[1]

    def x_copy(tile, slot):
        rows = pl.ds(pl.multiple_of((tile % nr) * tm, tm), tm)
        return pltpu.make_async_copy(x_hbm.at[tile // nr, rows, :], xbuf_ref.at[slot], xsem.at[slot])

    @pl.when(first & (j == 0))
    def _():
        x_copy(t, 0).start()

    @pl.when(j == 0)
    def _():
        x_copy(t, t % 2).wait()

    @pl.when((j == 0) & (t + 1 < total))
    def _():
        x_copy(t + 1, (t + 1) % 2).start()

    def tile(jj):
        modb_ref[...] = _ada_tile(c_ref, wada_ref, bada_ref)
        if jj == 0:
            row = pl.ds(i, 1)
            scale = g_ref[...] * (1.0 + sc_ref[row, :])
            h_ref[...] = (_rms(xbuf_ref[t % 2]) * scale + sh_ref[row, :]).astype(BF16)
        zg_ref[...] = _gelu(_dot(h_ref[...], wb_ref[jj])).astype(BF16)

    for jj in range(ncol):
        @pl.when(first & (j == jj))
        def _(jj=jj):
            wb_ref[jj] = w_ref[:, jj * tn:(jj + 1) * tn].astype(BF16)

        pl.when(j == jj)(functools.partial(tile, jj))


def _first(x, mod_a, g, w, c, w_ada, b_ada, ncol=2, tm=512, tn=1024):
    b, s, d = x.shape
    nr = s // tm
    steps = b * nr * ncol
    done = mod_a.shape[1]
    slab = (w_ada.shape[2] - done) // steps
    assert slab * steps == w_ada.shape[2] - done and slab % 128 == 0 and done % slab == 0
    step = lambda i, r, j: (i * nr + r) * ncol + j
    return pl.pallas_call(
        functools.partial(_first_kernel, ncol=ncol),
        grid=(b, nr, ncol),
        in_specs=[pl.BlockSpec(memory_space=pl.ANY),
                  pl.BlockSpec((b, d), lambda i, r, j: (0, SH1)),
                  pl.BlockSpec((b, d), lambda i, r, j: (0, SC1)),
                  pl.BlockSpec((1, d), lambda i, r, j: (0, 0)),
                  pl.BlockSpec((None, d, ncol * tn), lambda i, r, j: (0, 0, 0), pipeline_mode=pl.Buffered(1)),
                  pl.BlockSpec(c.shape, lambda i, r, j: (0, 0)),
                  pl.BlockSpec((None, d, slab), lambda i, r, j: (0, 0, done // slab + step(i, r, j))),
                  pl.BlockSpec((1, slab), lambda i, r, j: (0, done // slab + step(i, r, j)))],
        out_specs=[pl.BlockSpec((tm, tn), lambda i, r, j: (i * nr + r, j)),
                   pl.BlockSpec((tm, d), lambda i, r, j: (i * nr + r, 0)),
                   pl.BlockSpec((c.shape[0], slab), lambda i, r, j: (0, step(i, r, j)))],
        out_shape=[jax.ShapeDtypeStruct((b * s, ncol * tn), BF16), jax.ShapeDtypeStruct((b * s, d), BF16),
                   jax.ShapeDtypeStruct((c.shape[0], w_ada.shape[2] - done), F32)],
        scratch_shapes=[pltpu.VMEM((ncol, d, tn), BF16), pltpu.VMEM((2, tm, d), F32), pltpu.SemaphoreType.DMA((2,))],
        compiler_params=pltpu.CompilerParams(dimension_semantics=("arbitrary", "arbitrary", "arbitrary"),
                                             vmem_limit_bytes=VMEM_LIMIT),
        name="first",
    )(x, mod_a, mod_a, g, w, c, w_ada, b_ada)


IN_TILES = ((2, "silu"), (5, "silu"), (6, "gate"), (7, "gate"), (8, "gate"), (9, "gate"), (4, "none"), (3, "logf"))
Z_Q, Z_OG, Z_GATE, Z_IV = 0, 1, 2, 6


def _inproj_kernel(h_ref, w_ref, bg_ref, lb_ref, z_ref, lf_ref, wb_ref):
    j = pl.program_id(0)

    @pl.when(pl.program_id(1) == 0)
    def _():
        wb_ref[...] = w_ref[...].astype(BF16)

    def tile(act):
        acc = _dot(h_ref[...], wb_ref[...])
        if act == "silu":
            hx = 0.5 * acc
            z_ref[...] = (hx * jnp.tanh(hx) + hx).astype(BF16)
        elif act == "gate":
            z_ref[...] = (0.5 * jnp.tanh(0.5 * (acc + bg_ref[...])) + 0.5).astype(BF16)
        elif act == "none":
            z_ref[...] = acc.astype(BF16)
        else:
            lb = jax.nn.softmax(lb_ref[...], axis=0)[0:1, :]
            lf_ref[...] = jnp.log((0.5 + 0.5 * lb) + (0.5 - 0.5 * lb) * jnp.tanh(0.5 * acc))

    acts = [a for _, a in IN_TILES]
    for act in dict.fromkeys(acts):
        first = acts.index(act)
        last = len(acts) - 1 - acts[::-1].index(act)
        pl.when((j >= first) & (j <= last))(functools.partial(tile, act))


def _inproj(h, w, b_gate, hg_lb, tm=1024, tn=1024):
    m, d = h.shape
    ni, nj = m // tm, len(IN_TILES)
    acts = [a for _, a in IN_TILES]
    assert acts[-1] == "logf" and acts.count("logf") == 1
    gate0 = acts.index("gate")
    ngate = acts.count("gate")

    def wcol(j):
        col = j
        for k, (c, _) in enumerate(IN_TILES):
            col = jnp.where(j == k, c, col)
        return col

    last = nj - 1
    return pl.pallas_call(
        _inproj_kernel,
        grid=(nj, ni),
        in_specs=[pl.BlockSpec((tm, d), lambda j, i: (i, 0)),
                  pl.BlockSpec((None, d, tn), lambda j, i: (0, 0, wcol(j))),
                  pl.BlockSpec((1, tn), lambda j, i: (0, jnp.clip(j - gate0, 0, ngate - 1))),
                  pl.BlockSpec(hg_lb.shape, lambda j, i: (0, 0))],
        out_specs=[pl.BlockSpec((tm, tn), lambda j, i: (jnp.where(j < last, i, ni - 1), jnp.minimum(j, last - 1))),
                   pl.BlockSpec((tm, tn), lambda j, i: (jnp.where(j < last, 0, i), 0))],
        out_shape=[jax.ShapeDtypeStruct((m, (nj - 1) * tn), BF16), jax.ShapeDtypeStruct((m, tn), F32)],
        scratch_shapes=[pltpu.VMEM((d, tn), BF16)],
        compiler_params=pltpu.CompilerParams(dimension_semantics=("arbitrary", "arbitrary"),
                                             vmem_limit_bytes=VMEM_LIMIT),
        name="inproj",
    )(h, w, b_gate, hg_lb)


def _level_masks():
    t = np.arange(CHUNK)[:, None]
    s = np.arange(CHUNK)[None, :]
    out = [(t // (2 * m) == s // (2 * m)) & ((t // m) % 2 == 1) & ((s // m) % 2 == 0) for m in LEVELS]
    return np.stack(out).astype(np.float32)


def _decay_matrix():
    t = np.arange(CHUNK)[:, None]
    j = np.arange(CHUNK)[None, :]
    blocks = [j <= t]
    for m in MATMUL_LEVELS:
        r = (t // (2 * m)) * (2 * m) + m - 1
        blocks.append(np.where((t // m) % 2 == 1, (j > r) & (j <= t), (j > t) & (j <= r)))
    blocks.append(j > t)
    w = np.concatenate(blocks, axis=0).astype(np.float32)
    return np.concatenate([w, w], axis=1)


def _mixer_kernel(u_ref, v_ref, q_ref, og_ref, lf_ref, iv_ref, lng_ref, lnb_ref, ws_ref, bs_ref, ng_ref,
                  cw_ref, msk_ref, *rest, n_cast):
    src, (ya_ref, yb_ref), dst, st_ref = rest[:n_cast], rest[n_cast:n_cast + 2], rest[n_cast + 2:-1], rest[-1]
    rows = u_ref.shape[0]
    nchunk = rows // CHUNK
    ngroup = u_ref.shape[1] // GROUP

    @pl.when(pl.program_id(1) == 0)
    def _():
        st_ref[...] = jnp.zeros_like(st_ref)

    for s_ref, d_ref in zip(src, dst):
        d_ref[...] = s_ref[...].astype(BF16)

    v = v_ref[...].astype(F32)
    mu = jnp.mean(v, axis=-1, keepdims=True)
    vc = v - mu
    var = jnp.mean(vc * vc, axis=-1, keepdims=True)
    vn = (vc * lax.rsqrt(var + EPS) * lng_ref[...] + lnb_ref[...]).astype(BF16)
    tri = lax.broadcasted_iota(jnp.int32, (CHUNK, CHUNK), 0) >= lax.broadcasted_iota(jnp.int32, (CHUNK, CHUNK), 1)
    bst = bs_ref[...].T
    for g in range(ngroup):
        cols = slice(g * GROUP, (g + 1) * GROUP)
        w = jnp.where(tri, ws_ref[g], 0.0).astype(BF16)
        rhs = jnp.concatenate([vn[c * CHUNK:(c + 1) * CHUNK, cols] for c in range(nchunk)], axis=1)
        sg = _dot(w, rhs) + bst[:, g:g + 1]
        for c in range(nchunk):
            r = slice(c * CHUNK, (c + 1) * CHUNK)
            ya_ref[r, cols] = (u_ref[r, cols].astype(F32) * sg[:, c * CHUNK:(c + 1) * CHUNK]).astype(BF16)

    nlev = len(LEVELS)
    odd = lax.broadcasted_iota(jnp.int32, (CHUNK, GROUP), 0) % 2 == 1
    for c in range(nchunk):
        r = slice(c * CHUNK, (c + 1) * CHUNK)
        lf2 = lf_ref[r, :] * LOG2E
        ex = _dot(cw_ref[...], jnp.concatenate(_split_bf16(lf2, 2), axis=0))
        for h0 in range(0, ngroup, HEADS_PER_PASS):
            hs = range(h0, h0 + HEADS_PER_PASS)
            cs = {h: slice(h * GROUP, (h + 1) * GROUP) for h in hs}
            blk = lambda h, i: ex[i * CHUNK:(i + 1) * CHUNK, cs[h]]
            nmm = len(MATMUL_LEVELS)

            def level_decay(h, m):
                if m in MATMUL_LEVELS:
                    return jnp.exp2(blk(h, 1 + MATMUL_LEVELS.index(m))).astype(BF16)
                b3 = blk(h, 0).reshape(CHUNK // (2 * m), 2 * m, GROUP)
                dl = (b3 - b3[:, m - 1:m, :]).reshape(CHUNK, GROUP)
                return jnp.exp2(-jnp.abs(dl)).astype(BF16)

            qb = {h: q_ref[r, cs[h]] for h in hs}
            kb = {h: (1.0 - jnp.exp2(lf2[:, cs[h]])).astype(BF16) for h in hs}
            st = {h: st_ref[h] for h in hs}
            o = {h: _dot_nt(qb[h] * jnp.exp2(blk(h, 0)).astype(BF16), st[h].astype(BF16)) for h in hs}
            a = {}
            for li in range(nlev):
                for h in hs:
                    e = level_decay(h, LEVELS[li])
                    p = msk_ref[li] * _dot_nt(qb[h] * e, kb[h] * e)
                    a[h] = p if li == 0 else a[h] + p
            for h in hs:
                iv = iv_ref[r, cs[h]]
                ivf = iv.astype(F32)
                qf = qb[h].astype(F32)
                fh = jnp.exp2(lf2[:, cs[h]])
                kf = 1.0 - fh
                c0 = jnp.sum(qf * kf, axis=-1, keepdims=True)
                c1 = jnp.sum(jnp.where(odd, qf * fh * pltpu.roll(kf, 1, 0), 0.0), axis=-1, keepdims=True)
                oh = o[h] + _dot(a[h].astype(BF16), iv) + c0 * ivf + c1 * pltpu.roll(ivf, 1, 0)
                b_last = ex[CHUNK - 1:CHUNK, cs[h]]
                st_ref[h] = st[h] * jnp.exp2(b_last) + _dot_tn(iv, kb[h] * jnp.exp2(blk(h, 1 + nmm)).astype(BF16))
                y = _rms(oh) * ng_ref[:, cs[h]] * og_ref[r, cs[h]].astype(F32)
                yb_ref[r, cs[h]] = y.astype(BF16)


def _mixer(zg, z, lf, ln_g, ln_b, ws, bs, norm_g, batch, casts, rows=256):
    m, width = lf.shape
    nr = m // batch // rows
    nsteps = batch * nr
    ngroup = width // GROUP
    cw = jnp.asarray(_decay_matrix(), BF16)
    msk = jnp.asarray(_level_masks(), F32)
    row_blk = lambda col: pl.BlockSpec((rows, width), lambda b, r: (b * nr + r, col))
    full = lambda a: pl.BlockSpec(a.shape, lambda b, r: (0,) * a.ndim)
    in_specs = [row_blk(0), row_blk(1), row_blk(Z_Q), row_blk(Z_OG), row_blk(0), row_blk(Z_IV),
                full(ln_g), full(ln_b), full(ws), full(bs), full(norm_g), full(cw), full(msk)]
    out_specs = [row_blk(0), row_blk(0)]
    out_shape = [jax.ShapeDtypeStruct((m, width), BF16), jax.ShapeDtypeStruct((m, width), BF16)]
    for a in casts:
        _, n, w = a.shape
        slab = n // nsteps
        assert slab * nsteps == n and slab % 16 == 0
        in_specs.append(pl.BlockSpec((None, slab, w), lambda b, r: (0, b * nr + r, 0)))
        out_specs.append(pl.BlockSpec((slab, w), lambda b, r: (b * nr + r, 0)))
        out_shape.append(jax.ShapeDtypeStruct((n, w), BF16))
    return pl.pallas_call(
        functools.partial(_mixer_kernel, n_cast=len(casts)),
        grid=(batch, nr),
        in_specs=in_specs,
        out_specs=out_specs,
        out_shape=out_shape,
        scratch_shapes=[pltpu.VMEM((ngroup, GROUP, GROUP), F32)],
        compiler_params=pltpu.CompilerParams(dimension_semantics=("arbitrary", "arbitrary"),
                                             vmem_limit_bytes=VMEM_LIMIT),
        name="mixer",
    )(zg, zg, z, z, lf, z, ln_g, ln_b, ws, bs, norm_g, cw, msk, *casts)


def _merge_kernel(ya_ref, yb_ref, ga_ref, gb_ref, x_ref, gt_ref, sh_ref, sc_ref, wa_ref, wb_ref, wo_ref, g_ref, x1_ref,
                  h2_ref):
    tm = ya_ref.shape[0]
    row = pl.ds(pl.program_id(0), 1)
    scale = g_ref[...] * (1.0 + sc_ref[row, :])
    gate, shift = gt_ref[row, :], sh_ref[row, :]
    for p in range(MERGE_PARTS):
        r = slice(p * tm // MERGE_PARTS, (p + 1) * tm // MERGE_PARTS)
        pa = _dot(ya_ref[r, :], wa_ref[...])
        pb = _dot(yb_ref[r, :], wb_ref[...])
        y = (ga_ref[r, :].astype(F32) * pa + gb_ref[r, :].astype(F32) * pb).astype(BF16)
        x1 = x_ref[0, r, :] + gate * _dot(y, wo_ref[...])
        x1_ref[0, r, :] = x1
        h2_ref[0, r, :] = (_rms(x1) * scale + shift).astype(BF16)


def _merge(ya, yb, z, x, mod_b, wa, wb, wo, g, tm=512):
    b, s, d = x.shape
    nr = s // tm
    width = ya.shape[1]
    resident = lambda a: pl.BlockSpec(a.shape, lambda i, r: (0,) * a.ndim, pipeline_mode=pl.Buffered(1))
    return pl.pallas_call(
        _merge_kernel,
        grid=(b, nr),
        in_specs=[pl.BlockSpec((tm, width), lambda i, r: (i * nr + r, 0)),
                  pl.BlockSpec((tm, width), lambda i, r: (i * nr + r, 0)),
                  pl.BlockSpec((tm, d), lambda i, r: (i * nr + r, Z_GATE * width // d)),
                  pl.BlockSpec((tm, d), lambda i, r: (i * nr + r, Z_GATE * width // d + 1)),
                  pl.BlockSpec((1, tm, d), lambda i, r: (i, r, 0)),
                  pl.BlockSpec((b, d), lambda i, r: (0, GT1)),
                  pl.BlockSpec((b, d), lambda i, r: (0, SH2)),
                  pl.BlockSpec((b, d), lambda i, r: (0, SC2)),
                  resident(wa), resident(wb), resident(wo),
                  pl.BlockSpec((1, d), lambda i, r: (0, 0))],
        out_specs=[pl.BlockSpec((1, tm, d), lambda i, r: (i, r, 0)),
                   pl.BlockSpec((1, tm, d), lambda i, r: (i, r, 0))],
        out_shape=[jax.ShapeDtypeStruct((b, s, d), F32), jax.ShapeDtypeStruct((b, s, d), BF16)],
        compiler_params=pltpu.CompilerParams(dimension_semantics=("arbitrary", "arbitrary"),
                                             vmem_limit_bytes=VMEM_LIMIT),
        name="merge",
    )(ya, yb, z, z, x, mod_b, mod_b, mod_b, wa, wb, wo, g)


def _ffn_kernel(h_ref, wa_ref, wu_ref, wo_ref, x1_hbm, mod_ref, g_ref, o_ref, acc_ref, x1buf_ref, x1sem, *, nt):
    assert nt > 4
    t = pl.program_id(2)
    half = o_ref.shape[1]

    def x1_copy(k):
        rows = pl.ds(pl.multiple_of((2 * pl.program_id(1) + k) * half, half), half)
        return pltpu.make_async_copy(x1_hbm.at[pl.program_id(0), rows, :], x1buf_ref.at[k], x1sem.at[k])

    for k in range(2):
        @pl.when(t == 2 * k + 2)
        def _(k=k):
            x1_copy(k).start()

        @pl.when(t == nt + k)
        def _(k=k):
            x1_copy(k).wait()

    def hidden_tile():
        h = h_ref[0]
        a = _dot(h, wa_ref[...])
        up = _dot(h, wu_ref[...])
        ha = 0.5 * a
        return _dot(((ha * jnp.tanh(ha) + ha) * up).astype(BF16), wo_ref[...])

    @pl.when(t == 0)
    def _():
        acc_ref[...] = hidden_tile()

    @pl.when((t > 0) & (t < nt))
    def _():
        acc_ref[...] += hidden_tile()

    @pl.when(t >= nt)
    def _():
        base = (t - nt) * half
        gate = mod_ref[pl.ds(pl.program_id(0), 1), :]
        g = g_ref[...]

        def body(i, carry):
            rows = pl.ds(pl.multiple_of(i * ROW_GROUP, ROW_GROUP), ROW_GROUP)
            arows = pl.ds(pl.multiple_of(base + i * ROW_GROUP, ROW_GROUP), ROW_GROUP)
            o_ref[0, rows, :] = _rms(x1buf_ref[t - nt, rows, :] + gate * acc_ref[arows, :]) * g
            return carry

        lax.fori_loop(0, half // ROW_GROUP, body, 0, unroll=8)


def _ffn(h2, w_in, w_out, x1, mod_b, g, tm=1024, th=512):
    b, s, d = x1.shape
    hidden = w_out.shape[0]
    nt = hidden // th
    half = tm // 2
    nr = s // tm
    wt = lambda t: jnp.where(t < nt, t, 0)
    fin = lambda r, t: 2 * r + jnp.clip(t - nt, 0, 1)

    def h2_blk(i, r, t):
        lin = jnp.minimum(i * nr + r + (t >= nt).astype(jnp.int32), b * nr - 1)
        return (lin // nr, lin % nr, 0)

    return pl.pallas_call(
        functools.partial(_ffn_kernel, nt=nt),
        grid=(b, s // tm, nt + 2),
        in_specs=[pl.BlockSpec((1, tm, d), h2_blk),
                  pl.BlockSpec((d, th), lambda i, r, t: (0, wt(t))),
                  pl.BlockSpec((d, th), lambda i, r, t: (0, nt + wt(t))),
                  pl.BlockSpec((th, d), lambda i, r, t: (wt(t), 0)),
                  pl.BlockSpec(memory_space=pl.ANY),
                  pl.BlockSpec((b, d), lambda i, r, t: (0, GT2)),
                  pl.BlockSpec((1, d), lambda i, r, t: (0, 0))],
        out_specs=pl.BlockSpec((1, half, d), lambda i, r, t: (i, fin(r, t), 0)),
        out_shape=jax.ShapeDtypeStruct((b, s, d), F32),
        scratch_shapes=[pltpu.VMEM((tm, d), F32), pltpu.VMEM((2, half, d), F32), pltpu.SemaphoreType.DMA((2,))],
        compiler_params=pltpu.CompilerParams(dimension_semantics=("arbitrary", "arbitrary", "arbitrary"),
                                             vmem_limit_bytes=VMEM_LIMIT),
        name="ffn",
    )(h2, w_in, w_in, w_out, x1, mod_b, g)


def kernel(x, c, w_ada, b_ada, norm1_g, w_in, b_gate, gmlp_ln_g, gmlp_ln_b, gmlp_ws, gmlp_bs, hg_lb, hg_norm_g,
           w_branch_gmlp, w_branch_hg, w_out, norm2_g, w_ffn_in, w_ffn_out, final_norm_g):
    batch, seq, d = x.shape
    depth = w_ada.shape[0]
    width = w_branch_gmlp.shape[1]
    assert depth == 1 and width == 8 * GROUP and gmlp_ws.shape[2] == CHUNK and seq % 256 == 0
    assert w_in.shape[2] == 6 * width + 2 * d and hg_lb.shape[0] == depth + 1
    tn = 1024
    assert width == tn and d == 2 * tn

    mod_a = _ada(c, w_ada, b_ada, 2 * d)
    zg, h1, mod_b = _first(x, mod_a, norm1_g, w_in, c, w_ada, b_ada)
    z, lf = _inproj(h1, w_in, b_gate, hg_lb)

    ya, yb, w_a, w_b, w_o, w_fi, w_fo = _mixer(
        zg, z, lf, gmlp_ln_g, gmlp_ln_b, gmlp_ws[0], gmlp_bs[0], hg_norm_g, batch,
        casts=(w_branch_gmlp, w_branch_hg, w_out, w_ffn_in, w_ffn_out))

    x1, h2 = _merge(ya, yb, z, x, mod_b, w_a, w_b, w_o, norm2_g)
    return _ffn(h2, w_fi, w_fo, x1, mod_b, final_norm_g.reshape(1, d))
```
